```python
import math
import jax, jax.numpy as jnp
from jax import lax
import numpy as np

D_MODEL = 2048
BATCH = 4
SEQ = 2048
DEPTH = 1
DEC_BATCH = 128
DEC_SEQ = 1
PAST_LEN = 16384
PAGE_SIZE = 128

POOL_WINDOWS = (2, 4, 8, 16)
N_POOL_GROUPS = len(POOL_WINDOWS)
POOL_GROUP_DIM = D_MODEL // 8
POOL_DIM = N_POOL_GROUPS * POOL_GROUP_DIM
POOL_BUF = max(POOL_WINDOWS) - 1
GDN_HEAD_DIM = 128
GDN_HEADS = D_MODEL // GDN_HEAD_DIM
GDN_DIM = GDN_HEADS * GDN_HEAD_DIM
QKV_DIM = 3 * GDN_DIM
CONV_WIDTH = 4
CONV_BUF = CONV_WIDTH - 1
CHUNK = 64
D_FF = -(-8 * D_MODEL // (3 * 256)) * 256
PLE_DIM = 256
EPS = 1e-6
POOL_OFF = 0
QKV_OFF = POOL_OFF + POOL_DIM
Z_OFF = QKV_OFF + QKV_DIM
A_OFF = Z_OFF + GDN_DIM
B_OFF = A_OFF + GDN_HEADS
GP_OFF = B_OFF + GDN_HEADS
GG_OFF = GP_OFF + D_MODEL
IN_COLS = GG_OFF + D_MODEL

kernel_name = "pool_gdn_gated_hybrid_step"


def rmsnorm(x, g):
    xf = x.astype(jnp.float32)
    y = xf * lax.rsqrt(jnp.mean(xf * xf, axis=-1, keepdims=True) + EPS)
    return (y * g.astype(jnp.float32)).astype(x.dtype)


def l2norm(x):
    xf = x.astype(jnp.float32)
    return xf * lax.rsqrt(jnp.sum(xf * xf, axis=-1, keepdims=True) + EPS)


def pool_mixer(u, prefix, start_pos, w_grp, scale):
    Bx, T, _ = u.shape
    ext = jnp.concatenate([prefix.astype(u.dtype), u], axis=1)
    ef = ext.astype(jnp.float32)
    cs = jnp.concatenate([jnp.zeros_like(ef[:, :1]), jnp.cumsum(ef, axis=1)], axis=1)
    end = cs[:, POOL_BUF + 1:POOL_BUF + 1 + T]
    pos = start_pos + jnp.arange(T)
    means = []
    for gi, w in enumerate(POOL_WINDOWS):
        sl = slice(gi * POOL_GROUP_DIM, (gi + 1) * POOL_GROUP_DIM)
        start = cs[:, POOL_BUF + 1 - w:POOL_BUF + 1 - w + T, sl]
        cnt = jnp.minimum(pos + 1, w).astype(jnp.float32)[None, :, None]
        means.append((end[..., sl] - start) / cnt)
    d = jnp.concatenate(means, axis=-1) - u.astype(jnp.float32)
    d = d.reshape(Bx, T, N_POOL_GROUPS, POOL_GROUP_DIM)
    y = jnp.einsum("btgc,gcd->btgd", d, w_grp.astype(jnp.float32)).reshape(Bx, T, POOL_DIM)
    y = y * scale.astype(jnp.float32)
    return y.astype(u.dtype), ext[:, -POOL_BUF:]


def short_conv(u, prefix, w):
    T = u.shape[1]
    ext = jnp.concatenate([prefix.astype(u.dtype), u], axis=1)
    y = ext[:, 0:T] * w[0]
    for i in range(1, CONV_WIDTH):
        y = y + ext[:, i:i + T] * w[i]
    return jax.nn.silu(y), ext[:, -CONV_BUF:]


def gated_delta_chunked(q, k, v, g, beta, s0):
    f32 = jnp.float32
    Bx, T, H, DK = q.shape
    DV = v.shape[-1]
    C = min(CHUNK, T)
    n = -(-T // C)
    pad = n * C - T

    def prep(a):
        a = jnp.pad(a.astype(f32), [(0, 0), (0, pad)] + [(0, 0)] * (a.ndim - 2))
        a = a.reshape((Bx, n, C) + a.shape[2:])
        return jnp.moveaxis(a, 3, 1)

    q, k, v, g, beta = prep(q), prep(k), prep(v), prep(g), prep(beta)
    gc = jnp.cumsum(g, axis=-1)
    causal = jnp.tril(jnp.ones((C, C), bool))
    strict = jnp.tril(jnp.ones((C, C), bool), -1)
    decay = jnp.exp(jnp.where(causal, gc[..., :, None] - gc[..., None, :], -jnp.inf))
    kb = k * beta[..., None]
    a_mat = jnp.where(strict, jnp.einsum("bhncd,bhnsd->bhncs", kb, k) * decay, 0.0)
    eg = jnp.exp(gc)
    rhs = jnp.concatenate([v * beta[..., None], kb * eg[..., None]], axis=-1)
    sol = lax.linalg.triangular_solve(a_mat + jnp.eye(C, dtype=f32), rhs,
                                      left_side=True, lower=True, unit_diagonal=True)
    u_val, w_dec = sol[..., :DV], sol[..., DV:]
    qk = jnp.where(causal, jnp.einsum("bhncd,bhnsd->bhncs", q, k) * decay, 0.0)
    q_dec = q * eg[..., None]
    k_tail = k * jnp.exp(gc[..., -1:] - gc)[..., None]
    g_last = jnp.exp(gc[..., -1])
    xs = tuple(jnp.moveaxis(a, 2, 0) for a in (u_val, w_dec, qk, q_dec, k_tail, g_last))

    def step(S, inp):
        u_c, w_c, qk_c, qd_c, kt_c, gl_c = inp
        v_new = u_c - jnp.einsum("bhcd,bhde->bhce", w_c, S)
        o = jnp.einsum("bhcd,bhde->bhce", qd_c, S) + jnp.einsum("bhcs,bhse->bhce", qk_c, v_new)
        S = S * gl_c[..., None, None] + jnp.einsum("bhcd,bhce->bhde", kt_c, v_new)
        return S, o

    S, o = lax.scan(step, s0.astype(f32), xs)
    o = jnp.moveaxis(o, 0, 2).reshape(Bx, H, n * C, DV)[:, :, :T]
    return jnp.transpose(o, (0, 2, 1, 3)), S


def _layer(x, p_l, pool_buf, conv_buf, ssm, start_pos,
           norm_mix, w_in, pool_w, pool_scale, conv_w, a_log, dt_bias, gdn_norm,
           w_pool_up, w_gdn_up, w_o, norm_ffn, w_gate_up, w_down, w_ple, w_ple_gate):
    f32 = jnp.float32
    Bx, T, _ = x.shape
    h = rmsnorm(x, norm_mix)
    proj = h @ w_in
    gate_pool = jax.nn.sigmoid(proj[..., GP_OFF:GP_OFF + D_MODEL])
    gate_gdn = jax.nn.sigmoid(proj[..., GG_OFF:GG_OFF + D_MODEL])
    pool_out, pool_new = pool_mixer(proj[..., POOL_OFF:POOL_OFF + POOL_DIM], pool_buf,
                                    start_pos, pool_w, pool_scale)
    qkv, conv_new = short_conv(proj[..., QKV_OFF:QKV_OFF + QKV_DIM], conv_buf, conv_w)
    qkv = qkv.reshape(Bx, T, 3, GDN_HEADS, GDN_HEAD_DIM)
    q = l2norm(qkv[:, :, 0]) * (GDN_HEAD_DIM ** -0.5)
    k = l2norm(qkv[:, :, 1])
    v = qkv[:, :, 2]
    beta = jax.nn.sigmoid(proj[..., B_OFF:B_OFF + GDN_HEADS].astype(f32))
    g = -jnp.exp(a_log.astype(f32)) * jax.nn.softplus(
        proj[..., A_OFF:A_OFF + GDN_HEADS].astype(f32) + dt_bias.astype(f32))
    o, ssm_new = gated_delta_chunked(q, k, v, g, beta, ssm)
    z = proj[..., Z_OFF:Z_OFF + GDN_DIM].reshape(Bx, T, GDN_HEADS, GDN_HEAD_DIM).astype(f32)
    o = rmsnorm(o, gdn_norm) * jax.nn.silu(z)
    gdn_out = o.reshape(Bx, T, GDN_DIM).astype(x.dtype)
    merged = gate_pool * (pool_out @ w_pool_up) + gate_gdn * (gdn_out @ w_gdn_up)
    x = x + merged @ w_o
    gu = rmsnorm(x, norm_ffn) @ w_gate_up
    x = x + (jax.nn.silu(gu[..., :D_FF]) * gu[..., D_FF:]) @ w_down
    x = x + (p_l @ w_ple) * jax.nn.sigmoid(x @ w_ple_gate)
    return x, pool_new, conv_new, ssm_new.astype(x.dtype)


def _trunk(x, p, pool_st, conv_st, ssm_st, start_pos, layer_w, norm_final):
    pools, convs, ssms = [], [], []
    for i in range(DEPTH):
        x, pb, cb, sb = _layer(x, p[i], pool_st[i], conv_st[i], ssm_st[i], start_pos,
                               *[w[i] for w in layer_w])
        pools.append(pb)
        convs.append(cb)
        ssms.append(sb)
    return rmsnorm(x, norm_final), jnp.stack(pools), jnp.stack(convs), jnp.stack(ssms)


def setup_inputs(seed: int = 0) -> dict:
    key = jax.random.key(seed)
    ks = jax.random.split(key, 24)
    nrm = lambda k, s, sc: jax.random.normal(k, s, jnp.float32) * sc
    L = DEPTH
    u = jax.random.uniform(ks[10], (L, GDN_HEADS), jnp.float32)
    dt = jnp.exp(u * (math.log(0.1) - math.log(1e-3)) + math.log(1e-3))
    return {
        "x_prompt": nrm(ks[0], (BATCH, SEQ, D_MODEL), 1.0),
        "x_sample": nrm(ks[1], (DEC_BATCH, DEC_SEQ, D_MODEL), 1.0),
        "p_prompt": nrm(ks[2], (L, BATCH, SEQ, PLE_DIM), 1.0),
        "p_sample": nrm(ks[3], (L, DEC_BATCH, DEC_SEQ, PLE_DIM), 1.0),
        "state_pool": nrm(ks[4], (L, DEC_BATCH, POOL_BUF, POOL_DIM), 1.0),
        "state_conv": nrm(ks[5], (L, DEC_BATCH, CONV_BUF, QKV_DIM), 1.0),
        "state_ssm": nrm(ks[6], (L, DEC_BATCH, GDN_HEADS, GDN_HEAD_DIM, GDN_HEAD_DIM), 0.1),
        "norm_mix": 1.0 + nrm(ks[7], (L, D_MODEL), 0.02),
        "w_in": nrm(ks[8], (L, D_MODEL, IN_COLS), D_MODEL ** -0.5),
        "pool_w": nrm(ks[9], (L, N_POOL_GROUPS, POOL_GROUP_DIM, POOL_GROUP_DIM), POOL_GROUP_DIM ** -0.5),
        "pool_scale": 1.0 + nrm(ks[11], (L, POOL_DIM), 0.02),
        "conv_w": nrm(ks[12], (L, CONV_WIDTH, QKV_DIM), CONV_WIDTH ** -0.5),
        "a_log": jnp.log(jax.random.uniform(ks[13], (L, GDN_HEADS), jnp.float32, 1.0, 16.0)),
        "dt_bias": dt + jnp.log(-jnp.expm1(-dt)),
        "gdn_norm": 1.0 + nrm(ks[14], (L, GDN_HEAD_DIM), 0.02),
        "w_pool_up": nrm(ks[15], (L, POOL_DIM, D_MODEL), POOL_DIM ** -0.5),
        "w_gdn_up": nrm(ks[16], (L, GDN_DIM, D_MODEL), GDN_DIM ** -0.5),
        "w_o": nrm(ks[17], (L, D_MODEL, D_MODEL), D_MODEL ** -0.5),
        "norm_ffn": 1.0 + nrm(ks[18], (L, D_MODEL), 0.02),
        "w_gate_up": nrm(ks[19], (L, D_MODEL, 2 * D_FF), D_MODEL ** -0.5),
        "w_down": nrm(ks[20], (L, D_FF, D_MODEL), D_FF ** -0.5),
        "w_ple": nrm(ks[21], (L, PLE_DIM, D_MODEL), PLE_DIM ** -0.5),
        "w_ple_gate": nrm(ks[22], (L, D_MODEL, D_MODEL), D_MODEL ** -0.5),
        "norm_final": 1.0 + nrm(ks[23], (D_MODEL,), 0.02),
    }


def reference(x_prompt, x_sample, p_prompt, p_sample, state_pool, state_conv, state_ssm,
              norm_mix, w_in, pool_w, pool_scale, conv_w, a_log, dt_bias, gdn_norm,
              w_pool_up, w_gdn_up, w_o, norm_ffn, w_gate_up, w_down, w_ple, w_ple_gate,
              norm_final):
    layer_w = (norm_mix, w_in, pool_w, pool_scale, conv_w, a_log, dt_bias, gdn_norm,
               w_pool_up, w_gdn_up, w_o, norm_ffn, w_gate_up, w_down, w_ple, w_ple_gate)
    dt_ = x_prompt.dtype
    zero_pool = jnp.zeros((DEPTH, BATCH, POOL_BUF, POOL_DIM), dt_)
    zero_conv = jnp.zeros((DEPTH, BATCH, CONV_BUF, QKV_DIM), dt_)
    zero_ssm = jnp.zeros((DEPTH, BATCH, GDN_HEADS, GDN_HEAD_DIM, GDN_HEAD_DIM), dt_)
    y_prompt, pool_p, conv_p, ssm_p = _trunk(x_prompt, p_prompt, zero_pool, zero_conv, zero_ssm,
                                             0, layer_w, norm_final)
    y_sample, pool_s, conv_s, ssm_s = _trunk(x_sample, p_sample, state_pool, state_conv, state_ssm,
                                             PAST_LEN, layer_w, norm_final)
    return (y_prompt, y_sample, pool_p, conv_p, ssm_p, pool_s, conv_s, ssm_s)
```

```python
import functools

import jax
import jax.numpy as jnp
from jax import lax
from jax.experimental import pallas as pl
from jax.experimental.pallas import tpu as pltpu

F32 = jnp.float32
BF16 = jnp.bfloat16

D_MODEL = 2048
PAST_LEN = 16384
POOL_WINDOWS = (2, 4, 8, 16)
POOL_GROUP_DIM = D_MODEL // 8
POOL_DIM = len(POOL_WINDOWS) * POOL_GROUP_DIM
POOL_BUF = max(POOL_WINDOWS) - 1
HEAD_DIM = 128
HEADS = D_MODEL // HEAD_DIM
GDN_DIM = HEADS * HEAD_DIM
QKV_DIM = 3 * GDN_DIM
CONV_WIDTH = 4
CONV_BUF = CONV_WIDTH - 1
D_FF = -(-8 * D_MODEL // (3 * 256)) * 256
PLE_DIM = 256
EPS = 1e-6

IN_QKV = POOL_DIM
IN_Z = IN_QKV + QKV_DIM
IN_AB = IN_Z + GDN_DIM
IN_GATES = IN_AB + 2 * HEADS
C_GP = 0
C_GG = C_GP + D_MODEL
C_Q = C_GG + D_MODEL
C_K = C_Q + GDN_DIM
C_V = C_K + GDN_DIM
C_Z = C_V + GDN_DIM
C_POOL = C_Z + GDN_DIM
C_END = C_POOL + POOL_DIM

LANES = 128
SUBLANES = 8
VMEM_LIMIT_BYTES = 56 * 1024 * 1024
CHUNK = 128


def _cparams(*sem):
    return pltpu.CompilerParams(dimension_semantics=sem, vmem_limit_bytes=VMEM_LIMIT_BYTES)


def _rms(x, gain):
    return x * lax.rsqrt(jnp.mean(x * x, axis=-1, keepdims=True) + EPS) * gain


def _dot(a, b):
    return jnp.dot(a, b, preferred_element_type=F32)


def _dot_nt(a, b):
    return lax.dot_general(a, b, (((1,), (1,)), ((), ())), preferred_element_type=F32)


def _silu(x):
    return x * jax.nn.sigmoid(x)


def _inproj_body(x_ref, gain_ref, w_ref, wab_ref, o_ref, ab_ref, h_ref, *, n_gate_tiles):
    j = pl.program_id(1)

    @pl.when(j == 0)
    def _():
        h = _rms(x_ref[...], gain_ref[...]).astype(BF16)
        h_ref[...] = h
        ab_ref[...] = _dot(h, wab_ref[...].astype(BF16))

    acc = _dot(h_ref[...], w_ref[...])

    @pl.when(j < n_gate_tiles)
    def _():
        o_ref[...] = jax.nn.sigmoid(acc)

    @pl.when(j >= n_gate_tiles)
    def _():
        o_ref[...] = acc


def _inproj(x, gain, w_cat, w_in, *, tm, tn=512):
    m = x.shape[0]
    n = w_cat.shape[1]
    body = functools.partial(_inproj_body, n_gate_tiles=(2 * D_MODEL) // tn)
    return pl.pallas_call(
        body,
        grid=(m // tm, n // tn),
        in_specs=[
            pl.BlockSpec((tm, D_MODEL), lambda i, j: (i, 0)),
            pl.BlockSpec((1, D_MODEL), lambda i, j: (0, 0)),
            pl.BlockSpec((D_MODEL, tn), lambda i, j: (0, j)),
            pl.BlockSpec((D_MODEL, LANES), lambda i, j: (0, IN_AB // LANES)),
        ],
        out_specs=[
            pl.BlockSpec((tm, tn), lambda i, j: (i, j)),
            pl.BlockSpec((tm, LANES), lambda i, j: (i, 0)),
        ],
        out_shape=[jax.ShapeDtypeStruct((m, n), F32), jax.ShapeDtypeStruct((m, LANES), F32)],
        scratch_shapes=[pltpu.VMEM((tm, D_MODEL), BF16)],
        compiler_params=_cparams("parallel", "arbitrary"),
        name="inproj",
    )(x, gain, w_cat, w_in)


def _pool_group_out(mean, tok, pw_ref, scale_ref, gi):
    cols = slice(gi * POOL_GROUP_DIM, (gi + 1) * POOL_GROUP_DIM)
    y = _dot((mean - tok).astype(BF16), pw_ref[gi])
    return (y * scale_ref[:, cols]).astype(BF16)


def _pool_prompt_body(u_ref, pw_ref, scale_ref, o_ref, ext_ref, *, tt, start_pos):
    t = pl.program_id(1)
    halo = POOL_BUF + 1

    @pl.when(t == 0)
    def _():
        ext_ref[0:halo, :] = jnp.zeros((halo, POOL_DIM), F32)

    ext_ref[halo:halo + tt, :] = u_ref[...]
    pos = start_pos + t * tt + lax.broadcasted_iota(jnp.int32, (tt, 1), 0)
    for gi, w in enumerate(POOL_WINDOWS):
        cols = slice(gi * POOL_GROUP_DIM, (gi + 1) * POOL_GROUP_DIM)
        tok = ext_ref[halo:halo + tt, cols]
        tot = tok
        for i in range(1, w):
            tot = tot + ext_ref[pl.ds(halo - i, tt), cols]
        cnt = jnp.minimum(pos + 1, w).astype(F32)
        o_ref[:, cols] = _pool_group_out(tot / cnt, tok, pw_ref, scale_ref, gi)
    ext_ref[0:halo, :] = ext_ref[tt:tt + halo, :]


def _pool_prompt(proj, pool_w, pool_scale, *, batch, seq, tt):
    nt = seq // tt
    body = functools.partial(_pool_prompt_body, tt=tt, start_pos=0)
    return pl.pallas_call(
        body,
        grid=(batch, nt),
        in_specs=[
            pl.BlockSpec((tt, POOL_DIM), lambda b, t: (b * nt + t, C_POOL // POOL_DIM)),
            pl.BlockSpec((len(POOL_WINDOWS), POOL_GROUP_DIM, POOL_GROUP_DIM), lambda b, t: (0, 0, 0)),
            pl.BlockSpec((1, POOL_DIM), lambda b, t: (0, 0)),
        ],
        out_specs=pl.BlockSpec((tt, POOL_DIM), lambda b, t: (b * nt + t, 0)),
        out_shape=jax.ShapeDtypeStruct((batch * seq, POOL_DIM), BF16),
        scratch_shapes=[pltpu.VMEM((POOL_BUF + 1 + tt, POOL_DIM), F32)],
        compiler_params=_cparams("parallel", "arbitrary"),
        name="pool_prompt",
    )(proj, pool_w, pool_scale)


def _pool_sample_body(u_ref, st_ref, pw_ref, scale_ref, o_ref, *, start_pos):
    for gi, w in enumerate(POOL_WINDOWS):
        cols = slice(gi * POOL_GROUP_DIM, (gi + 1) * POOL_GROUP_DIM)
        tok = u_ref[:, cols]
        tot = tok
        for i in range(1, w):
            tot = tot + st_ref[POOL_BUF - i, :, cols]
        cnt = float(min(start_pos + 1, w))
        o_ref[:, cols] = _pool_group_out(tot / cnt, tok, pw_ref, scale_ref, gi)


def _pool_sample(proj, state_t, pool_w, pool_scale):
    nb = proj.shape[0]
    body = functools.partial(_pool_sample_body, start_pos=PAST_LEN)
    return pl.pallas_call(
        body,
        grid=(1,),
        in_specs=[
            pl.BlockSpec((nb, POOL_DIM), lambda i: (0, C_POOL // POOL_DIM)),
            pl.BlockSpec((POOL_BUF, nb, POOL_DIM), lambda i: (0, 0, 0)),
            pl.BlockSpec((len(POOL_WINDOWS), POOL_GROUP_DIM, POOL_GROUP_DIM), lambda i: (0, 0, 0)),
            pl.BlockSpec((1, POOL_DIM), lambda i: (0, 0)),
        ],
        out_specs=pl.BlockSpec((nb, POOL_DIM), lambda i: (0, 0)),
        out_shape=jax.ShapeDtypeStruct((nb, POOL_DIM), BF16),
        compiler_params=_cparams("arbitrary"),
        name="pool_sample",
    )(proj, state_t, pool_w, pool_scale)


def _split3(x):
    hi = x.astype(BF16)
    r1 = x - hi.astype(F32)
    mid = r1.astype(BF16)
    lo = (r1 - mid.astype(F32)).astype(BF16)
    return hi, mid, lo


def _split2(x):
    hi = x.astype(BF16)
    return hi, (x - hi.astype(F32)).astype(BF16)


def _softplus(x):
    return jnp.maximum(x, 0.0) + jnp.log1p(jnp.exp(-jnp.abs(x)))


def _unit_lower_inverse_minus_identity(a):
    q = -a
    rp = q
    levels = CHUNK.bit_length() - 1
    qb = q.astype(BF16)
    q = _dot(qb, qb)
    for j in range(1, levels):
        qb = q.astype(BF16)
        if j < levels - 1:
            p = _dot(qb, jnp.concatenate([q, rp], axis=1).astype(BF16))
            q_next, qr = p[:, :CHUNK], p[:, CHUNK:]
        else:
            q_next, qr = None, _dot(qb, rp.astype(BF16))
        rp = rp + q + qr
        q = q_next
    return rp


def _gdn_prompt_body(q_ref, k_ref, v_ref, z_ref, ab_ref, cwq_ref, cwk_ref, cwv_ref, alog_ref, dtb_ref, gn_ref,
                     o_ref, ssm_ref, ext_ref, s_ref, gct_ref, *, hg, tt):
    hgi = pl.program_id(1)
    t = pl.program_id(2)
    w = hg * HEAD_DIM
    halo = SUBLANES

    @pl.when(t == 0)
    def _():
        ext_ref[0:halo, :] = jnp.zeros((halo, 3 * w), F32)
        s_ref[...] = jnp.zeros_like(s_ref)

    ext_ref[halo:halo + tt, 0:w] = q_ref[...]
    ext_ref[halo:halo + tt, w:2 * w] = k_ref[...]
    ext_ref[halo:halo + tt, 2 * w:3 * w] = v_ref[...]

    def conv(part, cw_ref):
        cols = slice(part * w, (part + 1) * w)
        acc = ext_ref[pl.ds(halo - CONV_BUF, tt), cols] * cw_ref[0:1, :]
        for i in range(1, CONV_WIDTH):
            acc = acc + ext_ref[pl.ds(halo - CONV_BUF + i, tt), cols] * cw_ref[i:i + 1, :]
        return _silu(acc)

    qc = conv(0, cwq_ref)
    kc = conv(1, cwk_ref)
    vc = conv(2, cwv_ref)
    ext_ref[0:halo, :] = ext_ref[tt:tt + halo, :]

    ab = ab_ref[...]
    g_all = -jnp.exp(alog_ref[...]) * _softplus(ab + dtb_ref[...])
    beta_all = jax.nn.sigmoid(ab)
    ri = lax.broadcasted_iota(jnp.int32, (tt, tt), 0)
    ci = lax.broadcasted_iota(jnp.int32, (tt, tt), 1)
    shift = CHUNK.bit_length() - 1
    tri = jnp.where(((ri >> shift) == (ci >> shift)) & (ci <= ri), 1.0, 0.0).astype(BF16)
    g_hi, g_mid, g_lo = _split3(g_all)
    gc_all = _dot(tri, g_hi) + _dot(tri, g_mid) + _dot(tri, g_lo)
    gct_ref[...] = gc_all.T

    lane = lax.broadcasted_iota(jnp.int32, (tt, LANES), 1)
    r128 = lax.broadcasted_iota(jnp.int32, (CHUNK, CHUNK), 0)
    c128 = lax.broadcasted_iota(jnp.int32, (CHUNK, CHUNK), 1)
    sub8 = lax.broadcasted_iota(jnp.int32, (SUBLANES, CHUNK), 0)
    for hh in range(hg):
        head = hgi * hg + hh
        lanes = slice(hh * HEAD_DIM, (hh + 1) * HEAD_DIM)
        gcol_all = jnp.sum(jnp.where(lane == head, gc_all, 0.0), axis=-1, keepdims=True)
        bcol_all = jnp.sum(jnp.where(lane == head + HEADS, beta_all, 0.0), axis=-1, keepdims=True)
        s = s_ref[hh]
        for c in range(tt // CHUNK):
            rows = slice(c * CHUNK, (c + 1) * CHUNK)
            qh, kh, vh = qc[rows, lanes], kc[rows, lanes], vc[rows, lanes]
            qn = qh * lax.rsqrt(jnp.sum(qh * qh, axis=-1, keepdims=True) + EPS) * (HEAD_DIM ** -0.5)
            kn = kh * lax.rsqrt(jnp.sum(kh * kh, axis=-1, keepdims=True) + EPS)
            gcol = gcol_all[rows]
            bcol = bcol_all[rows]
            gblk = gct_ref[pl.ds(pl.multiple_of((head >> 3) * SUBLANES, SUBLANES), SUBLANES), rows]
            grow = jnp.sum(jnp.where(sub8 == (head & (SUBLANES - 1)), gblk, 0.0), axis=0, keepdims=True)
            decay = jnp.exp(jnp.minimum(gcol - grow, 0.0))
            kb = kn * bcol
            knb = kn.astype(BF16)
            a = _dot_nt(kb.astype(BF16), knb) * jnp.where(c128 < r128, decay, 0.0)
            qk = _dot_nt(qn.astype(BF16), knb) * jnp.where(c128 <= r128, decay, 0.0)
            tinv = _unit_lower_inverse_minus_identity(a).astype(BF16)
            eg = jnp.exp(gcol)
            rhs = jnp.concatenate([vh * bcol, kb * eg], axis=1)
            sol = rhs + _dot(tinv, rhs.astype(BF16))
            a_hi, a_lo = _split2(a)
            s_hi, s_lo = _split2(sol)
            a_sol = _dot(jnp.concatenate([a_hi, a_lo, a_hi], axis=1), jnp.concatenate([s_hi, s_hi, s_lo], axis=0))
            res = (rhs - sol) - a_sol
            sol = sol + res + _dot(tinv, res.astype(BF16))
            u, wdec = sol[:, :HEAD_DIM], sol[:, HEAD_DIM:]
            glast = gcol[CHUNK - 1:CHUNK, :]
            ktail = kn * jnp.exp(glast - gcol)
            p1 = _dot(jnp.concatenate([wdec, qn * eg], axis=0).astype(BF16), s.astype(BF16))
            v_new = u - p1[:CHUNK]
            v_new_b = v_new.astype(BF16)
            o = p1[CHUNK:] + _dot(qk.astype(BF16), v_new_b)
            s = s * jnp.exp(glast) + _dot(ktail.T.astype(BF16), v_new_b)
            o_ref[rows, lanes] = (_rms(o, gn_ref[...]) * _silu(z_ref[rows, lanes])).astype(BF16)
        s_ref[hh] = s

    @pl.when(t == pl.num_programs(2) - 1)
    def _():
        ssm_ref[0, 0] = s_ref[...]


def _gdn_prompt(proj, ab, conv_w, alog_pad, dtb_pad, gdn_norm, *, batch, seq, hg, tt):
    nt = seq // tt
    w = hg * HEAD_DIM
    body = functools.partial(_gdn_prompt_body, hg=hg, tt=tt)

    def col_spec(col0):
        return pl.BlockSpec((tt, w), lambda b, h, t: (b * nt + t, col0 // w + h))

    def cw_spec(part):
        return pl.BlockSpec((CONV_WIDTH, w), lambda b, h, t: (0, part * GDN_DIM // w + h))

    small = pl.BlockSpec((1, LANES), lambda b, h, t: (0, 0))
    return pl.pallas_call(
        body,
        grid=(batch, HEADS // hg, nt),
        in_specs=[col_spec(C_Q), col_spec(C_K), col_spec(C_V), col_spec(C_Z),
                  pl.BlockSpec((tt, LANES), lambda b, h, t: (b * nt + t, 0)),
                  cw_spec(0), cw_spec(1), cw_spec(2), small, small, small],
        out_specs=[
            pl.BlockSpec((tt, w), lambda b, h, t: (b * nt + t, h)),
            pl.BlockSpec((1, 1, hg, HEAD_DIM, HEAD_DIM), lambda b, h, t: (0, b, h, 0, 0)),
        ],
        out_shape=[
            jax.ShapeDtypeStruct((batch * seq, GDN_DIM), BF16),
            jax.ShapeDtypeStruct((1, batch, HEADS, HEAD_DIM, HEAD_DIM), F32),
        ],
        scratch_shapes=[
            pltpu.VMEM((SUBLANES + tt, 3 * w), F32),
            pltpu.VMEM((hg, HEAD_DIM, HEAD_DIM), F32),
            pltpu.VMEM((LANES, tt), F32),
        ],
        compiler_params=_cparams("parallel", "parallel", "arbitrary"),
        name="gdn_prompt",
    )(proj, proj, proj, proj, ab, conv_w, conv_w, conv_w, alog_pad, dtb_pad, gdn_norm)


def _gdn_sample_prep_body(q_ref, k_ref, v_ref, ab_ref, st_ref, cw_ref, alog_ref, dtb_ref,
                          qn_ref, kn_ref, vc_ref, eg_ref, beta_ref, qk_ref):
    def conv(part, u_ref):
        cols = slice(part * GDN_DIM, (part + 1) * GDN_DIM)
        acc = u_ref[...] * cw_ref[CONV_BUF:CONV_WIDTH, cols]
        for i in range(CONV_BUF):
            acc = acc + st_ref[i, :, cols] * cw_ref[i:i + 1, cols]
        return _silu(acc)

    qc = conv(0, q_ref)
    kc = conv(1, k_ref)
    vc_ref[...] = conv(2, v_ref)
    ab = ab_ref[...]
    g_all = -jnp.exp(alog_ref[...]) * _softplus(ab + dtb_ref[...])
    eg_all = jnp.exp(g_all)
    beta_all = jax.nn.sigmoid(ab)
    for h in range(HEADS):
        lanes = slice(h * HEAD_DIM, (h + 1) * HEAD_DIM)
        qh, kh = qc[:, lanes], kc[:, lanes]
        qn = qh * lax.rsqrt(jnp.sum(qh * qh, axis=-1, keepdims=True) + EPS) * (HEAD_DIM ** -0.5)
        kn = kh * lax.rsqrt(jnp.sum(kh * kh, axis=-1, keepdims=True) + EPS)
        qn_ref[:, lanes] = qn
        kn_ref[:, lanes] = kn
        shape = qn.shape
        qk_ref[:, lanes] = jnp.broadcast_to(jnp.sum(qn * kn, axis=-1, keepdims=True), shape)
        eg_ref[:, lanes] = jnp.broadcast_to(eg_all[:, h:h + 1], shape)
        beta_ref[:, lanes] = jnp.broadcast_to(beta_all[:, HEADS + h:HEADS + h + 1], shape)


def _gdn_sample_prep(proj, ab, conv_state_t, conv_w, alog_pad, dtb_pad):
    nb = proj.shape[0]

    def col_spec(col0):
        return pl.BlockSpec((nb, GDN_DIM), lambda i: (0, col0 // GDN_DIM))

    small = pl.BlockSpec((1, LANES), lambda i: (0, 0))
    full = pl.BlockSpec((nb, GDN_DIM), lambda i: (0, 0))
    return pl.pallas_call(
        _gdn_sample_prep_body,
        grid=(1,),
        in_specs=[col_spec(C_Q), col_spec(C_K), col_spec(C_V),
                  pl.BlockSpec((nb, LANES), lambda i: (0, 0)),
                  pl.BlockSpec((CONV_BUF, nb, QKV_DIM), lambda i: (0, 0, 0)),
                  pl.BlockSpec((CONV_WIDTH, QKV_DIM), lambda i: (0, 0)),
                  small, small],
        out_specs=[full] * 6,
        out_shape=[jax.ShapeDtypeStruct((nb, GDN_DIM), F32)] * 6,
        compiler_params=_cparams("arbitrary"),
        name="gdn_sample_prep",
    )(proj, proj, proj, ab, conv_state_t, conv_w, alog_pad, dtb_pad)


def _gdn_sample_step_body(qn_ref, kn_ref, vc_ref, eg_ref, beta_ref, qk_ref, z_ref, gn_ref, s_ref,
                          o_ref, snew_ref, *, bb):
    pad = jnp.zeros((LANES - 2 * HEADS, HEAD_DIM), F32)
    for b in range(bb):
        cols = jnp.concatenate([kn_ref[b], qn_ref[b], pad], axis=0).T
        outs = []
        for h in range(HEADS):
            kcol = cols[:, h:h + 1]
            qcol = cols[:, HEADS + h:HEADS + h + 1]
            s = s_ref[0, b, h]
            k_s = jnp.sum(s * kcol, axis=0, keepdims=True)
            q_s = jnp.sum(s * qcol, axis=0, keepdims=True)
            eg = eg_ref[b, h:h + 1, :]
            v_new = beta_ref[b, h:h + 1, :] * (vc_ref[b, h:h + 1, :] - eg * k_s)
            snew_ref[0, b, h] = eg * s + kcol * v_new
            outs.append(eg * q_s + qk_ref[b, h:h + 1, :] * v_new)
        o = jnp.concatenate(outs, axis=0)
        o_ref[b] = (_rms(o, gn_ref[...]) * _silu(z_ref[b])).astype(BF16)


def _gdn_sample_step(qn, kn, vc, eg, beta, qk, z, gdn_norm, state_ssm, *, bb):
    nb = qn.shape[0]
    body = functools.partial(_gdn_sample_step_body, bb=bb)
    vec = pl.BlockSpec((bb, HEADS, HEAD_DIM), lambda i: (i, 0, 0))
    st = pl.BlockSpec((1, bb, HEADS, HEAD_DIM, HEAD_DIM), lambda i: (0, i, 0, 0, 0))
    return pl.pallas_call(
        body,
        grid=(nb // bb,),
        in_specs=[vec] * 7 + [pl.BlockSpec((1, LANES), lambda i: (0, 0)), st],
        out_specs=[vec, st],
        out_shape=[jax.ShapeDtypeStruct((nb, HEADS, HEAD_DIM), BF16),
                   jax.ShapeDtypeStruct(state_ssm.shape, F32)],
        compiler_params=_cparams("parallel"),
        name="gdn_sample_step",
    )(qn, kn, vc, eg, beta, qk, z, gdn_norm, state_ssm)


def _merge_body(pool_ref, gdn_ref, wpu_ref, wgu_ref, gp_ref, gg_ref, o_ref):
    o_ref[...] = (gp_ref[...] * _dot(pool_ref[...], wpu_ref[...])
                  + gg_ref[...] * _dot(gdn_ref[...], wgu_ref[...])).astype(BF16)


def _merge(proj, pool_out, gdn_out, w_pool_up, w_gdn_up, *, tm, tn=512):
    m = proj.shape[0]
    nj = D_MODEL // tn
    return pl.pallas_call(
        _merge_body,
        grid=(m // tm, nj),
        in_specs=[
            pl.BlockSpec((tm, POOL_DIM), lambda i, j: (i, 0)),
            pl.BlockSpec((tm, GDN_DIM), lambda i, j: (i, 0)),
            pl.BlockSpec((POOL_DIM, tn), lambda i, j: (0, j)),
            pl.BlockSpec((GDN_DIM, tn), lambda i, j: (0, j)),
            pl.BlockSpec((tm, tn), lambda i, j: (i, C_GP // tn + j)),
            pl.BlockSpec((tm, tn), lambda i, j: (i, C_GG // tn + j)),
        ],
        out_specs=pl.BlockSpec((tm, tn), lambda i, j: (i, j)),
        out_shape=jax.ShapeDtypeStruct((m, D_MODEL), BF16),
        compiler_params=_cparams("parallel", "parallel"),
        name="merge",
    )(pool_out, gdn_out, w_pool_up, w_gdn_up, proj, proj)


def _outproj_body(a_ref, w_ref, x_ref, o_ref):
    o_ref[...] = x_ref[...] + _dot(a_ref[...], w_ref[...])


def _outproj(merged, x, w_o, *, tm, tn=512):
    m = x.shape[0]
    return pl.pallas_call(
        _outproj_body,
        grid=(m // tm, D_MODEL // tn),
        in_specs=[
            pl.BlockSpec((tm, D_MODEL), lambda i, j: (i, 0)),
            pl.BlockSpec((D_MODEL, tn), lambda i, j: (0, j)),
            pl.BlockSpec((tm, tn), lambda i, j: (i, j)),
        ],
        out_specs=pl.BlockSpec((tm, tn), lambda i, j: (i, j)),
        out_shape=jax.ShapeDtypeStruct((m, D_MODEL), F32),
        compiler_params=_cparams("parallel", "parallel"),
        name="outproj",
    )(merged, w_o, x)


def _ffn_body(x_ref, gain_ref, wg_ref, wu_ref, wd_ref, o_ref, h_ref):
    f = pl.program_id(1)

    @pl.when(f == 0)
    def _():
        x = x_ref[...]
        h_ref[...] = _rms(x, gain_ref[...]).astype(BF16)
        o_ref[...] = x

    h = h_ref[...]
    act = (_silu(_dot(h, wg_ref[...])) * _dot(h, wu_ref[...])).astype(BF16)
    o_ref[...] += _dot(act, wd_ref[...])


def _ffn(x, gain, w_gate_up, w_down, *, tm, tf=512):
    m = x.shape[0]
    nf = D_FF // tf
    return pl.pallas_call(
        _ffn_body,
        grid=(m // tm, nf),
        in_specs=[
            pl.BlockSpec((tm, D_MODEL), lambda i, f: (i, 0)),
            pl.BlockSpec((1, D_MODEL), lambda i, f: (0, 0)),
            pl.BlockSpec((D_MODEL, tf), lambda i, f: (0, f)),
            pl.BlockSpec((D_MODEL, tf), lambda i, f: (0, nf + f)),
            pl.BlockSpec((tf, D_MODEL), lambda i, f: (f, 0)),
        ],
        out_specs=pl.BlockSpec((tm, D_MODEL), lambda i, f: (i, 0)),
        out_shape=jax.ShapeDtypeStruct((m, D_MODEL), F32),
        scratch_shapes=[pltpu.VMEM((tm, D_MODEL), BF16)],
        compiler_params=_cparams("parallel", "arbitrary"),
        name="ffn",
    )(x, gain, w_gate_up, w_gate_up, w_down)


def _final_body(x_ref, p_ref, wple_ref, wpg_ref, gain_ref, o_ref):
    x = x_ref[...]
    gate = jax.nn.sigmoid(_dot(x.astype(BF16), wpg_ref[...]))
    emb = _dot(p_ref[...].astype(BF16), wple_ref[...])
    o_ref[...] = _rms(x + emb * gate, gain_ref[...])


def _final(x, p, w_ple, w_ple_gate, gain, *, tm):
    m = x.shape[0]
    return pl.pallas_call(
        _final_body,
        grid=(m // tm,),
        in_specs=[
            pl.BlockSpec((tm, D_MODEL), lambda i: (i, 0)),
            pl.BlockSpec((tm, PLE_DIM), lambda i: (i, 0)),
            pl.BlockSpec((PLE_DIM, D_MODEL), lambda i: (0, 0)),
            pl.BlockSpec((D_MODEL, D_MODEL), lambda i: (0, 0)),
            pl.BlockSpec((1, D_MODEL), lambda i: (0, 0)),
        ],
        out_specs=pl.BlockSpec((tm, D_MODEL), lambda i: (i, 0)),
        out_shape=jax.ShapeDtypeStruct((m, D_MODEL), F32),
        compiler_params=_cparams("parallel"),
        name="final",
    )(x, p, w_ple, w_ple_gate, gain)


def _tile_rows(m, want):
    return want if m % want == 0 else m


def _dense_tail(x, p, proj, pool_out, gdn_out, wts):
    m = x.shape[0]
    tm = _tile_rows(m, 512)
    merged = _merge(proj, pool_out, gdn_out, wts["w_pool_up"], wts["w_gdn_up"], tm=tm)
    x = _outproj(merged, x, wts["w_o"], tm=tm)
    x = _ffn(x, wts["norm_ffn"], wts["w_gate_up"], wts["w_down"], tm=tm)
    return _final(x, p, wts["w_ple"], wts["w_ple_gate"], wts["norm_final"], tm=_tile_rows(m, 256))


def kernel(x_prompt, x_sample, p_prompt, p_sample, state_pool, state_conv, state_ssm, norm_mix, w_in, pool_w,
           pool_scale, conv_w, a_log, dt_bias, gdn_norm, w_pool_up, w_gdn_up, w_o, norm_ffn, w_gate_up, w_down,
           w_ple, w_ple_gate, norm_final):
    batch, seq, _ = x_prompt.shape
    nb = x_sample.shape[0]
    w_in0 = w_in[0]
    w_cat = jnp.concatenate([w_in0[:, IN_GATES:], w_in0[:, IN_QKV:IN_AB], w_in0[:, :IN_QKV]], axis=1).astype(BF16)
    wts = dict(
        w_pool_up=w_pool_up[0].astype(BF16), w_gdn_up=w_gdn_up[0].astype(BF16), w_o=w_o[0].astype(BF16),
        norm_ffn=norm_ffn, w_gate_up=w_gate_up[0].astype(BF16), w_down=w_down[0].astype(BF16),
        w_ple=w_ple[0].astype(BF16), w_ple_gate=w_ple_gate[0].astype(BF16), norm_final=norm_final[None, :],
    )
    pool_w_b = pool_w[0].astype(BF16)
    lane_pad = ((0, 0), (0, LANES - HEADS))
    alog_pad = jnp.pad(a_log, lane_pad)
    dtb_pad = jnp.pad(dt_bias, lane_pad)
    gn = gdn_norm

    xp = x_prompt.reshape(batch * seq, D_MODEL)
    proj_p, ab_p = _inproj(xp, norm_mix, w_cat, w_in0, tm=_tile_rows(batch * seq, 1024))
    pool_out_p = _pool_prompt(proj_p, pool_w_b, pool_scale, batch=batch, seq=seq, tt=_tile_rows(seq, 256))
    gdn_out_p, ssm_p = _gdn_prompt(proj_p, ab_p, conv_w[0], alog_pad, dtb_pad, gn, batch=batch, seq=seq,
                                   hg=2, tt=_tile_rows(seq, 256))
    y_p = _dense_tail(xp, p_prompt[0].reshape(batch * seq, PLE_DIM), proj_p, pool_out_p, gdn_out_p, wts)
    proj_p3 = proj_p.reshape(batch, seq, C_END)
    pool_p = proj_p3[:, seq - POOL_BUF:, C_POOL:C_END][None]
    conv_p = proj_p3[:, seq - CONV_BUF:, C_Q:C_Z][None]

    xs = x_sample.reshape(nb, D_MODEL)
    proj_s, ab_s = _inproj(xs, norm_mix, w_cat, w_in0, tm=nb)
    pool_out_s = _pool_sample(proj_s, jnp.swapaxes(state_pool[0], 0, 1), pool_w_b, pool_scale)
    prep = _gdn_sample_prep(proj_s, ab_s, jnp.swapaxes(state_conv[0], 0, 1), conv_w[0], alog_pad, dtb_pad)
    to_heads = lambda a: a.reshape(nb, HEADS, HEAD_DIM)
    gdn_out_s, ssm_s = _gdn_sample_step(*[to_heads(a) for a in prep], to_heads(proj_s[:, C_Z:C_POOL]), gn,
                                        state_ssm, bb=2)
    y_s = _dense_tail(xs, p_sample[0].reshape(nb, PLE_DIM), proj_s, pool_out_s, gdn_out_s.reshape(nb, GDN_DIM), wts)
    pool_s = jnp.concatenate([state_pool[0][:, 1:], proj_s[:, None, C_POOL:C_END]], axis=1)[None]
    conv_s = jnp.concatenate([state_conv[0][:, 1:], proj_s[:, None, C_Q:C_Z]], axis=1)[None]

    return (y_p.reshape(batch, seq, D_MODEL), y_s.reshape(nb, 1, D_MODEL), pool_p, conv_p, ssm_p,
            pool_s, conv_s, ssm_s)
```

```python
import functools

import jax
import jax.numpy as jnp
from jax import lax
from jax.experimental import pallas as pl
from jax.experimental.pallas import tpu as pltpu

F32 = jnp.float32
BF16 = jnp.bfloat16

D_MODEL = 2048
PAST_LEN = 16384
POOL_WINDOWS = (2, 4, 8, 16)
POOL_GROUP_DIM = D_MODEL // 8
POOL_DIM = len(POOL_WINDOWS) * POOL_GROUP_DIM
POOL_BUF = max(POOL_WINDOWS) - 1
HEAD_DIM = 128
HEADS = D_MODEL // HEAD_DIM
GDN_DIM = HEADS * HEAD_DIM
QKV_DIM = 3 * GDN_DIM
CONV_WIDTH = 4
CONV_BUF = CONV_WIDTH - 1
D_FF = -(-8 * D_MODEL // (3 * 256)) * 256
PLE_DIM = 256
EPS = 1e-6

IN_QKV = POOL_DIM
IN_Z = IN_QKV + QKV_DIM
IN_AB = IN_Z + GDN_DIM
IN_GATES = IN_AB + 2 * HEADS
C_GP = 0
C_GG = C_GP + D_MODEL
C_Q = C_GG + D_MODEL
C_K = C_Q + GDN_DIM
C_V = C_K + GDN_DIM
C_Z = C_V + GDN_DIM
C_POOL = C_Z + GDN_DIM
C_END = C_POOL + POOL_DIM

LANES = 128
SUBLANES = 8
VMEM_LIMIT_BYTES = 56 * 1024 * 1024
CHUNK = 128


def _cparams(*sem):
    return pltpu.CompilerParams(dimension_semantics=sem, vmem_limit_bytes=VMEM_LIMIT_BYTES)


def _rms(x, gain):
    return x * lax.rsqrt(jnp.mean(x * x, axis=-1, keepdims=True) + EPS) * gain


def _dot(a, b):
    return jnp.dot(a, b, preferred_element_type=F32)


def _dot_nt(a, b):
    return lax.dot_general(a, b, (((1,), (1,)), ((), ())), preferred_element_type=F32)


def _silu(x):
    return x * jax.nn.sigmoid(x)


def _inproj_body(x_ref, gain_ref, w_ref, wab_ref, o_ref, ab_ref, h_ref, *, n_gate_tiles):
    j = pl.program_id(1)

    @pl.when(j == 0)
    def _():
        h = _rms(x_ref[...], gain_ref[...]).astype(BF16)
        h_ref[...] = h
        ab_ref[...] = _dot(h, wab_ref[...].astype(BF16))

    acc = _dot(h_ref[...], w_ref[...])

    @pl.when(j < n_gate_tiles)
    def _():
        o_ref[...] = jax.nn.sigmoid(acc)

    @pl.when(j >= n_gate_tiles)
    def _():
        o_ref[...] = acc


def _inproj(x, gain, w_cat, w_in, *, tm, tn=512):
    m = x.shape[0]
    n = w_cat.shape[1]
    body = functools.partial(_inproj_body, n_gate_tiles=(2 * D_MODEL) // tn)
    return pl.pallas_call(
        body,
        grid=(m // tm, n // tn),
        in_specs=[
            pl.BlockSpec((tm, D_MODEL), lambda i, j: (i, 0)),
            pl.BlockSpec((1, D_MODEL), lambda i, j: (0, 0)),
            pl.BlockSpec((D_MODEL, tn), lambda i, j: (0, j)),
            pl.BlockSpec((D_MODEL, LANES), lambda i, j: (0, IN_AB // LANES)),
        ],
        out_specs=[
            pl.BlockSpec((tm, tn), lambda i, j: (i, j)),
            pl.BlockSpec((tm, LANES), lambda i, j: (i, 0)),
        ],
        out_shape=[jax.ShapeDtypeStruct((m, n), F32), jax.ShapeDtypeStruct((m, LANES), F32)],
        scratch_shapes=[pltpu.VMEM((tm, D_MODEL), BF16)],
        compiler_params=_cparams("parallel", "arbitrary"),
        name="inproj",
    )(x, gain, w_cat, w_in)


def _pool_group_out(mean, tok, pw_ref, scale_ref, gi):
    cols = slice(gi * POOL_GROUP_DIM, (gi + 1) * POOL_GROUP_DIM)
    y = _dot((mean - tok).astype(BF16), pw_ref[gi])
    return (y * scale_ref[:, cols]).astype(BF16)


def _pool_prompt_body(u_ref, pw_ref, scale_ref, o_ref, ext_ref, *, tt, start_pos):
    t = pl.program_id(1)
    halo = POOL_BUF + 1

    @pl.when(t == 0)
    def _():
        ext_ref[0:halo, :] = jnp.zeros((halo, POOL_DIM), F32)

    ext_ref[halo:halo + tt, :] = u_ref[...]
    pos = start_pos + t * tt + lax.broadcasted_iota(jnp.int32, (tt, 1), 0)
    for gi, w in enumerate(POOL_WINDOWS):
        cols = slice(gi * POOL_GROUP_DIM, (gi + 1) * POOL_GROUP_DIM)
        tok = ext_ref[halo:halo + tt, cols]
        tot = tok
        for i in range(1, w):
            tot = tot + ext_ref[pl.ds(halo - i, tt), cols]
        cnt = jnp.minimum(pos + 1, w).astype(F32)
        o_ref[:, cols] = _pool_group_out(tot / cnt, tok, pw_ref, scale_ref, gi)
    ext_ref[0:halo, :] = ext_ref[tt:tt + halo, :]


def _pool_prompt(proj, pool_w, pool_scale, *, batch, seq, tt):
    nt = seq // tt
    body = functools.partial(_pool_prompt_body, tt=tt, start_pos=0)
    return pl.pallas_call(
        body,
        grid=(batch, nt),
        in_specs=[
            pl.BlockSpec((tt, POOL_DIM), lambda b, t: (b * nt + t, C_POOL // POOL_DIM)),
            pl.BlockSpec((len(POOL_WINDOWS), POOL_GROUP_DIM, POOL_GROUP_DIM), lambda b, t: (0, 0, 0)),
            pl.BlockSpec((1, POOL_DIM), lambda b, t: (0, 0)),
        ],
        out_specs=pl.BlockSpec((tt, POOL_DIM), lambda b, t: (b * nt + t, 0)),
        out_shape=jax.ShapeDtypeStruct((batch * seq, POOL_DIM), BF16),
        scratch_shapes=[pltpu.VMEM((POOL_BUF + 1 + tt, POOL_DIM), F32)],
        compiler_params=_cparams("parallel", "arbitrary"),
        name="pool_prompt",
    )(proj, pool_w, pool_scale)


def _pool_sample_body(u_ref, st_ref, pw_ref, scale_ref, o_ref, *, start_pos):
    for gi, w in enumerate(POOL_WINDOWS):
        cols = slice(gi * POOL_GROUP_DIM, (gi + 1) * POOL_GROUP_DIM)
        tok = u_ref[:, cols]
        tot = tok
        for i in range(1, w):
            tot = tot + st_ref[POOL_BUF - i, :, cols]
        cnt = float(min(start_pos + 1, w))
        o_ref[:, cols] = _pool_group_out(tot / cnt, tok, pw_ref, scale_ref, gi)


def _pool_sample(proj, state_t, pool_w, pool_scale):
    nb = proj.shape[0]
    body = functools.partial(_pool_sample_body, start_pos=PAST_LEN)
    return pl.pallas_call(
        body,
        grid=(1,),
        in_specs=[
            pl.BlockSpec((nb, POOL_DIM), lambda i: (0, C_POOL // POOL_DIM)),
            pl.BlockSpec((POOL_BUF, nb, POOL_DIM), lambda i: (0, 0, 0)),
            pl.BlockSpec((len(POOL_WINDOWS), POOL_GROUP_DIM, POOL_GROUP_DIM), lambda i: (0, 0, 0)),
            pl.BlockSpec((1, POOL_DIM), lambda i: (0, 0)),
        ],
        out_specs=pl.BlockSpec((nb, POOL_DIM), lambda i: (0, 0)),
        out_shape=jax.ShapeDtypeStruct((nb, POOL_DIM), BF16),
        compiler_params=_cparams("arbitrary"),
        name="pool_sample",
    )(proj, state_t, pool_w, pool_scale)


def _split3(x):
    hi = x.astype(BF16)
    r1 = x - hi.astype(F32)
    mid = r1.astype(BF16)
    lo = (r1 - mid.astype(F32)).astype(BF16)
    return hi, mid, lo


def _split2(x):
    hi = x.astype(BF16)
    return hi, (x - hi.astype(F32)).astype(BF16)


def _softplus(x):
    return jnp.maximum(x, 0.0) + jnp.log1p(jnp.exp(-jnp.abs(x)))


def _unit_lower_inverse_minus_identity(mats):
    qs = [-a for a in mats]
    rps = list(qs)
    levels = CHUNK.bit_length() - 1
    qbs = [q.astype(BF16) for q in qs]
    qs = [_dot(qb, qb) for qb in qbs]
    for j in range(1, levels):
        qbs = [q.astype(BF16) for q in qs]
        if j < levels - 1:
            ps = [_dot(qb, jnp.concatenate([q, rp], axis=1).astype(BF16)) for qb, q, rp in zip(qbs, qs, rps)]
            rps = [rp + q + p[:, CHUNK:] for rp, q, p in zip(rps, qs, ps)]
            qs = [p[:, :CHUNK] for p in ps]
        else:
            rps = [rp + q + _dot(qb, rp.astype(BF16)) for qb, q, rp in zip(qbs, qs, rps)]
    return rps


def _gdn_prompt_body(q_ref, k_ref, v_ref, z_ref, ab_ref, cwq_ref, cwk_ref, cwv_ref, alog_ref, dtb_ref, gn_ref,
                     o_ref, ssm_ref, ext_ref, s_ref, gct_ref, *, hg, tt):
    hgi = pl.program_id(1)
    t = pl.program_id(2)
    w = hg * HEAD_DIM
    halo = SUBLANES

    @pl.when(t == 0)
    def _():
        ext_ref[0:halo, :] = jnp.zeros((halo, 3 * w), F32)
        s_ref[...] = jnp.zeros_like(s_ref)

    ext_ref[halo:halo + tt, 0:w] = q_ref[...]
    ext_ref[halo:halo + tt, w:2 * w] = k_ref[...]
    ext_ref[halo:halo + tt, 2 * w:3 * w] = v_ref[...]

    def conv(part, cw_ref):
        cols = slice(part * w, (part + 1) * w)
        acc = ext_ref[halo:halo + tt, cols] * cw_ref[CONV_BUF:CONV_WIDTH, :]
        for i in range(CONV_BUF):
            acc = acc + ext_ref[pl.ds(halo - CONV_BUF + i, tt), cols] * cw_ref[i:i + 1, :]
        return _silu(acc)

    qc = conv(0, cwq_ref)
    kc = conv(1, cwk_ref)
    vc = conv(2, cwv_ref)
    ext_ref[0:halo, :] = ext_ref[tt:tt + halo, :]

    ab = ab_ref[...]
    g_all = -jnp.exp(alog_ref[...]) * _softplus(ab + dtb_ref[...])
    beta_all = jax.nn.sigmoid(ab)
    ri = lax.broadcasted_iota(jnp.int32, (tt, tt), 0)
    ci = lax.broadcasted_iota(jnp.int32, (tt, tt), 1)
    shift = CHUNK.bit_length() - 1
    tri = jnp.where(((ri >> shift) == (ci >> shift)) & (ci <= ri), 1.0, 0.0).astype(BF16)
    g_hi, g_mid, g_lo = _split3(g_all)
    gc_all = _dot(tri, g_hi) + _dot(tri, g_mid) + _dot(tri, g_lo)
    gct_ref[...] = gc_all.T

    lane = lax.broadcasted_iota(jnp.int32, (tt, LANES), 1)
    r128 = lax.broadcasted_iota(jnp.int32, (CHUNK, CHUNK), 0)
    c128 = lax.broadcasted_iota(jnp.int32, (CHUNK, CHUNK), 1)
    sub8 = lax.broadcasted_iota(jnp.int32, (SUBLANES, CHUNK), 0)
    n_chunks = tt // CHUNK
    chains = [(hh, c) for hh in range(hg) for c in range(n_chunks)]
    gcol_heads, bcol_heads = [], []
    for hh in range(hg):
        head = hgi * hg + hh
        gcol_heads.append(jnp.sum(jnp.where(lane == head, gc_all, 0.0), axis=-1, keepdims=True))
        bcol_heads.append(jnp.sum(jnp.where(lane == head + HEADS, beta_all, 0.0), axis=-1, keepdims=True))

    qn_l, kn_l, kb_l, rhs_l, eg_l, gcol_l, dec_l = [], [], [], [], [], [], []
    for hh, c in chains:
        head = hgi * hg + hh
        rows = slice(c * CHUNK, (c + 1) * CHUNK)
        lanes = slice(hh * HEAD_DIM, (hh + 1) * HEAD_DIM)
        qh, kh, vh = qc[rows, lanes], kc[rows, lanes], vc[rows, lanes]
        qn = qh * lax.rsqrt(jnp.sum(qh * qh, axis=-1, keepdims=True) + EPS) * (HEAD_DIM ** -0.5)
        kn = kh * lax.rsqrt(jnp.sum(kh * kh, axis=-1, keepdims=True) + EPS)
        gcol = gcol_heads[hh][rows]
        bcol = bcol_heads[hh][rows]
        gblk = gct_ref[pl.ds(pl.multiple_of((head >> 3) * SUBLANES, SUBLANES), SUBLANES), rows]
        grow = jnp.sum(jnp.where(sub8 == (head & (SUBLANES - 1)), gblk, 0.0), axis=0, keepdims=True)
        eg = jnp.exp(gcol)
        kb = kn * bcol
        qn_l.append(qn)
        kn_l.append(kn)
        kb_l.append(kb)
        eg_l.append(eg)
        gcol_l.append(gcol)
        dec_l.append(jnp.exp(jnp.minimum(gcol - grow, 0.0)))
        rhs_l.append(jnp.concatenate([vh * bcol, kb * eg], axis=1))

    a_l, qk_l = [], []
    for qn, kn, kb, decay in zip(qn_l, kn_l, kb_l, dec_l):
        knb = kn.astype(BF16)
        a_l.append(_dot_nt(kb.astype(BF16), knb) * jnp.where(c128 < r128, decay, 0.0))
        qk_l.append(_dot_nt(qn.astype(BF16), knb) * jnp.where(c128 <= r128, decay, 0.0))
    tinv_l = [x.astype(BF16) for x in _unit_lower_inverse_minus_identity(a_l)]
    sol_l = [rhs + _dot(tinv, rhs.astype(BF16)) for tinv, rhs in zip(tinv_l, rhs_l)]
    res_l = []
    for a, sol, rhs in zip(a_l, sol_l, rhs_l):
        a_hi, a_lo = _split2(a)
        s_hi, s_lo = _split2(sol)
        a_sol = _dot(jnp.concatenate([a_hi, a_lo, a_hi], axis=1), jnp.concatenate([s_hi, s_hi, s_lo], axis=0))
        res_l.append((rhs - sol) - a_sol)
    sol_l = [(sol + res + _dot(tinv, res.astype(BF16))).astype(BF16) for sol, res, tinv in zip(sol_l, res_l, tinv_l)]
    lhs_l, o_add_l, s_add_l, glast_l = [], [], [], []
    for qn, kn, eg, gcol, qk, sol_b in zip(qn_l, kn_l, eg_l, gcol_l, qk_l, sol_l):
        glast = gcol[CHUNK - 1:CHUNK, :]
        ktail = kn * jnp.exp(glast - gcol)
        kt_sol = _dot(ktail.T.astype(BF16), sol_b)
        qk_sol = _dot(qk.astype(BF16), sol_b)
        lhs_l.append(jnp.concatenate([qn * eg - qk_sol[:, HEAD_DIM:], kt_sol[:, HEAD_DIM:]], axis=0).astype(BF16))
        o_add_l.append(qk_sol[:, :HEAD_DIM])
        s_add_l.append(kt_sol[:, :HEAD_DIM])
        glast_l.append(jnp.exp(glast))
    states = [s_ref[hh] for hh in range(hg)]
    for c in range(n_chunks):
        for hh in range(hg):
            i = hh * n_chunks + c
            rows = slice(c * CHUNK, (c + 1) * CHUNK)
            lanes = slice(hh * HEAD_DIM, (hh + 1) * HEAD_DIM)
            on_s = _dot(lhs_l[i], states[hh].astype(BF16))
            o = on_s[:CHUNK] + o_add_l[i]
            states[hh] = states[hh] * glast_l[i] - on_s[CHUNK:] + s_add_l[i]
            o_ref[rows, lanes] = (_rms(o, gn_ref[...]) * _silu(z_ref[rows, lanes])).astype(BF16)
    for hh in range(hg):
        s_ref[hh] = states[hh]

    @pl.when(t == pl.num_programs(2) - 1)
    def _():
        ssm_ref[0, 0] = s_ref[...]


def _gdn_prompt(proj, ab, conv_w, alog_pad, dtb_pad, gdn_norm, *, batch, seq, hg, tt):
    nt = seq // tt
    w = hg * HEAD_DIM
    body = functools.partial(_gdn_prompt_body, hg=hg, tt=tt)

    def col_spec(col0):
        return pl.BlockSpec((tt, w), lambda b, h, t: (b * nt + t, col0 // w + h))

    def cw_spec(part):
        return pl.BlockSpec((CONV_WIDTH, w), lambda b, h, t: (0, part * GDN_DIM // w + h))

    small = pl.BlockSpec((1, LANES), lambda b, h, t: (0, 0))
    return pl.pallas_call(
        body,
        grid=(batch, HEADS // hg, nt),
        in_specs=[col_spec(C_Q), col_spec(C_K), col_spec(C_V), col_spec(C_Z),
                  pl.BlockSpec((tt, LANES), lambda b, h, t: (b * nt + t, 0)),
                  cw_spec(0), cw_spec(1), cw_spec(2), small, small, small],
        out_specs=[
            pl.BlockSpec((tt, w), lambda b, h, t: (b * nt + t, h)),
            pl.BlockSpec((1, 1, hg, HEAD_DIM, HEAD_DIM), lambda b, h, t: (0, b, h, 0, 0)),
        ],
        out_shape=[
            jax.ShapeDtypeStruct((batch * seq, GDN_DIM), BF16),
            jax.ShapeDtypeStruct((1, batch, HEADS, HEAD_DIM, HEAD_DIM), F32),
        ],
        scratch_shapes=[
            pltpu.VMEM((SUBLANES + tt, 3 * w), F32),
            pltpu.VMEM((hg, HEAD_DIM, HEAD_DIM), F32),
            pltpu.VMEM((LANES, tt), F32),
        ],
        compiler_params=_cparams("parallel", "parallel", "arbitrary"),
        name="gdn_prompt",
    )(proj, proj, proj, proj, ab, conv_w, conv_w, conv_w, alog_pad, dtb_pad, gdn_norm)


def _gdn_sample_prep_body(q_ref, k_ref, v_ref, ab_ref, st_ref, cw_ref, alog_ref, dtb_ref,
                          qn_ref, kn_ref, vc_ref, eg_ref, beta_ref, qk_ref):
    def conv(part, u_ref):
        cols = slice(part * GDN_DIM, (part + 1) * GDN_DIM)
        acc = u_ref[...] * cw_ref[CONV_BUF:CONV_WIDTH, cols]
        for i in range(CONV_BUF):
            acc = acc + st_ref[i, :, cols] * cw_ref[i:i + 1, cols]
        return _silu(acc)

    qc = conv(0, q_ref)
    kc = conv(1, k_ref)
    vc_ref[...] = conv(2, v_ref)
    ab = ab_ref[...]
    g_all = -jnp.exp(alog_ref[...]) * _softplus(ab + dtb_ref[...])
    eg_all = jnp.exp(g_all)
    beta_all = jax.nn.sigmoid(ab)
    for h in range(HEADS):
        lanes = slice(h * HEAD_DIM, (h + 1) * HEAD_DIM)
        qh, kh = qc[:, lanes], kc[:, lanes]
        qn = qh * lax.rsqrt(jnp.sum(qh * qh, axis=-1, keepdims=True) + EPS) * (HEAD_DIM ** -0.5)
        kn = kh * lax.rsqrt(jnp.sum(kh * kh, axis=-1, keepdims=True) + EPS)
        qn_ref[:, lanes] = qn
        kn_ref[:, lanes] = kn
        shape = qn.shape
        qk_ref[:, lanes] = jnp.broadcast_to(jnp.sum(qn * kn, axis=-1, keepdims=True), shape)
        eg_ref[:, lanes] = jnp.broadcast_to(eg_all[:, h:h + 1], shape)
        beta_ref[:, lanes] = jnp.broadcast_to(beta_all[:, HEADS + h:HEADS + h + 1], shape)


def _gdn_sample_prep(proj, ab, conv_state_t, conv_w, alog_pad, dtb_pad):
    nb = proj.shape[0]

    def col_spec(col0):
        return pl.BlockSpec((nb, GDN_DIM), lambda i: (0, col0 // GDN_DIM))

    small = pl.BlockSpec((1, LANES), lambda i: (0, 0))
    full = pl.BlockSpec((nb, GDN_DIM), lambda i: (0, 0))
    return pl.pallas_call(
        _gdn_sample_prep_body,
        grid=(1,),
        in_specs=[col_spec(C_Q), col_spec(C_K), col_spec(C_V),
                  pl.BlockSpec((nb, LANES), lambda i: (0, 0)),
                  pl.BlockSpec((CONV_BUF, nb, QKV_DIM), lambda i: (0, 0, 0)),
                  pl.BlockSpec((CONV_WIDTH, QKV_DIM), lambda i: (0, 0)),
                  small, small],
        out_specs=[full] * 6,
        out_shape=[jax.ShapeDtypeStruct((nb, GDN_DIM), F32)] * 6,
        compiler_params=_cparams("arbitrary"),
        name="gdn_sample_prep",
    )(proj, proj, proj, ab, conv_state_t, conv_w, alog_pad, dtb_pad)


def _gdn_sample_step_body(qn_ref, kn_ref, vc_ref, eg_ref, beta_ref, qk_ref, z_ref, gn_ref, s_ref,
                          o_ref, snew_ref, *, bb):
    pad = jnp.zeros((LANES - 2 * HEADS, HEAD_DIM), F32)
    for b in range(bb):
        cols = jnp.concatenate([kn_ref[b], qn_ref[b], pad], axis=0).T
        outs = []
        for h in range(HEADS):
            kcol = cols[:, h:h + 1]
            qcol = cols[:, HEADS + h:HEADS + h + 1]
            s = s_ref[0, b, h]
            k_s = jnp.sum(s * kcol, axis=0, keepdims=True)
            q_s = jnp.sum(s * qcol, axis=0, keepdims=True)
            eg = eg_ref[b, h:h + 1, :]
            v_new = beta_ref[b, h:h + 1, :] * (vc_ref[b, h:h + 1, :] - eg * k_s)
            snew_ref[0, b, h] = eg * s + kcol * v_new
            outs.append(eg * q_s + qk_ref[b, h:h + 1, :] * v_new)
        o = jnp.concatenate(outs, axis=0)
        o_ref[b] = (_rms(o, gn_ref[...]) * _silu(z_ref[b])).astype(BF16)


def _gdn_sample_step(qn, kn, vc, eg, beta, qk, z, gdn_norm, state_ssm, *, bb):
    nb = qn.shape[0]
    body = functools.partial(_gdn_sample_step_body, bb=bb)
    vec = pl.BlockSpec((bb, HEADS, HEAD_DIM), lambda i: (i, 0, 0))
    st = pl.BlockSpec((1, bb, HEADS, HEAD_DIM, HEAD_DIM), lambda i: (0, i, 0, 0, 0))
    return pl.pallas_call(
        body,
        grid=(nb // bb,),
        in_specs=[vec] * 7 + [pl.BlockSpec((1, LANES), lambda i: (0, 0)), st],
        out_specs=[vec, st],
        out_shape=[jax.ShapeDtypeStruct((nb, HEADS, HEAD_DIM), BF16),
                   jax.ShapeDtypeStruct(state_ssm.shape, F32)],
        compiler_params=_cparams("parallel"),
        name="gdn_sample_step",
    )(qn, kn, vc, eg, beta, qk, z, gdn_norm, state_ssm)


def _merge_body(pool_ref, gdn_ref, wpu_ref, wgu_ref, gp_ref, gg_ref, o_ref):
    o_ref[...] = (gp_ref[...] * _dot(pool_ref[...], wpu_ref[...])
                  + gg_ref[...] * _dot(gdn_ref[...], wgu_ref[...])).astype(BF16)


def _merge(proj, pool_out, gdn_out, w_pool_up, w_gdn_up, *, tm, tn=512):
    m = proj.shape[0]
    nj = D_MODEL // tn
    return pl.pallas_call(
        _merge_body,
        grid=(m // tm, nj),
        in_specs=[
            pl.BlockSpec((tm, POOL_DIM), lambda i, j: (i, 0)),
            pl.BlockSpec((tm, GDN_DIM), lambda i, j: (i, 0)),
            pl.BlockSpec((POOL_DIM, tn), lambda i, j: (0, j)),
            pl.BlockSpec((GDN_DIM, tn), lambda i, j: (0, j)),
            pl.BlockSpec((tm, tn), lambda i, j: (i, C_GP // tn + j)),
            pl.BlockSpec((tm, tn), lambda i, j: (i, C_GG // tn + j)),
        ],
        out_specs=pl.BlockSpec((tm, tn), lambda i, j: (i, j)),
        out_shape=jax.ShapeDtypeStruct((m, D_MODEL), BF16),
        compiler_params=_cparams("parallel", "parallel"),
        name="merge",
    )(pool_out, gdn_out, w_pool_up, w_gdn_up, proj, proj)


def _outproj_body(a_ref, w_ref, x_ref, o_ref):
    o_ref[...] = x_ref[...] + _dot(a_ref[...], w_ref[...])


def _outproj(merged, x, w_o, *, tm, tn=512):
    m = x.shape[0]
    return pl.pallas_call(
        _outproj_body,
        grid=(m // tm, D_MODEL // tn),
        in_specs=[
            pl.BlockSpec((tm, D_MODEL), lambda i, j: (i, 0)),
            pl.BlockSpec((D_MODEL, tn), lambda i, j: (0, j)),
            pl.BlockSpec((tm, tn), lambda i, j: (i, j)),
        ],
        out_specs=pl.BlockSpec((tm, tn), lambda i, j: (i, j)),
        out_shape=jax.ShapeDtypeStruct((m, D_MODEL), F32),
        compiler_params=_cparams("parallel", "parallel"),
        name="outproj",
    )(merged, w_o, x)


def _ffn_body(x_ref, gain_ref, wg_ref, wu_ref, wd_ref, o_ref, h_ref):
    f = pl.program_id(1)

    @pl.when(f == 0)
    def _():
        x = x_ref[...]
        h_ref[...] = _rms(x, gain_ref[...]).astype(BF16)
        o_ref[...] = x

    h = h_ref[...]
    act = (_silu(_dot(h, wg_ref[...])) * _dot(h, wu_ref[...])).astype(BF16)
    o_ref[...] += _dot(act, wd_ref[...])


def _ffn(x, gain, w_gate_up, w_down, *, tm, tf=512):
    m = x.shape[0]
    nf = D_FF // tf
    return pl.pallas_call(
        _ffn_body,
        grid=(m // tm, nf),
        in_specs=[
            pl.BlockSpec((tm, D_MODEL), lambda i, f: (i, 0)),
            pl.BlockSpec((1, D_MODEL), lambda i, f: (0, 0)),
            pl.BlockSpec((D_MODEL, tf), lambda i, f: (0, f)),
            pl.BlockSpec((D_MODEL, tf), lambda i, f: (0, nf + f)),
            pl.BlockSpec((tf, D_MODEL), lambda i, f: (f, 0)),
        ],
        out_specs=pl.BlockSpec((tm, D_MODEL), lambda i, f: (i, 0)),
        out_shape=jax.ShapeDtypeStruct((m, D_MODEL), F32),
        scratch_shapes=[pltpu.VMEM((tm, D_MODEL), BF16)],
        compiler_params=_cparams("parallel", "arbitrary"),
        name="ffn",
    )(x, gain, w_gate_up, w_gate_up, w_down)


def _final_body(x_ref, p_ref, wple_ref, wpg_ref, gain_ref, o_ref):
    x = x_ref[...]
    gate = jax.nn.sigmoid(_dot(x.astype(BF16), wpg_ref[...]))
    emb = _dot(p_ref[...].astype(BF16), wple_ref[...])
    o_ref[...] = _rms(x + emb * gate, gain_ref[...])


def _final(x, p, w_ple, w_ple_gate, gain, *, tm):
    m = x.shape[0]
    return pl.pallas_call(
        _final_body,
        grid=(m // tm,),
        in_specs=[
            pl.BlockSpec((tm, D_MODEL), lambda i: (i, 0)),
            pl.BlockSpec((tm, PLE_DIM), lambda i: (i, 0)),
            pl.BlockSpec((PLE_DIM, D_MODEL), lambda i: (0, 0)),
            pl.BlockSpec((D_MODEL, D_MODEL), lambda i: (0, 0)),
            pl.BlockSpec((1, D_MODEL), lambda i: (0, 0)),
        ],
        out_specs=pl.BlockSpec((tm, D_MODEL), lambda i: (i, 0)),
        out_shape=jax.ShapeDtypeStruct((m, D_MODEL), F32),
        compiler_params=_cparams("parallel"),
        name="final",
    )(x, p, w_ple, w_ple_gate, gain)


def _tile_rows(m, want):
    return want if m % want == 0 else m


def _dense_tail(x, p, proj, pool_out, gdn_out, wts):
    m = x.shape[0]
    tm = _tile_rows(m, 512)
    merged = _merge(proj, pool_out, gdn_out, wts["w_pool_up"], wts["w_gdn_up"], tm=tm)
    x = _outproj(merged, x, wts["w_o"], tm=tm)
    x = _ffn(x, wts["norm_ffn"], wts["w_gate_up"], wts["w_down"], tm=tm)
    return _final(x, p, wts["w_ple"], wts["w_ple_gate"], wts["norm_final"], tm=_tile_rows(m, 256))


def kernel(x_prompt, x_sample, p_prompt, p_sample, state_pool, state_conv, state_ssm, norm_mix, w_in, pool_w,
           pool_scale, conv_w, a_log, dt_bias, gdn_norm, w_pool_up, w_gdn_up, w_o, norm_ffn, w_gate_up, w_down,
           w_ple, w_ple_gate, norm_final):
    batch, seq, _ = x_prompt.shape
    nb = x_sample.shape[0]
    w_in0 = w_in[0]
    w_cat = jnp.concatenate([w_in0[:, IN_GATES:], w_in0[:, IN_QKV:IN_AB], w_in0[:, :IN_QKV]], axis=1).astype(BF16)
    wts = dict(
        w_pool_up=w_pool_up[0].astype(BF16), w_gdn_up=w_gdn_up[0].astype(BF16), w_o=w_o[0].astype(BF16),
        norm_ffn=norm_ffn, w_gate_up=w_gate_up[0].astype(BF16), w_down=w_down[0].astype(BF16),
        w_ple=w_ple[0].astype(BF16), w_ple_gate=w_ple_gate[0].astype(BF16), norm_final=norm_final[None, :],
    )
    pool_w_b = pool_w[0].astype(BF16)
    lane_pad = ((0, 0), (0, LANES - HEADS))
    alog_pad = jnp.pad(a_log, lane_pad)
    dtb_pad = jnp.pad(dt_bias, lane_pad)
    gn = gdn_norm

    xp = x_prompt.reshape(batch * seq, D_MODEL)
    proj_p, ab_p = _inproj(xp, norm_mix, w_cat, w_in0, tm=_tile_rows(batch * seq, 1024))
    pool_out_p = _pool_prompt(proj_p, pool_w_b, pool_scale, batch=batch, seq=seq, tt=_tile_rows(seq, 256))
    gdn_out_p, ssm_p = _gdn_prompt(proj_p, ab_p, conv_w[0], alog_pad, dtb_pad, gn, batch=batch, seq=seq,
                                   hg=4, tt=_tile_rows(seq, 512))
    y_p = _dense_tail(xp, p_prompt[0].reshape(batch * seq, PLE_DIM), proj_p, pool_out_p, gdn_out_p, wts)
    proj_p3 = proj_p.reshape(batch, seq, C_END)
    pool_p = proj_p3[:, seq - POOL_BUF:, C_POOL:C_END][None]
    conv_p = proj_p3[:, seq - CONV_BUF:, C_Q:C_Z][None]

    xs = x_sample.reshape(nb, D_MODEL)
    proj_s, ab_s = _inproj(xs, norm_mix, w_cat, w_in0, tm=nb)
    pool_out_s = _pool_sample(proj_s, jnp.swapaxes(state_pool[0], 0, 1), pool_w_b, pool_scale)
    prep = _gdn_sample_prep(proj_s, ab_s, jnp.swapaxes(state_conv[0], 0, 1), conv_w[0], alog_pad, dtb_pad)
    to_heads = lambda a: a.reshape(nb, HEADS, HEAD_DIM)
    gdn_out_s, ssm_s = _gdn_sample_step(*[to_heads(a) for a in prep], to_heads(proj_s[:, C_Z:C_POOL]), gn,
                                        state_ssm, bb=2)
    y_s = _dense_tail(xs, p_sample[0].reshape(nb, PLE_DIM), proj_s, pool_out_s, gdn_out_s.reshape(nb, GDN_DIM), wts)
    pool_s = jnp.concatenate([state_pool[0][:, 1:], proj_s[:, None, C_POOL:C_END]], axis=1)[None]
    conv_s = jnp.concatenate([state_conv[0][:, 1:], proj_s[:, None, C_Q:C_Z]], axis=1)[None]

    return (y_p.reshape(batch, seq, D_MODEL), y_s.reshape(nb, 1, D_MODEL), pool_p, conv_p, ssm_p,
            pool_s, conv_s, ssm_s)
```

```python
import functools

import jax
import jax.numpy as jnp
from jax import lax
from jax.experimental import pallas as pl
from jax.experimental.pallas import tpu as pltpu

F32 = jnp.float32
BF16 = jnp.bfloat16

D_MODEL = 2048
PAST_LEN = 16384
POOL_WINDOWS = (2, 4, 8, 16)
POOL_GROUP_DIM = D_MODEL // 8
POOL_DIM = len(POOL_WINDOWS) * POOL_GROUP_DIM
POOL_BUF = max(POOL_WINDOWS) - 1
HEAD_DIM = 128
HEADS = D_MODEL // HEAD_DIM
GDN_DIM = HEADS * HEAD_DIM
QKV_DIM = 3 * GDN_DIM
CONV_WIDTH = 4
CONV_BUF = CONV_WIDTH - 1
D_FF = -(-8 * D_MODEL // (3 * 256)) * 256
PLE_DIM = 256
EPS = 1e-6

IN_QKV = POOL_DIM
IN_Z = IN_QKV + QKV_DIM
IN_AB = IN_Z + GDN_DIM
IN_GATES = IN_AB + 2 * HEADS
C_GP = 0
C_GG = C_GP + D_MODEL
C_Q = C_GG + D_MODEL
C_K = C_Q + GDN_DIM
C_V = C_K + GDN_DIM
C_Z = C_V + GDN_DIM
C_POOL = C_Z + GDN_DIM
C_END = C_POOL + POOL_DIM

LANES = 128
SUBLANES = 8
VMEM_LIMIT_BYTES = 56 * 1024 * 1024
CHUNK = 128


def _cparams(*sem):
    return pltpu.CompilerParams(dimension_semantics=sem, vmem_limit_bytes=VMEM_LIMIT_BYTES)


def _rms(x, gain):
    return x * lax.rsqrt(jnp.mean(x * x, axis=-1, keepdims=True) + EPS) * gain


def _dot(a, b):
    return jnp.dot(a, b, preferred_element_type=F32)


def _dot_nt(a, b):
    return lax.dot_general(a, b, (((1,), (1,)), ((), ())), preferred_element_type=F32)


def _silu(x):
    return x * jax.nn.sigmoid(x)


def _inproj_body(x_ref, gain_ref, w_ref, wab_ref, o_ref, ab_ref, h_ref, *, n_gate_tiles):
    j = pl.program_id(1)

    @pl.when(j == 0)
    def _():
        h = _rms(x_ref[...], gain_ref[...]).astype(BF16)
        h_ref[...] = h
        ab_ref[...] = _dot_nt(h, wab_ref[...].astype(BF16))

    acc = _dot_nt(h_ref[...], w_ref[...])

    @pl.when(j < n_gate_tiles)
    def _():
        o_ref[...] = jax.nn.sigmoid(acc)

    @pl.when(j >= n_gate_tiles)
    def _():
        o_ref[...] = acc


def _inproj(x, gain, w_cat_t, w_in_t, *, tm, tn):
    m = x.shape[0]
    n = w_cat_t.shape[0]
    body = functools.partial(_inproj_body, n_gate_tiles=(2 * D_MODEL) // tn)
    return pl.pallas_call(
        body,
        grid=(m // tm, n // tn),
        in_specs=[
            pl.BlockSpec((tm, D_MODEL), lambda i, j: (i, 0)),
            pl.BlockSpec((1, D_MODEL), lambda i, j: (0, 0)),
            pl.BlockSpec((tn, D_MODEL), lambda i, j: (j, 0)),
            pl.BlockSpec((LANES, D_MODEL), lambda i, j: (IN_AB // LANES, 0)),
        ],
        out_specs=[
            pl.BlockSpec((tm, tn), lambda i, j: (i, j)),
            pl.BlockSpec((tm, LANES), lambda i, j: (i, 0)),
        ],
        out_shape=[jax.ShapeDtypeStruct((m, n), F32), jax.ShapeDtypeStruct((m, LANES), F32)],
        scratch_shapes=[pltpu.VMEM((tm, D_MODEL), BF16)],
        compiler_params=_cparams("parallel", "arbitrary"),
        name="inproj",
    )(x, gain, w_cat_t, w_in_t)


def _pool_group_out(mean, tok, pw_ref, scale_ref, gi):
    cols = slice(gi * POOL_GROUP_DIM, (gi + 1) * POOL_GROUP_DIM)
    y = _dot((mean - tok).astype(BF16), pw_ref[gi])
    return (y * scale_ref[:, cols]).astype(BF16)


def _pool_prompt_body(u_ref, pw_ref, scale_ref, o_ref, ext_ref, *, tt, start_pos):
    t = pl.program_id(1)
    halo = POOL_BUF + 1

    @pl.when(t == 0)
    def _():
        ext_ref[0:halo, :] = jnp.zeros((halo, POOL_DIM), F32)

    ext_ref[halo:halo + tt, :] = u_ref[...]
    pos = start_pos + t * tt + lax.broadcasted_iota(jnp.int32, (tt, 1), 0)
    for gi, w in enumerate(POOL_WINDOWS):
        cols = slice(gi * POOL_GROUP_DIM, (gi + 1) * POOL_GROUP_DIM)
        tok = ext_ref[halo:halo + tt, cols]
        tot = tok
        for i in range(1, w):
            tot = tot + ext_ref[pl.ds(halo - i, tt), cols]
        cnt = jnp.minimum(pos + 1, w).astype(F32)
        o_ref[:, cols] = _pool_group_out(tot / cnt, tok, pw_ref, scale_ref, gi)
    ext_ref[0:halo, :] = ext_ref[tt:tt + halo, :]


def _pool_prompt(proj, pool_w, pool_scale, *, batch, seq, tt):
    nt = seq // tt
    body = functools.partial(_pool_prompt_body, tt=tt, start_pos=0)
    return pl.pallas_call(
        body,
        grid=(batch, nt),
        in_specs=[
            pl.BlockSpec((tt, POOL_DIM), lambda b, t: (b * nt + t, C_POOL // POOL_DIM)),
            pl.BlockSpec((len(POOL_WINDOWS), POOL_GROUP_DIM, POOL_GROUP_DIM), lambda b, t: (0, 0, 0)),
            pl.BlockSpec((1, POOL_DIM), lambda b, t: (0, 0)),
        ],
        out_specs=pl.BlockSpec((tt, POOL_DIM), lambda b, t: (b * nt + t, 0)),
        out_shape=jax.ShapeDtypeStruct((batch * seq, POOL_DIM), BF16),
        scratch_shapes=[pltpu.VMEM((POOL_BUF + 1 + tt, POOL_DIM), F32)],
        compiler_params=_cparams("parallel", "arbitrary"),
        name="pool_prompt",
    )(proj, pool_w, pool_scale)


def _pool_sample_body(u_ref, st_ref, pw_ref, scale_ref, o_ref, *, start_pos):
    for gi, w in enumerate(POOL_WINDOWS):
        cols = slice(gi * POOL_GROUP_DIM, (gi + 1) * POOL_GROUP_DIM)
        tok = u_ref[:, cols]
        tot = tok
        for i in range(1, w):
            tot = tot + st_ref[POOL_BUF - i, :, cols]
        cnt = float(min(start_pos + 1, w))
        o_ref[:, cols] = _pool_group_out(tot / cnt, tok, pw_ref, scale_ref, gi)


def _pool_sample(proj, state_t, pool_w, pool_scale):
    nb = proj.shape[0]
    body = functools.partial(_pool_sample_body, start_pos=PAST_LEN)
    return pl.pallas_call(
        body,
        grid=(1,),
        in_specs=[
            pl.BlockSpec((nb, POOL_DIM), lambda i: (0, C_POOL // POOL_DIM)),
            pl.BlockSpec((POOL_BUF, nb, POOL_DIM), lambda i: (0, 0, 0)),
            pl.BlockSpec((len(POOL_WINDOWS), POOL_GROUP_DIM, POOL_GROUP_DIM), lambda i: (0, 0, 0)),
            pl.BlockSpec((1, POOL_DIM), lambda i: (0, 0)),
        ],
        out_specs=pl.BlockSpec((nb, POOL_DIM), lambda i: (0, 0)),
        out_shape=jax.ShapeDtypeStruct((nb, POOL_DIM), BF16),
        compiler_params=_cparams("arbitrary"),
        name="pool_sample",
    )(proj, state_t, pool_w, pool_scale)


def _split3(x):
    hi = x.astype(BF16)
    r1 = x - hi.astype(F32)
    mid = r1.astype(BF16)
    lo = (r1 - mid.astype(F32)).astype(BF16)
    return hi, mid, lo


def _split2(x):
    hi = x.astype(BF16)
    return hi, (x - hi.astype(F32)).astype(BF16)


def _softplus(x):
    return jnp.maximum(x, 0.0) + jnp.log1p(jnp.exp(-jnp.abs(x)))


def _unit_lower_inverse_minus_identity(mats):
    qs = [-a for a in mats]
    rps = list(qs)
    levels = CHUNK.bit_length() - 1
    qbs = [q.astype(BF16) for q in qs]
    qs = [_dot(qb, qb) for qb in qbs]
    for j in range(1, levels):
        qbs = [q.astype(BF16) for q in qs]
        if j < levels - 1:
            ps = [_dot(qb, jnp.concatenate([q, rp], axis=1).astype(BF16)) for qb, q, rp in zip(qbs, qs, rps)]
            rps = [rp + q + p[:, CHUNK:] for rp, q, p in zip(rps, qs, ps)]
            qs = [p[:, :CHUNK] for p in ps]
        else:
            rps = [rp + q + _dot(qb, rp.astype(BF16)) for qb, q, rp in zip(qbs, qs, rps)]
    return rps


def _gdn_prompt_body(q_ref, k_ref, v_ref, z_ref, ab_ref, cwq_ref, cwk_ref, cwv_ref, alog_ref, dtb_ref, gn_ref,
                     o_ref, ssm_ref, ext_ref, s_ref, gct_ref, *, hg, tt):
    hgi = pl.program_id(1)
    t = pl.program_id(2)
    w = hg * HEAD_DIM
    halo = SUBLANES

    @pl.when(t == 0)
    def _():
        ext_ref[0:halo, :] = jnp.zeros((halo, 3 * w), F32)
        s_ref[...] = jnp.zeros_like(s_ref)

    ext_ref[halo:halo + tt, 0:w] = q_ref[...]
    ext_ref[halo:halo + tt, w:2 * w] = k_ref[...]
    ext_ref[halo:halo + tt, 2 * w:3 * w] = v_ref[...]

    def conv(part, cw_ref):
        cols = slice(part * w, (part + 1) * w)
        acc = ext_ref[halo:halo + tt, cols] * cw_ref[CONV_BUF:CONV_WIDTH, :]
        for i in range(CONV_BUF):
            acc = acc + ext_ref[pl.ds(halo - CONV_BUF + i, tt), cols] * cw_ref[i:i + 1, :]
        return _silu(acc)

    qc = conv(0, cwq_ref)
    kc = conv(1, cwk_ref)
    vc = conv(2, cwv_ref)
    ext_ref[0:halo, :] = ext_ref[tt:tt + halo, :]

    ab = ab_ref[...]
    g_all = -jnp.exp(alog_ref[...]) * _softplus(ab + dtb_ref[...])
    beta_all = jax.nn.sigmoid(ab)
    ri = lax.broadcasted_iota(jnp.int32, (tt, tt), 0)
    ci = lax.broadcasted_iota(jnp.int32, (tt, tt), 1)
    shift = CHUNK.bit_length() - 1
    tri = jnp.where(((ri >> shift) == (ci >> shift)) & (ci <= ri), 1.0, 0.0).astype(BF16)
    g_hi, g_mid, g_lo = _split3(g_all)
    gc_all = _dot(tri, g_hi) + _dot(tri, g_mid) + _dot(tri, g_lo)
    gct_ref[...] = gc_all.T

    lane = lax.broadcasted_iota(jnp.int32, (tt, LANES), 1)
    r128 = lax.broadcasted_iota(jnp.int32, (CHUNK, CHUNK), 0)
    c128 = lax.broadcasted_iota(jnp.int32, (CHUNK, CHUNK), 1)
    sub8 = lax.broadcasted_iota(jnp.int32, (SUBLANES, CHUNK), 0)
    n_chunks = tt // CHUNK
    chains = [(hh, c) for hh in range(hg) for c in range(n_chunks)]
    gcol_heads, bcol_heads = [], []
    for hh in range(hg):
        head = hgi * hg + hh
        gcol_heads.append(jnp.sum(jnp.where(lane == head, gc_all, 0.0), axis=-1, keepdims=True))
        bcol_heads.append(jnp.sum(jnp.where(lane == head + HEADS, beta_all, 0.0), axis=-1, keepdims=True))

    qn_l, kn_l, kb_l, rhs_l, eg_l, gcol_l, dec_l = [], [], [], [], [], [], []
    for hh, c in chains:
        head = hgi * hg + hh
        rows = slice(c * CHUNK, (c + 1) * CHUNK)
        lanes = slice(hh * HEAD_DIM, (hh + 1) * HEAD_DIM)
        qh, kh, vh = qc[rows, lanes], kc[rows, lanes], vc[rows, lanes]
        qn = qh * lax.rsqrt(jnp.sum(qh * qh, axis=-1, keepdims=True) + EPS) * (HEAD_DIM ** -0.5)
        kn = kh * lax.rsqrt(jnp.sum(kh * kh, axis=-1, keepdims=True) + EPS)
        gcol = gcol_heads[hh][rows]
        bcol = bcol_heads[hh][rows]
        gblk = gct_ref[pl.ds(pl.multiple_of((head >> 3) * SUBLANES, SUBLANES), SUBLANES), rows]
        grow = jnp.sum(jnp.where(sub8 == (head & (SUBLANES - 1)), gblk, 0.0), axis=0, keepdims=True)
        eg = jnp.exp(gcol)
        kb = kn * bcol
        qn_l.append(qn)
        kn_l.append(kn)
        kb_l.append(kb)
        eg_l.append(eg)
        gcol_l.append(gcol)
        dec_l.append(jnp.exp(jnp.minimum(gcol - grow, 0.0)))
        rhs_l.append(jnp.concatenate([vh * bcol, kb * eg], axis=1))

    a_l, qk_l = [], []
    for qn, kn, kb, decay in zip(qn_l, kn_l, kb_l, dec_l):
        knb = kn.astype(BF16)
        a_l.append(_dot_nt(kb.astype(BF16), knb) * jnp.where(c128 < r128, decay, 0.0))
        qk_l.append(_dot_nt(qn.astype(BF16), knb) * jnp.where(c128 <= r128, decay, 0.0))
    tinv_l = [x.astype(BF16) for x in _unit_lower_inverse_minus_identity(a_l)]
    sol_l = [rhs + _dot(tinv, rhs.astype(BF16)) for tinv, rhs in zip(tinv_l, rhs_l)]
    res_l = []
    for a, sol, rhs in zip(a_l, sol_l, rhs_l):
        a_hi, a_lo = _split2(a)
        s_hi, s_lo = _split2(sol)
        a_sol = _dot(jnp.concatenate([a_hi, a_lo, a_hi], axis=1), jnp.concatenate([s_hi, s_hi, s_lo], axis=0))
        res_l.append((rhs - sol) - a_sol)
    sol_l = [(sol + res + _dot(tinv, res.astype(BF16))).astype(BF16) for sol, res, tinv in zip(sol_l, res_l, tinv_l)]
    lhs_l, o_add_l, s_add_l, glast_l = [], [], [], []
    for qn, kn, eg, gcol, qk, sol_b in zip(qn_l, kn_l, eg_l, gcol_l, qk_l, sol_l):
        glast = gcol[CHUNK - 1:CHUNK, :]
        ktail = kn * jnp.exp(glast - gcol)
        kt_sol = _dot(ktail.T.astype(BF16), sol_b)
        qk_sol = _dot(qk.astype(BF16), sol_b)
        lhs_l.append(jnp.concatenate([qn * eg - qk_sol[:, HEAD_DIM:], kt_sol[:, HEAD_DIM:]], axis=0).astype(BF16))
        o_add_l.append(qk_sol[:, :HEAD_DIM])
        s_add_l.append(kt_sol[:, :HEAD_DIM])
        glast_l.append(jnp.exp(glast))
    states = [s_ref[hh] for hh in range(hg)]
    for c in range(n_chunks):
        for hh in range(hg):
            i = hh * n_chunks + c
            rows = slice(c * CHUNK, (c + 1) * CHUNK)
            lanes = slice(hh * HEAD_DIM, (hh + 1) * HEAD_DIM)
            on_s = _dot(lhs_l[i], states[hh].astype(BF16))
            o = on_s[:CHUNK] + o_add_l[i]
            states[hh] = states[hh] * glast_l[i] - on_s[CHUNK:] + s_add_l[i]
            o_ref[rows, lanes] = (_rms(o, gn_ref[...]) * _silu(z_ref[rows, lanes])).astype(BF16)
    for hh in range(hg):
        s_ref[hh] = states[hh]

    @pl.when(t == pl.num_programs(2) - 1)
    def _():
        ssm_ref[0, 0] = s_ref[...]


def _gdn_prompt(proj, ab, conv_w, alog_pad, dtb_pad, gdn_norm, *, batch, seq, hg, tt):
    nt = seq // tt
    w = hg * HEAD_DIM
    body = functools.partial(_gdn_prompt_body, hg=hg, tt=tt)

    def col_spec(col0):
        return pl.BlockSpec((tt, w), lambda b, h, t: (b * nt + t, col0 // w + h))

    def cw_spec(part):
        return pl.BlockSpec((CONV_WIDTH, w), lambda b, h, t: (0, part * GDN_DIM // w + h))

    small = pl.BlockSpec((1, LANES), lambda b, h, t: (0, 0))
    return pl.pallas_call(
        body,
        grid=(batch, HEADS // hg, nt),
        in_specs=[col_spec(C_Q), col_spec(C_K), col_spec(C_V), col_spec(C_Z),
                  pl.BlockSpec((tt, LANES), lambda b, h, t: (b * nt + t, 0)),
                  cw_spec(0), cw_spec(1), cw_spec(2), small, small, small],
        out_specs=[
            pl.BlockSpec((tt, w), lambda b, h, t: (b * nt + t, h)),
            pl.BlockSpec((1, 1, hg, HEAD_DIM, HEAD_DIM), lambda b, h, t: (0, b, h, 0, 0)),
        ],
        out_shape=[
            jax.ShapeDtypeStruct((batch * seq, GDN_DIM), BF16),
            jax.ShapeDtypeStruct((1, batch, HEADS, HEAD_DIM, HEAD_DIM), F32),
        ],
        scratch_shapes=[
            pltpu.VMEM((SUBLANES + tt, 3 * w), F32),
            pltpu.VMEM((hg, HEAD_DIM, HEAD_DIM), F32),
            pltpu.VMEM((LANES, tt), F32),
        ],
        compiler_params=_cparams("parallel", "parallel", "arbitrary"),
        name="gdn_prompt",
    )(proj, proj, proj, proj, ab, conv_w, conv_w, conv_w, alog_pad, dtb_pad, gdn_norm)


def _gdn_sample_prep_body(q_ref, k_ref, v_ref, ab_ref, st_ref, cw_ref, alog_ref, dtb_ref,
                          qn_ref, kn_ref, vc_ref, eg_ref, beta_ref, qk_ref):
    def conv(part, u_ref):
        cols = slice(part * GDN_DIM, (part + 1) * GDN_DIM)
        acc = u_ref[...] * cw_ref[CONV_BUF:CONV_WIDTH, cols]
        for i in range(CONV_BUF):
            acc = acc + st_ref[i, :, cols] * cw_ref[i:i + 1, cols]
        return _silu(acc)

    qc = conv(0, q_ref)
    kc = conv(1, k_ref)
    vc_ref[...] = conv(2, v_ref)
    ab = ab_ref[...]
    g_all = -jnp.exp(alog_ref[...]) * _softplus(ab + dtb_ref[...])
    eg_all = jnp.exp(g_all)
    beta_all = jax.nn.sigmoid(ab)
    for h in range(HEADS):
        lanes = slice(h * HEAD_DIM, (h + 1) * HEAD_DIM)
        qh, kh = qc[:, lanes], kc[:, lanes]
        qn = qh * lax.rsqrt(jnp.sum(qh * qh, axis=-1, keepdims=True) + EPS) * (HEAD_DIM ** -0.5)
        kn = kh * lax.rsqrt(jnp.sum(kh * kh, axis=-1, keepdims=True) + EPS)
        qn_ref[:, lanes] = qn
        kn_ref[:, lanes] = kn
        shape = qn.shape
        qk_ref[:, lanes] = jnp.broadcast_to(jnp.sum(qn * kn, axis=-1, keepdims=True), shape)
        eg_ref[:, lanes] = jnp.broadcast_to(eg_all[:, h:h + 1], shape)
        beta_ref[:, lanes] = jnp.broadcast_to(beta_all[:, HEADS + h:HEADS + h + 1], shape)


def _gdn_sample_prep(proj, ab, conv_state_t, conv_w, alog_pad, dtb_pad):
    nb = proj.shape[0]

    def col_spec(col0):
        return pl.BlockSpec((nb, GDN_DIM), lambda i: (0, col0 // GDN_DIM))

    small = pl.BlockSpec((1, LANES), lambda i: (0, 0))
    full = pl.BlockSpec((nb, GDN_DIM), lambda i: (0, 0))
    return pl.pallas_call(
        _gdn_sample_prep_body,
        grid=(1,),
        in_specs=[col_spec(C_Q), col_spec(C_K), col_spec(C_V),
                  pl.BlockSpec((nb, LANES), lambda i: (0, 0)),
                  pl.BlockSpec((CONV_BUF, nb, QKV_DIM), lambda i: (0, 0, 0)),
                  pl.BlockSpec((CONV_WIDTH, QKV_DIM), lambda i: (0, 0)),
                  small, small],
        out_specs=[full] * 6,
        out_shape=[jax.ShapeDtypeStruct((nb, GDN_DIM), F32)] * 6,
        compiler_params=_cparams("arbitrary"),
        name="gdn_sample_prep",
    )(proj, proj, proj, ab, conv_state_t, conv_w, alog_pad, dtb_pad)


def _gdn_sample_step_body(qn_ref, kn_ref, vc_ref, eg_ref, beta_ref, qk_ref, z_ref, gn_ref, s_ref,
                          o_ref, snew_ref, *, bb):
    pad = jnp.zeros((LANES - 2 * HEADS, HEAD_DIM), F32)
    for b in range(bb):
        cols = jnp.concatenate([kn_ref[b], qn_ref[b], pad], axis=0).T
        outs = []
        for h in range(HEADS):
            kcol = cols[:, h:h + 1]
            qcol = cols[:, HEADS + h:HEADS + h + 1]
            s = s_ref[0, b, h]
            k_s = jnp.sum(s * kcol, axis=0, keepdims=True)
            q_s = jnp.sum(s * qcol, axis=0, keepdims=True)
            eg = eg_ref[b, h:h + 1, :]
            v_new = beta_ref[b, h:h + 1, :] * (vc_ref[b, h:h + 1, :] - eg * k_s)
            snew_ref[0, b, h] = eg * s + kcol * v_new
            outs.append(eg * q_s + qk_ref[b, h:h + 1, :] * v_new)
        o = jnp.concatenate(outs, axis=0)
        o_ref[b] = (_rms(o, gn_ref[...]) * _silu(z_ref[b])).astype(BF16)


def _gdn_sample_step(qn, kn, vc, eg, beta, qk, z, gdn_norm, state_ssm, *, bb):
    nb = qn.shape[0]
    body = functools.partial(_gdn_sample_step_body, bb=bb)
    vec = pl.BlockSpec((bb, HEADS, HEAD_DIM), lambda i: (i, 0, 0))
    st = pl.BlockSpec((1, bb, HEADS, HEAD_DIM, HEAD_DIM), lambda i: (0, i, 0, 0, 0))
    return pl.pallas_call(
        body,
        grid=(nb // bb,),
        in_specs=[vec] * 7 + [pl.BlockSpec((1, LANES), lambda i: (0, 0)), st],
        out_specs=[vec, st],
        out_shape=[jax.ShapeDtypeStruct((nb, HEADS, HEAD_DIM), BF16),
                   jax.ShapeDtypeStruct(state_ssm.shape, F32)],
        compiler_params=_cparams("parallel"),
        name="gdn_sample_step",
    )(qn, kn, vc, eg, beta, qk, z, gdn_norm, state_ssm)


def _merge_body(pool_ref, gdn_ref, wpu_ref, wgu_ref, gp_ref, gg_ref, o_ref):
    o_ref[...] = (gp_ref[...] * _dot(pool_ref[...], wpu_ref[...])
                  + gg_ref[...] * _dot(gdn_ref[...], wgu_ref[...])).astype(BF16)


def _merge(proj, pool_out, gdn_out, w_pool_up, w_gdn_up, *, tm, tn=512):
    m = proj.shape[0]
    nj = D_MODEL // tn
    return pl.pallas_call(
        _merge_body,
        grid=(m // tm, nj),
        in_specs=[
            pl.BlockSpec((tm, POOL_DIM), lambda i, j: (i, 0)),
            pl.BlockSpec((tm, GDN_DIM), lambda i, j: (i, 0)),
            pl.BlockSpec((POOL_DIM, tn), lambda i, j: (0, j)),
            pl.BlockSpec((GDN_DIM, tn), lambda i, j: (0, j)),
            pl.BlockSpec((tm, tn), lambda i, j: (i, C_GP // tn + j)),
            pl.BlockSpec((tm, tn), lambda i, j: (i, C_GG // tn + j)),
        ],
        out_specs=pl.BlockSpec((tm, tn), lambda i, j: (i, j)),
        out_shape=jax.ShapeDtypeStruct((m, D_MODEL), BF16),
        compiler_params=_cparams("parallel", "parallel"),
        name="merge",
    )(pool_out, gdn_out, w_pool_up, w_gdn_up, proj, proj)


def _outproj_body(a_ref, w_ref, x_ref, o_ref):
    o_ref[...] = x_ref[...] + _dot(a_ref[...], w_ref[...])


def _outproj(merged, x, w_o, *, tm, tn=512):
    m = x.shape[0]
    return pl.pallas_call(
        _outproj_body,
        grid=(m // tm, D_MODEL // tn),
        in_specs=[
            pl.BlockSpec((tm, D_MODEL), lambda i, j: (i, 0)),
            pl.BlockSpec((D_MODEL, tn), lambda i, j: (0, j)),
            pl.BlockSpec((tm, tn), lambda i, j: (i, j)),
        ],
        out_specs=pl.BlockSpec((tm, tn), lambda i, j: (i, j)),
        out_shape=jax.ShapeDtypeStruct((m, D_MODEL), F32),
        compiler_params=_cparams("parallel", "parallel"),
        name="outproj",
    )(merged, w_o, x)


def _ffn_body(x_ref, gain_ref, wg_ref, wu_ref, wd_ref, o_ref, h_ref):
    f = pl.program_id(1)

    @pl.when(f == 0)
    def _():
        x = x_ref[...]
        h_ref[...] = _rms(x, gain_ref[...]).astype(BF16)
        o_ref[...] = x

    h = h_ref[...]
    act = (_silu(_dot(h, wg_ref[...])) * _dot(h, wu_ref[...])).astype(BF16)
    o_ref[...] += _dot(act, wd_ref[...])


def _ffn(x, gain, w_gate_up, w_down, *, tm, tf=512):
    m = x.shape[0]
    nf = D_FF // tf
    return pl.pallas_call(
        _ffn_body,
        grid=(m // tm, nf),
        in_specs=[
            pl.BlockSpec((tm, D_MODEL), lambda i, f: (i, 0)),
            pl.BlockSpec((1, D_MODEL), lambda i, f: (0, 0)),
            pl.BlockSpec((D_MODEL, tf), lambda i, f: (0, f)),
            pl.BlockSpec((D_MODEL, tf), lambda i, f: (0, nf + f)),
            pl.BlockSpec((tf, D_MODEL), lambda i, f: (f, 0)),
        ],
        out_specs=pl.BlockSpec((tm, D_MODEL), lambda i, f: (i, 0)),
        out_shape=jax.ShapeDtypeStruct((m, D_MODEL), F32),
        scratch_shapes=[pltpu.VMEM((tm, D_MODEL), BF16)],
        compiler_params=_cparams("parallel", "arbitrary"),
        name="ffn",
    )(x, gain, w_gate_up, w_gate_up, w_down)


def _final_body(x_ref, p_ref, wple_ref, wpg_ref, gain_ref, o_ref):
    x = x_ref[...]
    gate = jax.nn.sigmoid(_dot(x.astype(BF16), wpg_ref[...]))
    emb = _dot(p_ref[...].astype(BF16), wple_ref[...])
    o_ref[...] = _rms(x + emb * gate, gain_ref[...])


def _final(x, p, w_ple, w_ple_gate, gain, *, tm):
    m = x.shape[0]
    return pl.pallas_call(
        _final_body,
        grid=(m // tm,),
        in_specs=[
            pl.BlockSpec((tm, D_MODEL), lambda i: (i, 0)),
            pl.BlockSpec((tm, PLE_DIM), lambda i: (i, 0)),
            pl.BlockSpec((PLE_DIM, D_MODEL), lambda i: (0, 0)),
            pl.BlockSpec((D_MODEL, D_MODEL), lambda i: (0, 0)),
            pl.BlockSpec((1, D_MODEL), lambda i: (0, 0)),
        ],
        out_specs=pl.BlockSpec((tm, D_MODEL), lambda i: (i, 0)),
        out_shape=jax.ShapeDtypeStruct((m, D_MODEL), F32),
        compiler_params=_cparams("parallel"),
        name="final",
    )(x, p, w_ple, w_ple_gate, gain)


def _tile_rows(m, want):
    return want if m % want == 0 else m


def _dense_tail(x, p, proj, pool_out, gdn_out, wts):
    m = x.shape[0]
    tm = _tile_rows(m, 1024)
    merged = _merge(proj, pool_out, gdn_out, wts["w_pool_up"], wts["w_gdn_up"], tm=tm)
    x = _outproj(merged, x, wts["w_o"], tm=tm)
    x = _ffn(x, wts["norm_ffn"], wts["w_gate_up"], wts["w_down"], tm=tm)
    return _final(x, p, wts["w_ple"], wts["w_ple_gate"], wts["norm_final"], tm=_tile_rows(m, 256))


def kernel(x_prompt, x_sample, p_prompt, p_sample, state_pool, state_conv, state_ssm, norm_mix, w_in, pool_w,
           pool_scale, conv_w, a_log, dt_bias, gdn_norm, w_pool_up, w_gdn_up, w_o, norm_ffn, w_gate_up, w_down,
           w_ple, w_ple_gate, norm_final):
    batch, seq, _ = x_prompt.shape
    nb = x_sample.shape[0]
    w_in_t = jnp.swapaxes(w_in[0], 0, 1)
    w_cat_t = jnp.concatenate([w_in_t[IN_GATES:], w_in_t[IN_QKV:IN_AB], w_in_t[:IN_QKV]], axis=0).astype(BF16)
    wts = dict(
        w_pool_up=w_pool_up[0].astype(BF16), w_gdn_up=w_gdn_up[0].astype(BF16), w_o=w_o[0].astype(BF16),
        norm_ffn=norm_ffn, w_gate_up=w_gate_up[0].astype(BF16), w_down=w_down[0].astype(BF16),
        w_ple=w_ple[0].astype(BF16), w_ple_gate=w_ple_gate[0].astype(BF16), norm_final=norm_final[None, :],
    )
    pool_w_b = pool_w[0].astype(BF16)
    lane_pad = ((0, 0), (0, LANES - HEADS))
    alog_pad = jnp.pad(a_log, lane_pad)
    dtb_pad = jnp.pad(dt_bias, lane_pad)
    gn = gdn_norm

    xp = x_prompt.reshape(batch * seq, D_MODEL)
    proj_p, ab_p = _inproj(xp, norm_mix, w_cat_t, w_in_t, tm=_tile_rows(batch * seq, 1024), tn=1024)
    pool_out_p = _pool_prompt(proj_p, pool_w_b, pool_scale, batch=batch, seq=seq, tt=_tile_rows(seq, 256))
    gdn_out_p, ssm_p = _gdn_prompt(proj_p, ab_p, conv_w[0], alog_pad, dtb_pad, gn, batch=batch, seq=seq,
                                   hg=4, tt=_tile_rows(seq, 512))
    y_p = _dense_tail(xp, p_prompt[0].reshape(batch * seq, PLE_DIM), proj_p, pool_out_p, gdn_out_p, wts)
    proj_p3 = proj_p.reshape(batch, seq, C_END)
    pool_p = proj_p3[:, seq - POOL_BUF:, C_POOL:C_END][None]
    conv_p = proj_p3[:, seq - CONV_BUF:, C_Q:C_Z][None]

    xs = x_sample.reshape(nb, D_MODEL)
    proj_s, ab_s = _inproj(xs, norm_mix, w_cat_t, w_in_t, tm=nb, tn=1024)
    pool_out_s = _pool_sample(proj_s, jnp.swapaxes(state_pool[0], 0, 1), pool_w_b, pool_scale)
    prep = _gdn_sample_prep(proj_s, ab_s, jnp.swapaxes(state_conv[0], 0, 1), conv_w[0], alog_pad, dtb_pad)
    to_heads = lambda a: a.reshape(nb, HEADS, HEAD_DIM)
    gdn_out_s, ssm_s = _gdn_sample_step(*[to_heads(a) for a in prep], to_heads(proj_s[:, C_Z:C_POOL]), gn,
                                        state_ssm, bb=2)
    y_s = _dense_tail(xs, p_sample[0].reshape(nb, PLE_DIM), proj_s, pool_out_s, gdn_out_s.reshape(nb, GDN_DIM), wts)
    pool_s = jnp.concatenate([state_pool[0][:, 1:], proj_s[:, None, C_POOL:C_END]], axis=1)[None]
    conv_s = jnp.concatenate([state_conv[0][:, 1:], proj_s[:, None, C_Q:C_Z]], axis=1)[None]

    return (y_p.reshape(batch, seq, D_MODEL), y_s.reshape(nb, 1, D_MODEL), pool_p, conv_p, ssm_p,
            pool_s, conv_s, ssm_s)
```

```python
import functools

import jax
import jax.numpy as jnp
from jax import lax
from jax.experimental import pallas as pl
from jax.experimental.pallas import tpu as pltpu

F32 = jnp.float32
BF16 = jnp.bfloat16

D_MODEL = 2048
PAST_LEN = 16384
POOL_WINDOWS = (2, 4, 8, 16)
POOL_GROUP_DIM = D_MODEL // 8
POOL_DIM = len(POOL_WINDOWS) * POOL_GROUP_DIM
POOL_BUF = max(POOL_WINDOWS) - 1
HEAD_DIM = 128
HEADS = D_MODEL // HEAD_DIM
GDN_DIM = HEADS * HEAD_DIM
QKV_DIM = 3 * GDN_DIM
CONV_WIDTH = 4
CONV_BUF = CONV_WIDTH - 1
D_FF = -(-8 * D_MODEL // (3 * 256)) * 256
PLE_DIM = 256
EPS = 1e-6

IN_QKV = POOL_DIM
IN_Z = IN_QKV + QKV_DIM
IN_AB = IN_Z + GDN_DIM
IN_GATES = IN_AB + 2 * HEADS
C_GP = 0
C_GG = C_GP + D_MODEL
C_Q = C_GG + D_MODEL
C_K = C_Q + GDN_DIM
C_V = C_K + GDN_DIM
C_Z = C_V + GDN_DIM
C_POOL = C_Z + GDN_DIM
C_END = C_POOL + POOL_DIM

LANES = 128
SUBLANES = 8
VMEM_LIMIT_BYTES = 56 * 1024 * 1024
CHUNK = 128


def _cparams(*sem):
    return pltpu.CompilerParams(dimension_semantics=sem, vmem_limit_bytes=VMEM_LIMIT_BYTES)


def _rms(x, gain):
    return x * lax.rsqrt(jnp.mean(x * x, axis=-1, keepdims=True) + EPS) * gain


def _dot(a, b):
    return jnp.dot(a, b, preferred_element_type=F32)


def _dot_nt(a, b):
    return lax.dot_general(a, b, (((1,), (1,)), ((), ())), preferred_element_type=F32)


def _silu(x):
    return x * jax.nn.sigmoid(x)


def _norm_in_body(x_ref, gain_ref, wab_ref, h_ref, ab_ref):
    h = _rms(x_ref[...], gain_ref[...]).astype(BF16)
    h_ref[...] = h
    ab_ref[...] = _dot_nt(h, wab_ref[...].astype(BF16))


def _norm_in(x, gain, w_in_t, *, tm):
    m = x.shape[0]
    return pl.pallas_call(
        _norm_in_body,
        grid=(m // tm,),
        in_specs=[
            pl.BlockSpec((tm, D_MODEL), lambda i: (i, 0)),
            pl.BlockSpec((1, D_MODEL), lambda i: (0, 0)),
            pl.BlockSpec((LANES, D_MODEL), lambda i: (IN_AB // LANES, 0)),
        ],
        out_specs=[
            pl.BlockSpec((tm, D_MODEL), lambda i: (i, 0)),
            pl.BlockSpec((tm, LANES), lambda i: (i, 0)),
        ],
        out_shape=[jax.ShapeDtypeStruct((m, D_MODEL), BF16), jax.ShapeDtypeStruct((m, LANES), F32)],
        compiler_params=_cparams("parallel"),
        name="norm_in",
    )(x, gain, w_in_t)


def _inproj_row_start(j, tn):
    n_gate = (2 * D_MODEL) // tn
    n_mid = (QKV_DIM + GDN_DIM) // tn
    row = jnp.where(j < n_gate, IN_GATES + j * tn,
                    jnp.where(j < n_gate + n_mid, IN_QKV + (j - n_gate) * tn, (j - n_gate - n_mid) * tn))
    return pl.multiple_of(row, SUBLANES)


def _inproj_body(h_ref, w_ref, o_ref, wb_ref, *, n_gate_tiles):
    j = pl.program_id(0)

    @pl.when(pl.program_id(1) == 0)
    def _():
        wb_ref[...] = w_ref[...].astype(BF16)

    acc = _dot_nt(h_ref[...], wb_ref[...])

    @pl.when(j < n_gate_tiles)
    def _():
        o_ref[...] = jax.nn.sigmoid(acc)

    @pl.when(j >= n_gate_tiles)
    def _():
        o_ref[...] = acc


def _inproj(h, w_in_t, *, tm, tn):
    m = h.shape[0]
    body = functools.partial(_inproj_body, n_gate_tiles=(2 * D_MODEL) // tn)
    return pl.pallas_call(
        body,
        grid=(C_END // tn, m // tm),
        in_specs=[
            pl.BlockSpec((tm, D_MODEL), lambda j, i: (i, 0)),
            pl.BlockSpec((pl.Element(tn), pl.Element(D_MODEL)), lambda j, i: (_inproj_row_start(j, tn), 0)),
        ],
        out_specs=pl.BlockSpec((tm, tn), lambda j, i: (i, j)),
        out_shape=jax.ShapeDtypeStruct((m, C_END), F32),
        scratch_shapes=[pltpu.VMEM((tn, D_MODEL), BF16)],
        compiler_params=_cparams("arbitrary", "arbitrary"),
        name="inproj",
    )(h, w_in_t)


def _pool_group_out(mean, tok, pw_ref, scale_ref, gi):
    cols = slice(gi * POOL_GROUP_DIM, (gi + 1) * POOL_GROUP_DIM)
    y = _dot((mean - tok).astype(BF16), pw_ref[gi])
    return (y * scale_ref[:, cols]).astype(BF16)


def _pool_prompt_body(u_ref, pw_ref, scale_ref, o_ref, ext_ref, *, tt, start_pos):
    t = pl.program_id(1)
    halo = POOL_BUF + 1

    @pl.when(t == 0)
    def _():
        ext_ref[0:halo, :] = jnp.zeros((halo, POOL_DIM), F32)

    ext_ref[halo:halo + tt, :] = u_ref[...]
    pos = start_pos + t * tt + lax.broadcasted_iota(jnp.int32, (tt, 1), 0)
    for gi, w in enumerate(POOL_WINDOWS):
        cols = slice(gi * POOL_GROUP_DIM, (gi + 1) * POOL_GROUP_DIM)
        tok = ext_ref[halo:halo + tt, cols]
        tot = tok
        for i in range(1, w):
            tot = tot + ext_ref[pl.ds(halo - i, tt), cols]
        cnt = jnp.minimum(pos + 1, w).astype(F32)
        o_ref[:, cols] = _pool_group_out(tot / cnt, tok, pw_ref, scale_ref, gi)
    ext_ref[0:halo, :] = ext_ref[tt:tt + halo, :]


def _pool_prompt(proj, pool_w, pool_scale, *, batch, seq, tt):
    nt = seq // tt
    body = functools.partial(_pool_prompt_body, tt=tt, start_pos=0)
    return pl.pallas_call(
        body,
        grid=(batch, nt),
        in_specs=[
            pl.BlockSpec((tt, POOL_DIM), lambda b, t: (b * nt + t, C_POOL // POOL_DIM)),
            pl.BlockSpec((len(POOL_WINDOWS), POOL_GROUP_DIM, POOL_GROUP_DIM), lambda b, t: (0, 0, 0)),
            pl.BlockSpec((1, POOL_DIM), lambda b, t: (0, 0)),
        ],
        out_specs=pl.BlockSpec((tt, POOL_DIM), lambda b, t: (b * nt + t, 0)),
        out_shape=jax.ShapeDtypeStruct((batch * seq, POOL_DIM), BF16),
        scratch_shapes=[pltpu.VMEM((POOL_BUF + 1 + tt, POOL_DIM), F32)],
        compiler_params=_cparams("parallel", "arbitrary"),
        name="pool_prompt",
    )(proj, pool_w, pool_scale)


def _pool_sample_body(u_ref, st_ref, pw_ref, scale_ref, o_ref, *, start_pos):
    for gi, w in enumerate(POOL_WINDOWS):
        cols = slice(gi * POOL_GROUP_DIM, (gi + 1) * POOL_GROUP_DIM)
        tok = u_ref[:, cols]
        tot = tok
        for i in range(1, w):
            tot = tot + st_ref[POOL_BUF - i, :, cols]
        cnt = float(min(start_pos + 1, w))
        o_ref[:, cols] = _pool_group_out(tot / cnt, tok, pw_ref, scale_ref, gi)


def _pool_sample(proj, state_t, pool_w, pool_scale):
    nb = proj.shape[0]
    body = functools.partial(_pool_sample_body, start_pos=PAST_LEN)
    return pl.pallas_call(
        body,
        grid=(1,),
        in_specs=[
            pl.BlockSpec((nb, POOL_DIM), lambda i: (0, C_POOL // POOL_DIM)),
            pl.BlockSpec((POOL_BUF, nb, POOL_DIM), lambda i: (0, 0, 0)),
            pl.BlockSpec((len(POOL_WINDOWS), POOL_GROUP_DIM, POOL_GROUP_DIM), lambda i: (0, 0, 0)),
            pl.BlockSpec((1, POOL_DIM), lambda i: (0, 0)),
        ],
        out_specs=pl.BlockSpec((nb, POOL_DIM), lambda i: (0, 0)),
        out_shape=jax.ShapeDtypeStruct((nb, POOL_DIM), BF16),
        compiler_params=_cparams("arbitrary"),
        name="pool_sample",
    )(proj, state_t, pool_w, pool_scale)


def _split3(x):
    hi = x.astype(BF16)
    r1 = x - hi.astype(F32)
    mid = r1.astype(BF16)
    lo = (r1 - mid.astype(F32)).astype(BF16)
    return hi, mid, lo


def _split2(x):
    hi = x.astype(BF16)
    return hi, (x - hi.astype(F32)).astype(BF16)


def _softplus(x):
    return jnp.maximum(x, 0.0) + jnp.log1p(jnp.exp(-jnp.abs(x)))


def _unit_lower_inverse_minus_identity(mats):
    qs = [-a for a in mats]
    rps = list(qs)
    levels = CHUNK.bit_length() - 1
    qbs = [q.astype(BF16) for q in qs]
    qs = [_dot(qb, qb) for qb in qbs]
    for j in range(1, levels):
        qbs = [q.astype(BF16) for q in qs]
        if j < levels - 1:
            ps = [_dot(qb, jnp.concatenate([q, rp], axis=1).astype(BF16)) for qb, q, rp in zip(qbs, qs, rps)]
            rps = [rp + q + p[:, CHUNK:] for rp, q, p in zip(rps, qs, ps)]
            qs = [p[:, :CHUNK] for p in ps]
        else:
            rps = [rp + q + _dot(qb, rp.astype(BF16)) for qb, q, rp in zip(qbs, qs, rps)]
    return rps


def _gdn_prompt_body(q_ref, k_ref, v_ref, z_ref, ab_ref, cwq_ref, cwk_ref, cwv_ref, alog_ref, dtb_ref, gn_ref,
                     o_ref, ssm_ref, ext_ref, s_ref, gct_ref, *, hg, tt):
    hgi = pl.program_id(1)
    t = pl.program_id(2)
    w = hg * HEAD_DIM
    halo = SUBLANES

    @pl.when(t == 0)
    def _():
        ext_ref[0:halo, :] = jnp.zeros((halo, 3 * w), F32)
        s_ref[...] = jnp.zeros_like(s_ref)

    ext_ref[halo:halo + tt, 0:w] = q_ref[...]
    ext_ref[halo:halo + tt, w:2 * w] = k_ref[...]
    ext_ref[halo:halo + tt, 2 * w:3 * w] = v_ref[...]

    def conv(part, cw_ref):
        cols = slice(part * w, (part + 1) * w)
        acc = ext_ref[halo:halo + tt, cols] * cw_ref[CONV_BUF:CONV_WIDTH, :]
        for i in range(CONV_BUF):
            acc = acc + ext_ref[pl.ds(halo - CONV_BUF + i, tt), cols] * cw_ref[i:i + 1, :]
        return _silu(acc)

    qc = conv(0, cwq_ref)
    kc = conv(1, cwk_ref)
    vc = conv(2, cwv_ref)
    ext_ref[0:halo, :] = ext_ref[tt:tt + halo, :]

    ab = ab_ref[...]
    g_all = -jnp.exp(alog_ref[...]) * _softplus(ab + dtb_ref[...])
    beta_all = jax.nn.sigmoid(ab)
    ri = lax.broadcasted_iota(jnp.int32, (tt, tt), 0)
    ci = lax.broadcasted_iota(jnp.int32, (tt, tt), 1)
    shift = CHUNK.bit_length() - 1
    tri = jnp.where(((ri >> shift) == (ci >> shift)) & (ci <= ri), 1.0, 0.0).astype(BF16)
    g_hi, g_mid, g_lo = _split3(g_all)
    gc_all = _dot(tri, g_hi) + _dot(tri, g_mid) + _dot(tri, g_lo)
    gct_ref[...] = gc_all.T

    lane = lax.broadcasted_iota(jnp.int32, (tt, LANES), 1)
    r128 = lax.broadcasted_iota(jnp.int32, (CHUNK, CHUNK), 0)
    c128 = lax.broadcasted_iota(jnp.int32, (CHUNK, CHUNK), 1)
    sub8 = lax.broadcasted_iota(jnp.int32, (SUBLANES, CHUNK), 0)
    n_chunks = tt // CHUNK
    chains = [(hh, c) for hh in range(hg) for c in range(n_chunks)]
    gcol_heads, bcol_heads = [], []
    for hh in range(hg):
        head = hgi * hg + hh
        gcol_heads.append(jnp.sum(jnp.where(lane == head, gc_all, 0.0), axis=-1, keepdims=True))
        bcol_heads.append(jnp.sum(jnp.where(lane == head + HEADS, beta_all, 0.0), axis=-1, keepdims=True))

    qn_l, kn_l, kb_l, rhs_l, eg_l, gcol_l, dec_l = [], [], [], [], [], [], []
    for hh, c in chains:
        head = hgi * hg + hh
        rows = slice(c * CHUNK, (c + 1) * CHUNK)
        lanes = slice(hh * HEAD_DIM, (hh + 1) * HEAD_DIM)
        qh, kh, vh = qc[rows, lanes], kc[rows, lanes], vc[rows, lanes]
        qn = qh * lax.rsqrt(jnp.sum(qh * qh, axis=-1, keepdims=True) + EPS) * (HEAD_DIM ** -0.5)
        kn = kh * lax.rsqrt(jnp.sum(kh * kh, axis=-1, keepdims=True) + EPS)
        gcol = gcol_heads[hh][rows]
        bcol = bcol_heads[hh][rows]
        gblk = gct_ref[pl.ds(pl.multiple_of((head >> 3) * SUBLANES, SUBLANES), SUBLANES), rows]
        grow = jnp.sum(jnp.where(sub8 == (head & (SUBLANES - 1)), gblk, 0.0), axis=0, keepdims=True)
        eg = jnp.exp(gcol)
        kb = kn * bcol
        qn_l.append(qn)
        kn_l.append(kn)
        kb_l.append(kb)
        eg_l.append(eg)
        gcol_l.append(gcol)
        dec_l.append(jnp.exp(jnp.minimum(gcol - grow, 0.0)))
        rhs_l.append(jnp.concatenate([vh * bcol, kb * eg], axis=1))

    a_l, qk_l = [], []
    for qn, kn, kb, decay in zip(qn_l, kn_l, kb_l, dec_l):
        knb = kn.astype(BF16)
        a_l.append(_dot_nt(kb.astype(BF16), knb) * jnp.where(c128 < r128, decay, 0.0))
        qk_l.append(_dot_nt(qn.astype(BF16), knb) * jnp.where(c128 <= r128, decay, 0.0))
    tinv_l = [x.astype(BF16) for x in _unit_lower_inverse_minus_identity(a_l)]
    sol_l = [rhs + _dot(tinv, rhs.astype(BF16)) for tinv, rhs in zip(tinv_l, rhs_l)]
    res_l = []
    for a, sol, rhs in zip(a_l, sol_l, rhs_l):
        a_hi, a_lo = _split2(a)
        s_hi, s_lo = _split2(sol)
        a_sol = _dot(jnp.concatenate([a_hi, a_lo, a_hi], axis=1), jnp.concatenate([s_hi, s_hi, s_lo], axis=0))
        res_l.append((rhs - sol) - a_sol)
    sol_l = [(sol + res + _dot(tinv, res.astype(BF16))).astype(BF16) for sol, res, tinv in zip(sol_l, res_l, tinv_l)]
    lhs_l, o_add_l, s_add_l, glast_l = [], [], [], []
    for qn, kn, eg, gcol, qk, sol_b in zip(qn_l, kn_l, eg_l, gcol_l, qk_l, sol_l):
        glast = gcol[CHUNK - 1:CHUNK, :]
        ktail = kn * jnp.exp(glast - gcol)
        kt_sol = _dot(ktail.T.astype(BF16), sol_b)
        qk_sol = _dot(qk.astype(BF16), sol_b)
        lhs_l.append(jnp.concatenate([qn * eg - qk_sol[:, HEAD_DIM:], kt_sol[:, HEAD_DIM:]], axis=0).astype(BF16))
        o_add_l.append(qk_sol[:, :HEAD_DIM])
        s_add_l.append(kt_sol[:, :HEAD_DIM])
        glast_l.append(jnp.exp(glast))
    states = [s_ref[hh] for hh in range(hg)]
    for c in range(n_chunks):
        for hh in range(hg):
            i = hh * n_chunks + c
            rows = slice(c * CHUNK, (c + 1) * CHUNK)
            lanes = slice(hh * HEAD_DIM, (hh + 1) * HEAD_DIM)
            on_s = _dot(lhs_l[i], states[hh].astype(BF16))
            o = on_s[:CHUNK] + o_add_l[i]
            states[hh] = states[hh] * glast_l[i] - on_s[CHUNK:] + s_add_l[i]
            o_ref[rows, lanes] = (_rms(o, gn_ref[...]) * _silu(z_ref[rows, lanes])).astype(BF16)
    for hh in range(hg):
        s_ref[hh] = states[hh]

    @pl.when(t == pl.num_programs(2) - 1)
    def _():
        ssm_ref[0, 0] = s_ref[...]


def _gdn_prompt(proj, ab, conv_w, alog_pad, dtb_pad, gdn_norm, *, batch, seq, hg, tt):
    nt = seq // tt
    w = hg * HEAD_DIM
    body = functools.partial(_gdn_prompt_body, hg=hg, tt=tt)

    def col_spec(col0):
        return pl.BlockSpec((tt, w), lambda b, h, t: (b * nt + t, col0 // w + h))

    def cw_spec(part):
        return pl.BlockSpec((CONV_WIDTH, w), lambda b, h, t: (0, part * GDN_DIM // w + h))

    small = pl.BlockSpec((1, LANES), lambda b, h, t: (0, 0))
    return pl.pallas_call(
        body,
        grid=(batch, HEADS // hg, nt),
        in_specs=[col_spec(C_Q), col_spec(C_K), col_spec(C_V), col_spec(C_Z),
                  pl.BlockSpec((tt, LANES), lambda b, h, t: (b * nt + t, 0)),
                  cw_spec(0), cw_spec(1), cw_spec(2), small, small, small],
        out_specs=[
            pl.BlockSpec((tt, w), lambda b, h, t: (b * nt + t, h)),
            pl.BlockSpec((1, 1, hg, HEAD_DIM, HEAD_DIM), lambda b, h, t: (0, b, h, 0, 0)),
        ],
        out_shape=[
            jax.ShapeDtypeStruct((batch * seq, GDN_DIM), BF16),
            jax.ShapeDtypeStruct((1, batch, HEADS, HEAD_DIM, HEAD_DIM), F32),
        ],
        scratch_shapes=[
            pltpu.VMEM((SUBLANES + tt, 3 * w), F32),
            pltpu.VMEM((hg, HEAD_DIM, HEAD_DIM), F32),
            pltpu.VMEM((LANES, tt), F32),
        ],
        compiler_params=_cparams("parallel", "parallel", "arbitrary"),
        name="gdn_prompt",
    )(proj, proj, proj, proj, ab, conv_w, conv_w, conv_w, alog_pad, dtb_pad, gdn_norm)


def _gdn_sample_prep_body(q_ref, k_ref, v_ref, ab_ref, st_ref, cw_ref, alog_ref, dtb_ref,
                          qn_ref, kn_ref, vc_ref, eg_ref, beta_ref, qk_ref):
    def conv(part, u_ref):
        cols = slice(part * GDN_DIM, (part + 1) * GDN_DIM)
        acc = u_ref[...] * cw_ref[CONV_BUF:CONV_WIDTH, cols]
        for i in range(CONV_BUF):
            acc = acc + st_ref[i, :, cols] * cw_ref[i:i + 1, cols]
        return _silu(acc)

    qc = conv(0, q_ref)
    kc = conv(1, k_ref)
    vc_ref[...] = conv(2, v_ref)
    ab = ab_ref[...]
    g_all = -jnp.exp(alog_ref[...]) * _softplus(ab + dtb_ref[...])
    eg_all = jnp.exp(g_all)
    beta_all = jax.nn.sigmoid(ab)
    for h in range(HEADS):
        lanes = slice(h * HEAD_DIM, (h + 1) * HEAD_DIM)
        qh, kh = qc[:, lanes], kc[:, lanes]
        qn = qh * lax.rsqrt(jnp.sum(qh * qh, axis=-1, keepdims=True) + EPS) * (HEAD_DIM ** -0.5)
        kn = kh * lax.rsqrt(jnp.sum(kh * kh, axis=-1, keepdims=True) + EPS)
        qn_ref[:, lanes] = qn
        kn_ref[:, lanes] = kn
        shape = qn.shape
        qk_ref[:, lanes] = jnp.broadcast_to(jnp.sum(qn * kn, axis=-1, keepdims=True), shape)
        eg_ref[:, lanes] = jnp.broadcast_to(eg_all[:, h:h + 1], shape)
        beta_ref[:, lanes] = jnp.broadcast_to(beta_all[:, HEADS + h:HEADS + h + 1], shape)


def _gdn_sample_prep(proj, ab, conv_state_t, conv_w, alog_pad, dtb_pad):
    nb = proj.shape[0]

    def col_spec(col0):
        return pl.BlockSpec((nb, GDN_DIM), lambda i: (0, col0 // GDN_DIM))

    small = pl.BlockSpec((1, LANES), lambda i: (0, 0))
    full = pl.BlockSpec((nb, GDN_DIM), lambda i: (0, 0))
    return pl.pallas_call(
        _gdn_sample_prep_body,
        grid=(1,),
        in_specs=[col_spec(C_Q), col_spec(C_K), col_spec(C_V),
                  pl.BlockSpec((nb, LANES), lambda i: (0, 0)),
                  pl.BlockSpec((CONV_BUF, nb, QKV_DIM), lambda i: (0, 0, 0)),
                  pl.BlockSpec((CONV_WIDTH, QKV_DIM), lambda i: (0, 0)),
                  small, small],
        out_specs=[full] * 6,
        out_shape=[jax.ShapeDtypeStruct((nb, GDN_DIM), F32)] * 6,
        compiler_params=_cparams("arbitrary"),
        name="gdn_sample_prep",
    )(proj, proj, proj, ab, conv_state_t, conv_w, alog_pad, dtb_pad)


def _gdn_sample_step_body(qn_ref, kn_ref, vc_ref, eg_ref, beta_ref, qk_ref, z_ref, gn_ref, s_ref,
                          o_ref, snew_ref, *, bb):
    pad = jnp.zeros((LANES - 2 * HEADS, HEAD_DIM), F32)
    for b in range(bb):
        cols = jnp.concatenate([kn_ref[b], qn_ref[b], pad], axis=0).T
        outs = []
        for h in range(HEADS):
            kcol = cols[:, h:h + 1]
            qcol = cols[:, HEADS + h:HEADS + h + 1]
            s = s_ref[0, b, h]
            k_s = jnp.sum(s * kcol, axis=0, keepdims=True)
            q_s = jnp.sum(s * qcol, axis=0, keepdims=True)
            eg = eg_ref[b, h:h + 1, :]
            v_new = beta_ref[b, h:h + 1, :] * (vc_ref[b, h:h + 1, :] - eg * k_s)
            snew_ref[0, b, h] = eg * s + kcol * v_new
            outs.append(eg * q_s + qk_ref[b, h:h + 1, :] * v_new)
        o = jnp.concatenate(outs, axis=0)
        o_ref[b] = (_rms(o, gn_ref[...]) * _silu(z_ref[b])).astype(BF16)


def _gdn_sample_step(qn, kn, vc, eg, beta, qk, z, gdn_norm, state_ssm, *, bb):
    nb = qn.shape[0]
    body = functools.partial(_gdn_sample_step_body, bb=bb)
    vec = pl.BlockSpec((bb, HEADS, HEAD_DIM), lambda i: (i, 0, 0))
    st = pl.BlockSpec((1, bb, HEADS, HEAD_DIM, HEAD_DIM), lambda i: (0, i, 0, 0, 0))
    return pl.pallas_call(
        body,
        grid=(nb // bb,),
        in_specs=[vec] * 7 + [pl.BlockSpec((1, LANES), lambda i: (0, 0)), st],
        out_specs=[vec, st],
        out_shape=[jax.ShapeDtypeStruct((nb, HEADS, HEAD_DIM), BF16),
                   jax.ShapeDtypeStruct(state_ssm.shape, F32)],
        compiler_params=_cparams("parallel"),
        name="gdn_sample_step",
    )(qn, kn, vc, eg, beta, qk, z, gdn_norm, state_ssm)


def _merge_body(pool_ref, gdn_ref, wpu_ref, wgu_ref, gp_ref, gg_ref, o_ref):
    o_ref[...] = (gp_ref[...] * _dot(pool_ref[...], wpu_ref[...])
                  + gg_ref[...] * _dot(gdn_ref[...], wgu_ref[...])).astype(BF16)


def _merge(proj, pool_out, gdn_out, w_pool_up, w_gdn_up, *, tm, tn=512):
    m = proj.shape[0]
    nj = D_MODEL // tn
    return pl.pallas_call(
        _merge_body,
        grid=(m // tm, nj),
        in_specs=[
            pl.BlockSpec((tm, POOL_DIM), lambda i, j: (i, 0)),
            pl.BlockSpec((tm, GDN_DIM), lambda i, j: (i, 0)),
            pl.BlockSpec((POOL_DIM, tn), lambda i, j: (0, j)),
            pl.BlockSpec((GDN_DIM, tn), lambda i, j: (0, j)),
            pl.BlockSpec((tm, tn), lambda i, j: (i, C_GP // tn + j)),
            pl.BlockSpec((tm, tn), lambda i, j: (i, C_GG // tn + j)),
        ],
        out_specs=pl.BlockSpec((tm, tn), lambda i, j: (i, j)),
        out_shape=jax.ShapeDtypeStruct((m, D_MODEL), BF16),
        compiler_params=_cparams("parallel", "parallel"),
        name="merge",
    )(pool_out, gdn_out, w_pool_up, w_gdn_up, proj, proj)


def _outproj_body(a_ref, w_ref, x_ref, o_ref):
    o_ref[...] = x_ref[...] + _dot(a_ref[...], w_ref[...])


def _outproj(merged, x, w_o, *, tm, tn=512):
    m = x.shape[0]
    return pl.pallas_call(
        _outproj_body,
        grid=(m // tm, D_MODEL // tn),
        in_specs=[
            pl.BlockSpec((tm, D_MODEL), lambda i, j: (i, 0)),
            pl.BlockSpec((D_MODEL, tn), lambda i, j: (0, j)),
            pl.BlockSpec((tm, tn), lambda i, j: (i, j)),
        ],
        out_specs=pl.BlockSpec((tm, tn), lambda i, j: (i, j)),
        out_shape=jax.ShapeDtypeStruct((m, D_MODEL), F32),
        compiler_params=_cparams("parallel", "parallel"),
        name="outproj",
    )(merged, w_o, x)


def _ffn_body(x_ref, gain_ref, wg_ref, wu_ref, wd_ref, o_ref, h_ref):
    f = pl.program_id(1)

    @pl.when(f == 0)
    def _():
        x = x_ref[...]
        h_ref[...] = _rms(x, gain_ref[...]).astype(BF16)
        o_ref[...] = x

    h = h_ref[...]
    act = (_silu(_dot(h, wg_ref[...])) * _dot(h, wu_ref[...])).astype(BF16)
    o_ref[...] += _dot(act, wd_ref[...])


def _ffn(x, gain, w_gate_up, w_down, *, tm, tf=512):
    m = x.shape[0]
    nf = D_FF // tf
    return pl.pallas_call(
        _ffn_body,
        grid=(m // tm, nf),
        in_specs=[
            pl.BlockSpec((tm, D_MODEL), lambda i, f: (i, 0)),
            pl.BlockSpec((1, D_MODEL), lambda i, f: (0, 0)),
            pl.BlockSpec((D_MODEL, tf), lambda i, f: (0, f)),
            pl.BlockSpec((D_MODEL, tf), lambda i, f: (0, nf + f)),
            pl.BlockSpec((tf, D_MODEL), lambda i, f: (f, 0)),
        ],
        out_specs=pl.BlockSpec((tm, D_MODEL), lambda i, f: (i, 0)),
        out_shape=jax.ShapeDtypeStruct((m, D_MODEL), F32),
        scratch_shapes=[pltpu.VMEM((tm, D_MODEL), BF16)],
        compiler_params=_cparams("parallel", "arbitrary"),
        name="ffn",
    )(x, gain, w_gate_up, w_gate_up, w_down)


def _final_body(x_ref, p_ref, wple_ref, wpg_ref, gain_ref, o_ref):
    x = x_ref[...]
    gate = jax.nn.sigmoid(_dot(x.astype(BF16), wpg_ref[...]))
    emb = _dot(p_ref[...].astype(BF16), wple_ref[...])
    o_ref[...] = _rms(x + emb * gate, gain_ref[...])


def _final(x, p, w_ple, w_ple_gate, gain, *, tm):
    m = x.shape[0]
    return pl.pallas_call(
        _final_body,
        grid=(m // tm,),
        in_specs=[
            pl.BlockSpec((tm, D_MODEL), lambda i: (i, 0)),
            pl.BlockSpec((tm, PLE_DIM), lambda i: (i, 0)),
            pl.BlockSpec((PLE_DIM, D_MODEL), lambda i: (0, 0)),
            pl.BlockSpec((D_MODEL, D_MODEL), lambda i: (0, 0)),
            pl.BlockSpec((1, D_MODEL), lambda i: (0, 0)),
        ],
        out_specs=pl.BlockSpec((tm, D_MODEL), lambda i: (i, 0)),
        out_shape=jax.ShapeDtypeStruct((m, D_MODEL), F32),
        compiler_params=_cparams("parallel"),
        name="final",
    )(x, p, w_ple, w_ple_gate, gain)


def _tile_rows(m, want):
    return want if m % want == 0 else m


def _dense_tail(x, p, proj, pool_out, gdn_out, wts):
    m = x.shape[0]
    tm = _tile_rows(m, 1024)
    merged = _merge(proj, pool_out, gdn_out, wts["w_pool_up"], wts["w_gdn_up"], tm=tm)
    x = _outproj(merged, x, wts["w_o"], tm=tm)
    x = _ffn(x, wts["norm_ffn"], wts["w_gate_up"], wts["w_down"], tm=tm)
    return _final(x, p, wts["w_ple"], wts["w_ple_gate"], wts["norm_final"], tm=_tile_rows(m, 256))


def kernel(x_prompt, x_sample, p_prompt, p_sample, state_pool, state_conv, state_ssm, norm_mix, w_in, pool_w,
           pool_scale, conv_w, a_log, dt_bias, gdn_norm, w_pool_up, w_gdn_up, w_o, norm_ffn, w_gate_up, w_down,
           w_ple, w_ple_gate, norm_final):
    batch, seq, _ = x_prompt.shape
    nb = x_sample.shape[0]
    w_in_t = jnp.swapaxes(w_in[0], 0, 1)
    wts = dict(
        w_pool_up=w_pool_up[0].astype(BF16), w_gdn_up=w_gdn_up[0].astype(BF16), w_o=w_o[0].astype(BF16),
        norm_ffn=norm_ffn, w_gate_up=w_gate_up[0].astype(BF16), w_down=w_down[0].astype(BF16),
        w_ple=w_ple[0].astype(BF16), w_ple_gate=w_ple_gate[0].astype(BF16), norm_final=norm_final[None, :],
    )
    pool_w_b = pool_w[0].astype(BF16)
    lane_pad = ((0, 0), (0, LANES - HEADS))
    alog_pad = jnp.pad(a_log, lane_pad)
    dtb_pad = jnp.pad(dt_bias, lane_pad)
    gn = gdn_norm

    xp = x_prompt.reshape(batch * seq, D_MODEL)
    h_p, ab_p = _norm_in(xp, norm_mix, w_in_t, tm=_tile_rows(batch * seq, 512))
    proj_p = _inproj(h_p, w_in_t, tm=_tile_rows(batch * seq, 1024), tn=1024)
    pool_out_p = _pool_prompt(proj_p, pool_w_b, pool_scale, batch=batch, seq=seq, tt=_tile_rows(seq, 256))
    gdn_out_p, ssm_p = _gdn_prompt(proj_p, ab_p, conv_w[0], alog_pad, dtb_pad, gn, batch=batch, seq=seq,
                                   hg=4, tt=_tile_rows(seq, 512))
    y_p = _dense_tail(xp, p_prompt[0].reshape(batch * seq, PLE_DIM), proj_p, pool_out_p, gdn_out_p, wts)
    proj_p3 = proj_p.reshape(batch, seq, C_END)
    pool_p = proj_p3[:, seq - POOL_BUF:, C_POOL:C_END][None]
    conv_p = proj_p3[:, seq - CONV_BUF:, C_Q:C_Z][None]

    xs = x_sample.reshape(nb, D_MODEL)
    h_s, ab_s = _norm_in(xs, norm_mix, w_in_t, tm=nb)
    proj_s = _inproj(h_s, w_in_t, tm=nb, tn=1024)
    pool_out_s = _pool_sample(proj_s, jnp.swapaxes(state_pool[0], 0, 1), pool_w_b, pool_scale)
    prep = _gdn_sample_prep(proj_s, ab_s, jnp.swapaxes(state_conv[0], 0, 1), conv_w[0], alog_pad, dtb_pad)
    to_heads = lambda a: a.reshape(nb, HEADS, HEAD_DIM)
    gdn_out_s, ssm_s = _gdn_sample_step(*[to_heads(a) for a in prep], to_heads(proj_s[:, C_Z:C_POOL]), gn,
                                        state_ssm, bb=2)
    y_s = _dense_tail(xs, p_sample[0].reshape(nb, PLE_DIM), proj_s, pool_out_s, gdn_out_s.reshape(nb, GDN_DIM), wts)
    pool_s = jnp.concatenate([state_pool[0][:, 1:], proj_s[:, None, C_POOL:C_END]], axis=1)[None]
    conv_s = jnp.concatenate([state_conv[0][:, 1:], proj_s[:, None, C_Q:C_Z]], axis=1)[None]

    return (y_p.reshape(batch, seq, D_MODEL), y_s.reshape(nb, 1, D_MODEL), pool_p, conv_p, ssm_p,
            pool_s, conv_s, ssm_s)
```

```python
import functools

import jax
import jax.numpy as jnp
from jax import lax
from jax.experimental import pallas as pl
from jax.experimental.pallas import tpu as pltpu

F32 = jnp.float32
BF16 = jnp.bfloat16

D_MODEL = 2048
PAST_LEN = 16384
POOL_WINDOWS = (2, 4, 8, 16)
POOL_GROUP_DIM = D_MODEL // 8
POOL_DIM = len(POOL_WINDOWS) * POOL_GROUP_DIM
POOL_BUF = max(POOL_WINDOWS) - 1
HEAD_DIM = 128
HEADS = D_MODEL // HEAD_DIM
GDN_DIM = HEADS * HEAD_DIM
QKV_DIM = 3 * GDN_DIM
CONV_WIDTH = 4
CONV_BUF = CONV_WIDTH - 1
D_FF = -(-8 * D_MODEL // (3 * 256)) * 256
PLE_DIM = 256
EPS = 1e-6

IN_QKV = POOL_DIM
IN_Z = IN_QKV + QKV_DIM
IN_AB = IN_Z + GDN_DIM
IN_GATES = IN_AB + 2 * HEADS
C_GP = 0
C_GG = C_GP + D_MODEL
C_Q = C_GG + D_MODEL
C_K = C_Q + GDN_DIM
C_V = C_K + GDN_DIM
C_Z = C_V + GDN_DIM
C_POOL = C_Z + GDN_DIM
C_END = C_POOL + POOL_DIM

LANES = 128
SUBLANES = 8
VMEM_LIMIT_BYTES = 56 * 1024 * 1024
CHUNK = 128


def _cparams(*sem):
    return pltpu.CompilerParams(dimension_semantics=sem, vmem_limit_bytes=VMEM_LIMIT_BYTES)


def _rms(x, gain):
    return x * lax.rsqrt(jnp.mean(x * x, axis=-1, keepdims=True) + EPS) * gain


def _dot(a, b):
    return jnp.dot(a, b, preferred_element_type=F32)


def _dot_nt(a, b):
    return lax.dot_general(a, b, (((1,), (1,)), ((), ())), preferred_element_type=F32)


def _silu(x):
    return x * jax.nn.sigmoid(x)


def _norm_in_body(x_ref, gain_ref, wab_ref, h_ref, ab_ref):
    h = _rms(x_ref[...], gain_ref[...]).astype(BF16)
    h_ref[...] = h
    ab_ref[...] = _dot_nt(h, wab_ref[...].astype(BF16))


def _norm_in(x, gain, w_in_t, *, tm):
    m = x.shape[0]
    return pl.pallas_call(
        _norm_in_body,
        grid=(m // tm,),
        in_specs=[
            pl.BlockSpec((tm, D_MODEL), lambda i: (i, 0)),
            pl.BlockSpec((1, D_MODEL), lambda i: (0, 0)),
            pl.BlockSpec((LANES, D_MODEL), lambda i: (IN_AB // LANES, 0)),
        ],
        out_specs=[
            pl.BlockSpec((tm, D_MODEL), lambda i: (i, 0)),
            pl.BlockSpec((tm, LANES), lambda i: (i, 0)),
        ],
        out_shape=[jax.ShapeDtypeStruct((m, D_MODEL), BF16), jax.ShapeDtypeStruct((m, LANES), F32)],
        compiler_params=_cparams("parallel"),
        name="norm_in",
    )(x, gain, w_in_t)


def _inproj_row_start(j, tn):
    n_gate = (2 * D_MODEL) // tn
    n_mid = (QKV_DIM + GDN_DIM) // tn
    row = jnp.where(j < n_gate, IN_GATES + j * tn,
                    jnp.where(j < n_gate + n_mid, IN_QKV + (j - n_gate) * tn, (j - n_gate - n_mid) * tn))
    return pl.multiple_of(row, SUBLANES)


def _inproj_body(h_ref, w_ref, o_ref, wb_ref, *, n_gate_tiles):
    j = pl.program_id(0)

    @pl.when(pl.program_id(1) == 0)
    def _():
        wb_ref[...] = w_ref[...].astype(BF16)

    @pl.when(j < n_gate_tiles)
    def _():
        o_ref[...] = jax.nn.sigmoid(_dot_nt(h_ref[...], wb_ref[...]))

    @pl.when(j >= n_gate_tiles)
    def _():
        o_ref[...] = _dot_nt(h_ref[...], wb_ref[...])


def _inproj(h, w_in_t, *, tm, tn):
    m = h.shape[0]
    body = functools.partial(_inproj_body, n_gate_tiles=(2 * D_MODEL) // tn)
    return pl.pallas_call(
        body,
        grid=(C_END // tn, m // tm),
        in_specs=[
            pl.BlockSpec((tm, D_MODEL), lambda j, i: (i, 0)),
            pl.BlockSpec((pl.Element(tn), pl.Element(D_MODEL)), lambda j, i: (_inproj_row_start(j, tn), 0)),
        ],
        out_specs=pl.BlockSpec((tm, tn), lambda j, i: (i, j)),
        out_shape=jax.ShapeDtypeStruct((m, C_END), F32),
        scratch_shapes=[pltpu.VMEM((tn, D_MODEL), BF16)],
        compiler_params=_cparams("arbitrary", "arbitrary"),
        name="inproj",
    )(h, w_in_t)


def _pool_group_out(mean, tok, pw_ref, scale_ref, gi):
    cols = slice(gi * POOL_GROUP_DIM, (gi + 1) * POOL_GROUP_DIM)
    y = _dot((mean - tok).astype(BF16), pw_ref[gi])
    return (y * scale_ref[:, cols]).astype(BF16)


def _pool_prompt_body(u_ref, pw_ref, scale_ref, o_ref, ext_ref, *, tt, start_pos):
    t = pl.program_id(1)
    halo = POOL_BUF + 1

    @pl.when(t == 0)
    def _():
        ext_ref[0:halo, :] = jnp.zeros((halo, POOL_DIM), F32)

    ext_ref[halo:halo + tt, :] = u_ref[...]
    pos = start_pos + t * tt + lax.broadcasted_iota(jnp.int32, (tt, 1), 0)
    for gi, w in enumerate(POOL_WINDOWS):
        cols = slice(gi * POOL_GROUP_DIM, (gi + 1) * POOL_GROUP_DIM)
        tok = ext_ref[halo:halo + tt, cols]
        tot = tok
        for i in range(1, w):
            tot = tot + ext_ref[pl.ds(halo - i, tt), cols]
        cnt = jnp.minimum(pos + 1, w).astype(F32)
        o_ref[:, cols] = _pool_group_out(tot / cnt, tok, pw_ref, scale_ref, gi)
    ext_ref[0:halo, :] = ext_ref[tt:tt + halo, :]


def _pool_prompt(proj, pool_w, pool_scale, *, batch, seq, tt):
    nt = seq // tt
    body = functools.partial(_pool_prompt_body, tt=tt, start_pos=0)
    return pl.pallas_call(
        body,
        grid=(batch, nt),
        in_specs=[
            pl.BlockSpec((tt, POOL_DIM), lambda b, t: (b * nt + t, C_POOL // POOL_DIM)),
            pl.BlockSpec((len(POOL_WINDOWS), POOL_GROUP_DIM, POOL_GROUP_DIM), lambda b, t: (0, 0, 0)),
            pl.BlockSpec((1, POOL_DIM), lambda b, t: (0, 0)),
        ],
        out_specs=pl.BlockSpec((tt, POOL_DIM), lambda b, t: (b * nt + t, 0)),
        out_shape=jax.ShapeDtypeStruct((batch * seq, POOL_DIM), BF16),
        scratch_shapes=[pltpu.VMEM((POOL_BUF + 1 + tt, POOL_DIM), F32)],
        compiler_params=_cparams("parallel", "arbitrary"),
        name="pool_prompt",
    )(proj, pool_w, pool_scale)


def _pool_sample_body(u_ref, st_ref, pw_ref, scale_ref, o_ref, *, start_pos):
    for gi, w in enumerate(POOL_WINDOWS):
        cols = slice(gi * POOL_GROUP_DIM, (gi + 1) * POOL_GROUP_DIM)
        tok = u_ref[:, cols]
        tot = tok
        for i in range(1, w):
            tot = tot + st_ref[POOL_BUF - i, :, cols]
        cnt = float(min(start_pos + 1, w))
        o_ref[:, cols] = _pool_group_out(tot / cnt, tok, pw_ref, scale_ref, gi)


def _pool_sample(proj, state_t, pool_w, pool_scale):
    nb = proj.shape[0]
    body = functools.partial(_pool_sample_body, start_pos=PAST_LEN)
    return pl.pallas_call(
        body,
        grid=(1,),
        in_specs=[
            pl.BlockSpec((nb, POOL_DIM), lambda i: (0, C_POOL // POOL_DIM)),
            pl.BlockSpec((POOL_BUF, nb, POOL_DIM), lambda i: (0, 0, 0)),
            pl.BlockSpec((len(POOL_WINDOWS), POOL_GROUP_DIM, POOL_GROUP_DIM), lambda i: (0, 0, 0)),
            pl.BlockSpec((1, POOL_DIM), lambda i: (0, 0)),
        ],
        out_specs=pl.BlockSpec((nb, POOL_DIM), lambda i: (0, 0)),
        out_shape=jax.ShapeDtypeStruct((nb, POOL_DIM), BF16),
        compiler_params=_cparams("arbitrary"),
        name="pool_sample",
    )(proj, state_t, pool_w, pool_scale)


def _split3(x):
    hi = x.astype(BF16)
    r1 = x - hi.astype(F32)
    mid = r1.astype(BF16)
    lo = (r1 - mid.astype(F32)).astype(BF16)
    return hi, mid, lo


def _split2(x):
    hi = x.astype(BF16)
    return hi, (x - hi.astype(F32)).astype(BF16)


def _softplus(x):
    return jnp.maximum(x, 0.0) + jnp.log1p(jnp.exp(-jnp.abs(x)))


def _unit_lower_inverse_minus_identity(mats):
    qs = [-a for a in mats]
    rps = list(qs)
    levels = CHUNK.bit_length() - 1
    qbs = [q.astype(BF16) for q in qs]
    qs = [_dot(qb, qb) for qb in qbs]
    for j in range(1, levels):
        qbs = [q.astype(BF16) for q in qs]
        if j < levels - 1:
            ps = [_dot(qb, jnp.concatenate([q, rp], axis=1).astype(BF16)) for qb, q, rp in zip(qbs, qs, rps)]
            rps = [rp + q + p[:, CHUNK:] for rp, q, p in zip(rps, qs, ps)]
            qs = [p[:, :CHUNK] for p in ps]
        else:
            rps = [rp + q + _dot(qb, rp.astype(BF16)) for qb, q, rp in zip(qbs, qs, rps)]
    return rps


def _gdn_prompt_body(q_ref, k_ref, v_ref, z_ref, ab_ref, cwq_ref, cwk_ref, cwv_ref, alog_ref, dtb_ref, gn_ref,
                     o_ref, ssm_ref, ext_ref, s_ref, gct_ref, *, hg, tt):
    hgi = pl.program_id(1)
    t = pl.program_id(2)
    w = hg * HEAD_DIM
    halo = SUBLANES

    @pl.when(t == 0)
    def _():
        ext_ref[0:halo, :] = jnp.zeros((halo, 3 * w), F32)
        s_ref[...] = jnp.zeros_like(s_ref)

    ext_ref[halo:halo + tt, 0:w] = q_ref[...]
    ext_ref[halo:halo + tt, w:2 * w] = k_ref[...]
    ext_ref[halo:halo + tt, 2 * w:3 * w] = v_ref[...]

    def conv(part, cw_ref):
        cols = slice(part * w, (part + 1) * w)
        acc = ext_ref[halo:halo + tt, cols] * cw_ref[CONV_BUF:CONV_WIDTH, :]
        for i in range(CONV_BUF):
            acc = acc + ext_ref[pl.ds(halo - CONV_BUF + i, tt), cols] * cw_ref[i:i + 1, :]
        return _silu(acc)

    qc = conv(0, cwq_ref)
    kc = conv(1, cwk_ref)
    vc = conv(2, cwv_ref)
    ext_ref[0:halo, :] = ext_ref[tt:tt + halo, :]

    ab = ab_ref[...]
    g_all = -jnp.exp(alog_ref[...]) * _softplus(ab + dtb_ref[...])
    beta_all = jax.nn.sigmoid(ab)
    ri = lax.broadcasted_iota(jnp.int32, (tt, tt), 0)
    ci = lax.broadcasted_iota(jnp.int32, (tt, tt), 1)
    shift = CHUNK.bit_length() - 1
    tri = jnp.where(((ri >> shift) == (ci >> shift)) & (ci <= ri), 1.0, 0.0).astype(BF16)
    g_hi, g_mid, g_lo = _split3(g_all)
    gc_all = _dot(tri, g_hi) + _dot(tri, g_mid) + _dot(tri, g_lo)
    gct_ref[...] = gc_all.T

    lane = lax.broadcasted_iota(jnp.int32, (tt, LANES), 1)
    r128 = lax.broadcasted_iota(jnp.int32, (CHUNK, CHUNK), 0)
    c128 = lax.broadcasted_iota(jnp.int32, (CHUNK, CHUNK), 1)
    sub8 = lax.broadcasted_iota(jnp.int32, (SUBLANES, CHUNK), 0)
    n_chunks = tt // CHUNK
    chains = [(hh, c) for hh in range(hg) for c in range(n_chunks)]
    gcol_heads, bcol_heads = [], []
    for hh in range(hg):
        head = hgi * hg + hh
        gcol_heads.append(jnp.sum(jnp.where(lane == head, gc_all, 0.0), axis=-1, keepdims=True))
        bcol_heads.append(jnp.sum(jnp.where(lane == head + HEADS, beta_all, 0.0), axis=-1, keepdims=True))

    qn_l, kn_l, kb_l, rhs_l, eg_l, gcol_l, dec_l = [], [], [], [], [], [], []
    for hh, c in chains:
        head = hgi * hg + hh
        rows = slice(c * CHUNK, (c + 1) * CHUNK)
        lanes = slice(hh * HEAD_DIM, (hh + 1) * HEAD_DIM)
        qh, kh, vh = qc[rows, lanes], kc[rows, lanes], vc[rows, lanes]
        qn = qh * lax.rsqrt(jnp.sum(qh * qh, axis=-1, keepdims=True) + EPS) * (HEAD_DIM ** -0.5)
        kn = kh * lax.rsqrt(jnp.sum(kh * kh, axis=-1, keepdims=True) + EPS)
        gcol = gcol_heads[hh][rows]
        bcol = bcol_heads[hh][rows]
        gblk = gct_ref[pl.ds(pl.multiple_of((head >> 3) * SUBLANES, SUBLANES), SUBLANES), rows]
        grow = jnp.sum(jnp.where(sub8 == (head & (SUBLANES - 1)), gblk, 0.0), axis=0, keepdims=True)
        eg = jnp.exp(gcol)
        kb = kn * bcol
        qn_l.append(qn)
        kn_l.append(kn)
        kb_l.append(kb)
        eg_l.append(eg)
        gcol_l.append(gcol)
        dec_l.append(jnp.exp(jnp.minimum(gcol - grow, 0.0)))
        rhs_l.append(jnp.concatenate([vh * bcol, kb * eg], axis=1))

    a_l, qk_l = [], []
    for qn, kn, kb, decay in zip(qn_l, kn_l, kb_l, dec_l):
        knb = kn.astype(BF16)
        a_l.append(_dot_nt(kb.astype(BF16), knb) * jnp.where(c128 < r128, decay, 0.0))
        qk_l.append(_dot_nt(qn.astype(BF16), knb) * jnp.where(c128 <= r128, decay, 0.0))
    tinv_l = [x.astype(BF16) for x in _unit_lower_inverse_minus_identity(a_l)]
    sol_l = [rhs + _dot(tinv, rhs.astype(BF16)) for tinv, rhs in zip(tinv_l, rhs_l)]
    res_l = []
    for a, sol, rhs in zip(a_l, sol_l, rhs_l):
        a_hi, a_lo = _split2(a)
        s_hi, s_lo = _split2(sol)
        a_sol = _dot(jnp.concatenate([a_hi, a_lo, a_hi], axis=1), jnp.concatenate([s_hi, s_hi, s_lo], axis=0))
        res_l.append((rhs - sol) - a_sol)
    sol_l = [(sol + res + _dot(tinv, res.astype(BF16))).astype(BF16) for sol, res, tinv in zip(sol_l, res_l, tinv_l)]
    lhs_l, o_add_l, s_add_l, glast_l = [], [], [], []
    for qn, kn, eg, gcol, qk, sol_b in zip(qn_l, kn_l, eg_l, gcol_l, qk_l, sol_l):
        glast = gcol[CHUNK - 1:CHUNK, :]
        ktail = kn * jnp.exp(glast - gcol)
        kt_sol = _dot(ktail.T.astype(BF16), sol_b)
        qk_sol = _dot(qk.astype(BF16), sol_b)
        lhs_l.append(jnp.concatenate([qn * eg - qk_sol[:, HEAD_DIM:], kt_sol[:, HEAD_DIM:]], axis=0).astype(BF16))
        o_add_l.append(qk_sol[:, :HEAD_DIM])
        s_add_l.append(kt_sol[:, :HEAD_DIM])
        glast_l.append(jnp.exp(glast))
    states = [s_ref[hh] for hh in range(hg)]
    for c in range(n_chunks):
        for hh in range(hg):
            i = hh * n_chunks + c
            rows = slice(c * CHUNK, (c + 1) * CHUNK)
            lanes = slice(hh * HEAD_DIM, (hh + 1) * HEAD_DIM)
            on_s = _dot(lhs_l[i], states[hh].astype(BF16))
            o = on_s[:CHUNK] + o_add_l[i]
            states[hh] = states[hh] * glast_l[i] - on_s[CHUNK:] + s_add_l[i]
            o_ref[rows, lanes] = (_rms(o, gn_ref[...]) * _silu(z_ref[rows, lanes])).astype(BF16)
    for hh in range(hg):
        s_ref[hh] = states[hh]

    @pl.when(t == pl.num_programs(2) - 1)
    def _():
        ssm_ref[0, 0] = s_ref[...]


def _gdn_prompt(proj, ab, conv_w, alog_pad, dtb_pad, gdn_norm, *, batch, seq, hg, tt):
    nt = seq // tt
    w = hg * HEAD_DIM
    body = functools.partial(_gdn_prompt_body, hg=hg, tt=tt)

    def col_spec(col0):
        return pl.BlockSpec((tt, w), lambda b, h, t: (b * nt + t, col0 // w + h))

    def cw_spec(part):
        return pl.BlockSpec((CONV_WIDTH, w), lambda b, h, t: (0, part * GDN_DIM // w + h))

    small = pl.BlockSpec((1, LANES), lambda b, h, t: (0, 0))
    return pl.pallas_call(
        body,
        grid=(batch, HEADS // hg, nt),
        in_specs=[col_spec(C_Q), col_spec(C_K), col_spec(C_V), col_spec(C_Z),
                  pl.BlockSpec((tt, LANES), lambda b, h, t: (b * nt + t, 0)),
                  cw_spec(0), cw_spec(1), cw_spec(2), small, small, small],
        out_specs=[
            pl.BlockSpec((tt, w), lambda b, h, t: (b * nt + t, h)),
            pl.BlockSpec((1, 1, hg, HEAD_DIM, HEAD_DIM), lambda b, h, t: (0, b, h, 0, 0)),
        ],
        out_shape=[
            jax.ShapeDtypeStruct((batch * seq, GDN_DIM), BF16),
            jax.ShapeDtypeStruct((1, batch, HEADS, HEAD_DIM, HEAD_DIM), F32),
        ],
        scratch_shapes=[
            pltpu.VMEM((SUBLANES + tt, 3 * w), F32),
            pltpu.VMEM((hg, HEAD_DIM, HEAD_DIM), F32),
            pltpu.VMEM((LANES, tt), F32),
        ],
        compiler_params=_cparams("parallel", "parallel", "arbitrary"),
        name="gdn_prompt",
    )(proj, proj, proj, proj, ab, conv_w, conv_w, conv_w, alog_pad, dtb_pad, gdn_norm)


def _gdn_sample_prep_body(q_ref, k_ref, v_ref, ab_ref, st_ref, cw_ref, alog_ref, dtb_ref,
                          qn_ref, kn_ref, vc_ref, eg_ref, beta_ref, qk_ref):
    def conv(part, u_ref):
        cols = slice(part * GDN_DIM, (part + 1) * GDN_DIM)
        acc = u_ref[...] * cw_ref[CONV_BUF:CONV_WIDTH, cols]
        for i in range(CONV_BUF):
            acc = acc + st_ref[i, :, cols] * cw_ref[i:i + 1, cols]
        return _silu(acc)

    qc = conv(0, q_ref)
    kc = conv(1, k_ref)
    vc_ref[...] = conv(2, v_ref)
    ab = ab_ref[...]
    g_all = -jnp.exp(alog_ref[...]) * _softplus(ab + dtb_ref[...])
    eg_all = jnp.exp(g_all)
    beta_all = jax.nn.sigmoid(ab)
    for h in range(HEADS):
        lanes = slice(h * HEAD_DIM, (h + 1) * HEAD_DIM)
        qh, kh = qc[:, lanes], kc[:, lanes]
        qn = qh * lax.rsqrt(jnp.sum(qh * qh, axis=-1, keepdims=True) + EPS) * (HEAD_DIM ** -0.5)
        kn = kh * lax.rsqrt(jnp.sum(kh * kh, axis=-1, keepdims=True) + EPS)
        qn_ref[:, lanes] = qn
        kn_ref[:, lanes] = kn
        shape = qn.shape
        qk_ref[:, lanes] = jnp.broadcast_to(jnp.sum(qn * kn, axis=-1, keepdims=True), shape)
        eg_ref[:, lanes] = jnp.broadcast_to(eg_all[:, h:h + 1], shape)
        beta_ref[:, lanes] = jnp.broadcast_to(beta_all[:, HEADS + h:HEADS + h + 1], shape)


def _gdn_sample_prep(proj, ab, conv_state_t, conv_w, alog_pad, dtb_pad):
    nb = proj.shape[0]

    def col_spec(col0):
        return pl.BlockSpec((nb, GDN_DIM), lambda i: (0, col0 // GDN_DIM))

    small = pl.BlockSpec((1, LANES), lambda i: (0, 0))
    full = pl.BlockSpec((nb, GDN_DIM), lambda i: (0, 0))
    return pl.pallas_call(
        _gdn_sample_prep_body,
        grid=(1,),
        in_specs=[col_spec(C_Q), col_spec(C_K), col_spec(C_V),
                  pl.BlockSpec((nb, LANES), lambda i: (0, 0)),
                  pl.BlockSpec((CONV_BUF, nb, QKV_DIM), lambda i: (0, 0, 0)),
                  pl.BlockSpec((CONV_WIDTH, QKV_DIM), lambda i: (0, 0)),
                  small, small],
        out_specs=[full] * 6,
        out_shape=[jax.ShapeDtypeStruct((nb, GDN_DIM), F32)] * 6,
        compiler_params=_cparams("arbitrary"),
        name="gdn_sample_prep",
    )(proj, proj, proj, ab, conv_state_t, conv_w, alog_pad, dtb_pad)


def _gdn_sample_step_body(qn_ref, kn_ref, vc_ref, eg_ref, beta_ref, qk_ref, z_ref, gn_ref, s_ref,
                          o_ref, snew_ref, *, bb):
    pad = jnp.zeros((LANES - HEADS, HEAD_DIM), F32)
    for b in range(bb):
        k16, q16 = kn_ref[b], qn_ref[b]
        cols = jnp.concatenate([k16, pad], axis=0).T
        kq = jnp.concatenate([k16, q16], axis=0).astype(BF16)
        outs = []
        for h in range(HEADS):
            kcol = cols[:, h:h + 1]
            s = s_ref[0, b, h]
            kq_s = _dot(kq, s.astype(BF16))
            k_s = kq_s[h:h + 1]
            q_s = kq_s[HEADS + h:HEADS + h + 1]
            eg = eg_ref[b, h:h + 1, :]
            v_new = beta_ref[b, h:h + 1, :] * (vc_ref[b, h:h + 1, :] - eg * k_s)
            snew_ref[0, b, h] = eg * s + kcol * v_new
            outs.append(eg * q_s + qk_ref[b, h:h + 1, :] * v_new)
        o = jnp.concatenate(outs, axis=0)
        o_ref[b] = (_rms(o, gn_ref[...]) * _silu(z_ref[b])).astype(BF16)


def _gdn_sample_step(qn, kn, vc, eg, beta, qk, z, gdn_norm, state_ssm, *, bb):
    nb = qn.shape[0]
    body = functools.partial(_gdn_sample_step_body, bb=bb)
    vec = pl.BlockSpec((bb, HEADS, HEAD_DIM), lambda i: (i, 0, 0))
    st = pl.BlockSpec((1, bb, HEADS, HEAD_DIM, HEAD_DIM), lambda i: (0, i, 0, 0, 0))
    return pl.pallas_call(
        body,
        grid=(nb // bb,),
        in_specs=[vec] * 7 + [pl.BlockSpec((1, LANES), lambda i: (0, 0)), st],
        out_specs=[vec, st],
        out_shape=[jax.ShapeDtypeStruct((nb, HEADS, HEAD_DIM), BF16),
                   jax.ShapeDtypeStruct(state_ssm.shape, F32)],
        compiler_params=_cparams("parallel"),
        name="gdn_sample_step",
    )(qn, kn, vc, eg, beta, qk, z, gdn_norm, state_ssm)


def _merge_body(pool_ref, gdn_ref, wpu_ref, wgu_ref, gp_ref, gg_ref, o_ref):
    o_ref[...] = (gp_ref[...] * _dot(pool_ref[...], wpu_ref[...])
                  + gg_ref[...] * _dot(gdn_ref[...], wgu_ref[...])).astype(BF16)


def _merge(proj, pool_out, gdn_out, w_pool_up, w_gdn_up, *, tm, tn=512):
    m = proj.shape[0]
    nj = D_MODEL // tn
    return pl.pallas_call(
        _merge_body,
        grid=(m // tm, nj),
        in_specs=[
            pl.BlockSpec((tm, POOL_DIM), lambda i, j: (i, 0)),
            pl.BlockSpec((tm, GDN_DIM), lambda i, j: (i, 0)),
            pl.BlockSpec((POOL_DIM, tn), lambda i, j: (0, j)),
            pl.BlockSpec((GDN_DIM, tn), lambda i, j: (0, j)),
            pl.BlockSpec((tm, tn), lambda i, j: (i, C_GP // tn + j)),
            pl.BlockSpec((tm, tn), lambda i, j: (i, C_GG // tn + j)),
        ],
        out_specs=pl.BlockSpec((tm, tn), lambda i, j: (i, j)),
        out_shape=jax.ShapeDtypeStruct((m, D_MODEL), BF16),
        compiler_params=_cparams("parallel", "parallel"),
        name="merge",
    )(pool_out, gdn_out, w_pool_up, w_gdn_up, proj, proj)


def _outproj_body(a_ref, w_ref, x_ref, o_ref):
    o_ref[...] = x_ref[...] + _dot(a_ref[...], w_ref[...])


def _outproj(merged, x, w_o, *, tm, tn=512):
    m = x.shape[0]
    return pl.pallas_call(
        _outproj_body,
        grid=(m // tm, D_MODEL // tn),
        in_specs=[
            pl.BlockSpec((tm, D_MODEL), lambda i, j: (i, 0)),
            pl.BlockSpec((D_MODEL, tn), lambda i, j: (0, j)),
            pl.BlockSpec((tm, tn), lambda i, j: (i, j)),
        ],
        out_specs=pl.BlockSpec((tm, tn), lambda i, j: (i, j)),
        out_shape=jax.ShapeDtypeStruct((m, D_MODEL), F32),
        compiler_params=_cparams("parallel", "parallel"),
        name="outproj",
    )(merged, w_o, x)


def _ffn_body(x_ref, gain_ref, wg_ref, wu_ref, wd_ref, o_ref, h_ref):
    f = pl.program_id(1)

    @pl.when(f == 0)
    def _():
        x = x_ref[...]
        h_ref[...] = _rms(x, gain_ref[...]).astype(BF16)
        o_ref[...] = x

    h = h_ref[...]
    act = (_silu(_dot(h, wg_ref[...])) * _dot(h, wu_ref[...])).astype(BF16)
    o_ref[...] += _dot(act, wd_ref[...])


def _ffn(x, gain, w_gate_up, w_down, *, tm, tf=512):
    m = x.shape[0]
    nf = D_FF // tf
    return pl.pallas_call(
        _ffn_body,
        grid=(m // tm, nf),
        in_specs=[
            pl.BlockSpec((tm, D_MODEL), lambda i, f: (i, 0)),
            pl.BlockSpec((1, D_MODEL), lambda i, f: (0, 0)),
            pl.BlockSpec((D_MODEL, tf), lambda i, f: (0, f)),
            pl.BlockSpec((D_MODEL, tf), lambda i, f: (0, nf + f)),
            pl.BlockSpec((tf, D_MODEL), lambda i, f: (f, 0)),
        ],
        out_specs=pl.BlockSpec((tm, D_MODEL), lambda i, f: (i, 0)),
        out_shape=jax.ShapeDtypeStruct((m, D_MODEL), F32),
        scratch_shapes=[pltpu.VMEM((tm, D_MODEL), BF16)],
        compiler_params=_cparams("parallel", "arbitrary"),
        name="ffn",
    )(x, gain, w_gate_up, w_gate_up, w_down)


def _final_body(x_ref, p_ref, wple_ref, wpg_ref, gain_ref, o_ref):
    x = x_ref[...]
    gate = jax.nn.sigmoid(_dot(x.astype(BF16), wpg_ref[...]))
    emb = _dot(p_ref[...].astype(BF16), wple_ref[...])
    o_ref[...] = _rms(x + emb * gate, gain_ref[...])


def _final(x, p, w_ple, w_ple_gate, gain, *, tm):
    m = x.shape[0]
    return pl.pallas_call(
        _final_body,
        grid=(m // tm,),
        in_specs=[
            pl.BlockSpec((tm, D_MODEL), lambda i: (i, 0)),
            pl.BlockSpec((tm, PLE_DIM), lambda i: (i, 0)),
            pl.BlockSpec((PLE_DIM, D_MODEL), lambda i: (0, 0)),
            pl.BlockSpec((D_MODEL, D_MODEL), lambda i: (0, 0)),
            pl.BlockSpec((1, D_MODEL), lambda i: (0, 0)),
        ],
        out_specs=pl.BlockSpec((tm, D_MODEL), lambda i: (i, 0)),
        out_shape=jax.ShapeDtypeStruct((m, D_MODEL), F32),
        compiler_params=_cparams("parallel"),
        name="final",
    )(x, p, w_ple, w_ple_gate, gain)


def _tile_rows(m, want):
    return want if m % want == 0 else m


def _dense_tail(x, p, proj, pool_out, gdn_out, wts):
    m = x.shape[0]
    tm = _tile_rows(m, 1024)
    merged = _merge(proj, pool_out, gdn_out, wts["w_pool_up"], wts["w_gdn_up"], tm=tm)
    x = _outproj(merged, x, wts["w_o"], tm=tm)
    x = _ffn(x, wts["norm_ffn"], wts["w_gate_up"], wts["w_down"], tm=tm)
    return _final(x, p, wts["w_ple"], wts["w_ple_gate"], wts["norm_final"], tm=_tile_rows(m, 256))


def kernel(x_prompt, x_sample, p_prompt, p_sample, state_pool, state_conv, state_ssm, norm_mix, w_in, pool_w,
           pool_scale, conv_w, a_log, dt_bias, gdn_norm, w_pool_up, w_gdn_up, w_o, norm_ffn, w_gate_up, w_down,
           w_ple, w_ple_gate, norm_final):
    batch, seq, _ = x_prompt.shape
    nb = x_sample.shape[0]
    w_in_t = jnp.swapaxes(w_in[0], 0, 1)
    wts = dict(
        w_pool_up=w_pool_up[0].astype(BF16), w_gdn_up=w_gdn_up[0].astype(BF16), w_o=w_o[0].astype(BF16),
        norm_ffn=norm_ffn, w_gate_up=w_gate_up[0].astype(BF16), w_down=w_down[0].astype(BF16),
        w_ple=w_ple[0].astype(BF16), w_ple_gate=w_ple_gate[0].astype(BF16), norm_final=norm_final[None, :],
    )
    pool_w_b = pool_w[0].astype(BF16)
    lane_pad = ((0, 0), (0, LANES - HEADS))
    alog_pad = jnp.pad(a_log, lane_pad)
    dtb_pad = jnp.pad(dt_bias, lane_pad)
    gn = gdn_norm

    xp = x_prompt.reshape(batch * seq, D_MODEL)
    h_p, ab_p = _norm_in(xp, norm_mix, w_in_t, tm=_tile_rows(batch * seq, 512))
    proj_p = _inproj(h_p, w_in_t, tm=_tile_rows(batch * seq, 1024), tn=1024)
    pool_out_p = _pool_prompt(proj_p, pool_w_b, pool_scale, batch=batch, seq=seq, tt=_tile_rows(seq, 256))
    gdn_out_p, ssm_p = _gdn_prompt(proj_p, ab_p, conv_w[0], alog_pad, dtb_pad, gn, batch=batch, seq=seq,
                                   hg=4, tt=_tile_rows(seq, 512))
    y_p = _dense_tail(xp, p_prompt[0].reshape(batch * seq, PLE_DIM), proj_p, pool_out_p, gdn_out_p, wts)
    proj_p3 = proj_p.reshape(batch, seq, C_END)
    pool_p = proj_p3[:, seq - POOL_BUF:, C_POOL:C_END][None]
    conv_p = proj_p3[:, seq - CONV_BUF:, C_Q:C_Z][None]

    xs = x_sample.reshape(nb, D_MODEL)
    h_s, ab_s = _norm_in(xs, norm_mix, w_in_t, tm=nb)
    proj_s = _inproj(h_s, w_in_t, tm=nb, tn=1024)
    pool_out_s = _pool_sample(proj_s, jnp.swapaxes(state_pool[0], 0, 1), pool_w_b, pool_scale)
    prep = _gdn_sample_prep(proj_s, ab_s, jnp.swapaxes(state_conv[0], 0, 1), conv_w[0], alog_pad, dtb_pad)
    to_heads = lambda a: a.reshape(nb, HEADS, HEAD_DIM)
    gdn_out_s, ssm_s = _gdn_sample_step(*[to_heads(a) for a in prep], to_heads(proj_s[:, C_Z:C_POOL]), gn,
                                        state_ssm, bb=2)
    y_s = _dense_tail(xs, p_sample[0].reshape(nb, PLE_DIM), proj_s, pool_out_s, gdn_out_s.reshape(nb, GDN_DIM), wts)
    pool_s = jnp.concatenate([state_pool[0][:, 1:], proj_s[:, None, C_POOL:C_END]], axis=1)[None]
    conv_s = jnp.concatenate([state_conv[0][:, 1:], proj_s[:, None, C_Q:C_Z]], axis=1)[None]

    return (y_p.reshape(batch, seq, D_MODEL), y_s.reshape(nb, 1, D_MODEL), pool_p, conv_p, ssm_p,
            pool_s, conv_s, ssm_s)
```

```python
import functools

import jax
import jax.numpy as jnp
from jax import lax
from jax.experimental import pallas as pl
from jax.experimental.pallas import tpu as pltpu

F32 = jnp.float32
BF16 = jnp.bfloat16

D_MODEL = 2048
PAST_LEN = 16384
POOL_WINDOWS = (2, 4, 8, 16)
POOL_GROUP_DIM = D_MODEL // 8
POOL_DIM = len(POOL_WINDOWS) * POOL_GROUP_DIM
POOL_BUF = max(POOL_WINDOWS) - 1
HEAD_DIM = 128
HEADS = D_MODEL // HEAD_DIM
GDN_DIM = HEADS * HEAD_DIM
QKV_DIM = 3 * GDN_DIM
CONV_WIDTH = 4
CONV_BUF = CONV_WIDTH - 1
D_FF = -(-8 * D_MODEL // (3 * 256)) * 256
PLE_DIM = 256
EPS = 1e-6

IN_QKV = POOL_DIM
IN_Z = IN_QKV + QKV_DIM
IN_AB = IN_Z + GDN_DIM
IN_GATES = IN_AB + 2 * HEADS
C_GP = 0
C_GG = C_GP + D_MODEL
C_Q = C_GG + D_MODEL
C_K = C_Q + GDN_DIM
C_V = C_K + GDN_DIM
C_Z = C_V + GDN_DIM
C_POOL = C_Z + GDN_DIM
C_END = C_POOL + POOL_DIM

LANES = 128
SUBLANES = 8
VMEM_LIMIT_BYTES = 56 * 1024 * 1024
CHUNK = 128


def _cparams(*sem):
    return pltpu.CompilerParams(dimension_semantics=sem, vmem_limit_bytes=VMEM_LIMIT_BYTES)


def _rms(x, gain):
    return x * lax.rsqrt(jnp.mean(x * x, axis=-1, keepdims=True) + EPS) * gain


def _dot(a, b):
    return jnp.dot(a, b, preferred_element_type=F32)


def _dot_nt(a, b):
    return lax.dot_general(a, b, (((1,), (1,)), ((), ())), preferred_element_type=F32)


def _silu(x):
    return x * jax.nn.sigmoid(x)


def _norm_in_body(xp_ref, xs_ref, gain_ref, wab_ref, h_ref, ab_ref, *, n_prompt_tiles):
    def emit(x_ref):
        h = _rms(x_ref[...], gain_ref[...]).astype(BF16)
        h_ref[...] = h
        ab_ref[...] = _dot_nt(h, wab_ref[...].astype(BF16))

    @pl.when(pl.program_id(0) < n_prompt_tiles)
    def _():
        emit(xp_ref)

    @pl.when(pl.program_id(0) >= n_prompt_tiles)
    def _():
        emit(xs_ref)


def _norm_in(x_prompt, x_sample, gain, w_in_t, *, tm):
    mp, ms = x_prompt.shape[0], x_sample.shape[0]
    assert mp % tm == 0 and ms % tm == 0
    n_p = mp // tm
    body = functools.partial(_norm_in_body, n_prompt_tiles=n_p)
    return pl.pallas_call(
        body,
        grid=((mp + ms) // tm,),
        in_specs=[
            pl.BlockSpec((tm, D_MODEL), lambda i: (jnp.minimum(i, n_p - 1), 0)),
            pl.BlockSpec((tm, D_MODEL), lambda i: (jnp.maximum(i - n_p, 0), 0)),
            pl.BlockSpec((1, D_MODEL), lambda i: (0, 0)),
            pl.BlockSpec((LANES, D_MODEL), lambda i: (IN_AB // LANES, 0)),
        ],
        out_specs=[
            pl.BlockSpec((tm, D_MODEL), lambda i: (i, 0)),
            pl.BlockSpec((tm, LANES), lambda i: (i, 0)),
        ],
        out_shape=[jax.ShapeDtypeStruct((mp + ms, D_MODEL), BF16), jax.ShapeDtypeStruct((mp + ms, LANES), F32)],
        compiler_params=_cparams("parallel"),
        name="norm_in",
    )(x_prompt, x_sample, gain, w_in_t)


def _inproj_row_start(j, tn):
    n_gate = (2 * D_MODEL) // tn
    n_mid = (QKV_DIM + GDN_DIM) // tn
    row = jnp.where(j < n_gate, IN_GATES + j * tn,
                    jnp.where(j < n_gate + n_mid, IN_QKV + (j - n_gate) * tn, (j - n_gate - n_mid) * tn))
    return pl.multiple_of(row, SUBLANES)


def _inproj_body(h_ref, w_ref, o_ref, wb_ref, *, n_gate_tiles):
    j = pl.program_id(0)

    @pl.when(pl.program_id(1) == 0)
    def _():
        wb_ref[...] = w_ref[...].astype(BF16)

    @pl.when(j < n_gate_tiles)
    def _():
        o_ref[...] = jax.nn.sigmoid(_dot_nt(h_ref[...], wb_ref[...]))

    @pl.when(j >= n_gate_tiles)
    def _():
        o_ref[...] = _dot_nt(h_ref[...], wb_ref[...])


def _inproj(h, w_in_t, *, tm, tn):
    m = h.shape[0]
    body = functools.partial(_inproj_body, n_gate_tiles=(2 * D_MODEL) // tn)
    return pl.pallas_call(
        body,
        grid=(C_END // tn, m // tm),
        in_specs=[
            pl.BlockSpec((tm, D_MODEL), lambda j, i: (i, 0)),
            pl.BlockSpec((pl.Element(tn), pl.Element(D_MODEL)), lambda j, i: (_inproj_row_start(j, tn), 0)),
        ],
        out_specs=pl.BlockSpec((tm, tn), lambda j, i: (i, j)),
        out_shape=jax.ShapeDtypeStruct((m, C_END), F32),
        scratch_shapes=[pltpu.VMEM((tn, D_MODEL), BF16)],
        compiler_params=_cparams("arbitrary", "arbitrary"),
        name="inproj",
    )(h, w_in_t)


def _pool_group_out(mean, tok, pw_ref, scale_ref, gi):
    cols = slice(gi * POOL_GROUP_DIM, (gi + 1) * POOL_GROUP_DIM)
    y = _dot((mean - tok).astype(BF16), pw_ref[gi])
    return (y * scale_ref[:, cols]).astype(BF16)


def _pool_prompt_body(u_ref, pw_ref, scale_ref, o_ref, ext_ref, *, tt, start_pos):
    t = pl.program_id(1)
    halo = POOL_BUF + 1

    @pl.when(t == 0)
    def _():
        ext_ref[0:halo, :] = jnp.zeros((halo, POOL_DIM), F32)

    ext_ref[halo:halo + tt, :] = u_ref[...]
    pos = start_pos + t * tt + lax.broadcasted_iota(jnp.int32, (tt, 1), 0)
    for gi, w in enumerate(POOL_WINDOWS):
        cols = slice(gi * POOL_GROUP_DIM, (gi + 1) * POOL_GROUP_DIM)
        tok = ext_ref[halo:halo + tt, cols]
        tot = tok
        for i in range(1, w):
            tot = tot + ext_ref[pl.ds(halo - i, tt), cols]
        cnt = jnp.minimum(pos + 1, w).astype(F32)
        o_ref[:, cols] = _pool_group_out(tot / cnt, tok, pw_ref, scale_ref, gi)
    ext_ref[0:halo, :] = ext_ref[tt:tt + halo, :]


def _pool_prompt(proj, pool_w, pool_scale, *, batch, seq, tt):
    nt = seq // tt
    body = functools.partial(_pool_prompt_body, tt=tt, start_pos=0)
    return pl.pallas_call(
        body,
        grid=(batch, nt),
        in_specs=[
            pl.BlockSpec((tt, POOL_DIM), lambda b, t: (b * nt + t, C_POOL // POOL_DIM)),
            pl.BlockSpec((len(POOL_WINDOWS), POOL_GROUP_DIM, POOL_GROUP_DIM), lambda b, t: (0, 0, 0)),
            pl.BlockSpec((1, POOL_DIM), lambda b, t: (0, 0)),
        ],
        out_specs=pl.BlockSpec((tt, POOL_DIM), lambda b, t: (b * nt + t, 0)),
        out_shape=jax.ShapeDtypeStruct((batch * seq, POOL_DIM), BF16),
        scratch_shapes=[pltpu.VMEM((POOL_BUF + 1 + tt, POOL_DIM), F32)],
        compiler_params=_cparams("parallel", "arbitrary"),
        name="pool_prompt",
    )(proj, pool_w, pool_scale)


def _pool_sample_body(u_ref, st_ref, pw_ref, scale_ref, o_ref, *, start_pos):
    for gi, w in enumerate(POOL_WINDOWS):
        cols = slice(gi * POOL_GROUP_DIM, (gi + 1) * POOL_GROUP_DIM)
        tok = u_ref[:, cols]
        tot = tok
        for i in range(1, w):
            tot = tot + st_ref[POOL_BUF - i, :, cols]
        cnt = float(min(start_pos + 1, w))
        o_ref[:, cols] = _pool_group_out(tot / cnt, tok, pw_ref, scale_ref, gi)


def _pool_sample(proj, state_t, pool_w, pool_scale):
    nb = proj.shape[0]
    body = functools.partial(_pool_sample_body, start_pos=PAST_LEN)
    return pl.pallas_call(
        body,
        grid=(1,),
        in_specs=[
            pl.BlockSpec((nb, POOL_DIM), lambda i: (0, C_POOL // POOL_DIM)),
            pl.BlockSpec((POOL_BUF, nb, POOL_DIM), lambda i: (0, 0, 0)),
            pl.BlockSpec((len(POOL_WINDOWS), POOL_GROUP_DIM, POOL_GROUP_DIM), lambda i: (0, 0, 0)),
            pl.BlockSpec((1, POOL_DIM), lambda i: (0, 0)),
        ],
        out_specs=pl.BlockSpec((nb, POOL_DIM), lambda i: (0, 0)),
        out_shape=jax.ShapeDtypeStruct((nb, POOL_DIM), BF16),
        compiler_params=_cparams("arbitrary"),
        name="pool_sample",
    )(proj, state_t, pool_w, pool_scale)


def _split3(x):
    hi = x.astype(BF16)
    r1 = x - hi.astype(F32)
    mid = r1.astype(BF16)
    lo = (r1 - mid.astype(F32)).astype(BF16)
    return hi, mid, lo


def _split2(x):
    hi = x.astype(BF16)
    return hi, (x - hi.astype(F32)).astype(BF16)


def _softplus(x):
    return jnp.maximum(x, 0.0) + jnp.log1p(jnp.exp(-jnp.abs(x)))


def _unit_lower_inverse_minus_identity(mats):
    eye = jnp.where(lax.broadcasted_iota(jnp.int32, (CHUNK, CHUNK), 0)
                    == lax.broadcasted_iota(jnp.int32, (CHUNK, CHUNK), 1), 1.0, 0.0)
    levels = CHUNK.bit_length() - 1
    qbs = [(-a).astype(BF16) for a in mats]
    rs = [eye - a for a in mats]
    qs = [_dot(qb, qb) for qb in qbs]
    for j in range(1, levels):
        qbs = [q.astype(BF16) for q in qs]
        rbs = [r.astype(BF16) for r in rs]
        if j < levels - 1:
            ps = [_dot(qb, jnp.concatenate([qb, rb], axis=1)) for qb, rb in zip(qbs, rbs)]
            rs = [r + p[:, CHUNK:] for r, p in zip(rs, ps)]
            qs = [p[:, :CHUNK] for p in ps]
        else:
            rs = [r + _dot(qb, rb) for r, qb, rb in zip(rs, qbs, rbs)]
    return [r - eye for r in rs]


def _gdn_prompt_body(q_ref, k_ref, v_ref, z_ref, ab_ref, cwq_ref, cwk_ref, cwv_ref, alog_ref, dtb_ref, gn_ref,
                     o_ref, ssm_ref, ext_ref, s_ref, gct_ref, *, hg, tt):
    hgi = pl.program_id(1)
    t = pl.program_id(2)
    w = hg * HEAD_DIM
    halo = SUBLANES

    @pl.when(t == 0)
    def _():
        ext_ref[0:halo, :] = jnp.zeros((halo, 3 * w), F32)
        s_ref[...] = jnp.zeros_like(s_ref)

    ext_ref[halo:halo + tt, 0:w] = q_ref[...]
    ext_ref[halo:halo + tt, w:2 * w] = k_ref[...]
    ext_ref[halo:halo + tt, 2 * w:3 * w] = v_ref[...]

    def conv(part, cw_ref):
        cols = slice(part * w, (part + 1) * w)
        acc = ext_ref[halo:halo + tt, cols] * cw_ref[CONV_BUF:CONV_WIDTH, :]
        for i in range(CONV_BUF):
            acc = acc + ext_ref[pl.ds(halo - CONV_BUF + i, tt), cols] * cw_ref[i:i + 1, :]
        return _silu(acc)

    qc = conv(0, cwq_ref)
    kc = conv(1, cwk_ref)
    vc = conv(2, cwv_ref)
    ext_ref[0:halo, :] = ext_ref[tt:tt + halo, :]

    ab = ab_ref[...]
    g_all = -jnp.exp(alog_ref[...]) * _softplus(ab + dtb_ref[...])
    beta_all = jax.nn.sigmoid(ab)
    ri = lax.broadcasted_iota(jnp.int32, (tt, tt), 0)
    ci = lax.broadcasted_iota(jnp.int32, (tt, tt), 1)
    shift = CHUNK.bit_length() - 1
    tri = jnp.where(((ri >> shift) == (ci >> shift)) & (ci <= ri), 1.0, 0.0).astype(BF16)
    g_hi, g_mid, g_lo = _split3(g_all)
    gc_all = _dot(tri, g_hi) + _dot(tri, g_mid) + _dot(tri, g_lo)
    gct_ref[...] = gc_all.T

    lane = lax.broadcasted_iota(jnp.int32, (tt, LANES), 1)
    r128 = lax.broadcasted_iota(jnp.int32, (CHUNK, CHUNK), 0)
    c128 = lax.broadcasted_iota(jnp.int32, (CHUNK, CHUNK), 1)
    sub8 = lax.broadcasted_iota(jnp.int32, (SUBLANES, CHUNK), 0)
    n_chunks = tt // CHUNK
    chains = [(hh, c) for hh in range(hg) for c in range(n_chunks)]
    gcol_heads, bcol_heads = [], []
    for hh in range(hg):
        head = hgi * hg + hh
        gcol_heads.append(jnp.sum(jnp.where(lane == head, gc_all, 0.0), axis=-1, keepdims=True))
        bcol_heads.append(jnp.sum(jnp.where(lane == head + HEADS, beta_all, 0.0), axis=-1, keepdims=True))

    qn_l, kn_l, kb_l, rhs_l, eg_l, gcol_l, dec_l = [], [], [], [], [], [], []
    for hh, c in chains:
        head = hgi * hg + hh
        rows = slice(c * CHUNK, (c + 1) * CHUNK)
        lanes = slice(hh * HEAD_DIM, (hh + 1) * HEAD_DIM)
        qh, kh, vh = qc[rows, lanes], kc[rows, lanes], vc[rows, lanes]
        qn = qh * lax.rsqrt(jnp.sum(qh * qh, axis=-1, keepdims=True) + EPS) * (HEAD_DIM ** -0.5)
        kn = kh * lax.rsqrt(jnp.sum(kh * kh, axis=-1, keepdims=True) + EPS)
        gcol = gcol_heads[hh][rows]
        bcol = bcol_heads[hh][rows]
        gblk = gct_ref[pl.ds(pl.multiple_of((head >> 3) * SUBLANES, SUBLANES), SUBLANES), rows]
        grow = jnp.sum(jnp.where(sub8 == (head & (SUBLANES - 1)), gblk, 0.0), axis=0, keepdims=True)
        eg = jnp.exp(gcol)
        kb = kn * bcol
        qn_l.append(qn)
        kn_l.append(kn)
        kb_l.append(kb)
        eg_l.append(eg)
        gcol_l.append(gcol)
        dec_l.append(jnp.exp(jnp.minimum(gcol - grow, 0.0)))
        rhs_l.append(jnp.concatenate([vh * bcol, kb * eg], axis=1))

    a_l, qk_l = [], []
    for qn, kn, kb, decay in zip(qn_l, kn_l, kb_l, dec_l):
        knb = kn.astype(BF16)
        a_l.append(_dot_nt(kb.astype(BF16), knb) * jnp.where(c128 < r128, decay, 0.0))
        qk_l.append(_dot_nt(qn.astype(BF16), knb) * jnp.where(c128 <= r128, decay, 0.0))
    tinv_l = [x.astype(BF16) for x in _unit_lower_inverse_minus_identity(a_l)]
    sol_l = [rhs + _dot(tinv, rhs.astype(BF16)) for tinv, rhs in zip(tinv_l, rhs_l)]
    res_l = []
    for a, sol, rhs in zip(a_l, sol_l, rhs_l):
        a_hi, a_lo = _split2(a)
        s_hi, s_lo = _split2(sol)
        a_sol = _dot(jnp.concatenate([a_hi, a_lo, a_hi], axis=1), jnp.concatenate([s_hi, s_hi, s_lo], axis=0))
        res_l.append((rhs - sol) - a_sol)
    sol_l = [(sol + res + _dot(tinv, res.astype(BF16))).astype(BF16) for sol, res, tinv in zip(sol_l, res_l, tinv_l)]
    lhs_l, o_add_l, s_add_l, glast_l = [], [], [], []
    for qn, kn, eg, gcol, qk, sol_b in zip(qn_l, kn_l, eg_l, gcol_l, qk_l, sol_l):
        glast = gcol[CHUNK - 1:CHUNK, :]
        ktail = kn * jnp.exp(glast - gcol)
        kt_sol = _dot(ktail.T.astype(BF16), sol_b)
        qk_sol = _dot(qk.astype(BF16), sol_b)
        lhs_l.append(jnp.concatenate([qn * eg - qk_sol[:, HEAD_DIM:], kt_sol[:, HEAD_DIM:]], axis=0).astype(BF16))
        o_add_l.append(qk_sol[:, :HEAD_DIM])
        s_add_l.append(kt_sol[:, :HEAD_DIM])
        glast_l.append(jnp.exp(glast))
    states = [s_ref[hh] for hh in range(hg)]
    for c in range(n_chunks):
        for hh in range(hg):
            i = hh * n_chunks + c
            rows = slice(c * CHUNK, (c + 1) * CHUNK)
            lanes = slice(hh * HEAD_DIM, (hh + 1) * HEAD_DIM)
            on_s = _dot(lhs_l[i], states[hh].astype(BF16))
            o = on_s[:CHUNK] + o_add_l[i]
            states[hh] = states[hh] * glast_l[i] - on_s[CHUNK:] + s_add_l[i]
            o_ref[rows, lanes] = (_rms(o, gn_ref[...]) * _silu(z_ref[rows, lanes])).astype(BF16)
    for hh in range(hg):
        s_ref[hh] = states[hh]

    @pl.when(t == pl.num_programs(2) - 1)
    def _():
        ssm_ref[0, 0] = s_ref[...]


def _gdn_prompt(proj, ab, conv_w, alog_pad, dtb_pad, gdn_norm, *, batch, seq, hg, tt):
    nt = seq // tt
    w = hg * HEAD_DIM
    body = functools.partial(_gdn_prompt_body, hg=hg, tt=tt)

    def col_spec(col0):
        return pl.BlockSpec((tt, w), lambda b, h, t: (b * nt + t, col0 // w + h))

    def cw_spec(part):
        return pl.BlockSpec((CONV_WIDTH, w), lambda b, h, t: (0, part * GDN_DIM // w + h))

    small = pl.BlockSpec((1, LANES), lambda b, h, t: (0, 0))
    return pl.pallas_call(
        body,
        grid=(batch, HEADS // hg, nt),
        in_specs=[col_spec(C_Q), col_spec(C_K), col_spec(C_V), col_spec(C_Z),
                  pl.BlockSpec((tt, LANES), lambda b, h, t: (b * nt + t, 0)),
                  cw_spec(0), cw_spec(1), cw_spec(2), small, small, small],
        out_specs=[
            pl.BlockSpec((tt, w), lambda b, h, t: (b * nt + t, h)),
            pl.BlockSpec((1, 1, hg, HEAD_DIM, HEAD_DIM), lambda b, h, t: (0, b, h, 0, 0)),
        ],
        out_shape=[
            jax.ShapeDtypeStruct((batch * seq, GDN_DIM), BF16),
            jax.ShapeDtypeStruct((1, batch, HEADS, HEAD_DIM, HEAD_DIM), F32),
        ],
        scratch_shapes=[
            pltpu.VMEM((SUBLANES + tt, 3 * w), F32),
            pltpu.VMEM((hg, HEAD_DIM, HEAD_DIM), F32),
            pltpu.VMEM((LANES, tt), F32),
        ],
        compiler_params=_cparams("parallel", "parallel", "arbitrary"),
        name="gdn_prompt",
    )(proj, proj, proj, proj, ab, conv_w, conv_w, conv_w, alog_pad, dtb_pad, gdn_norm)


def _gdn_sample_prep_body(q_ref, k_ref, v_ref, ab_ref, st_ref, cw_ref, alog_ref, dtb_ref,
                          qn_ref, kn_ref, vc_ref, eg_ref, beta_ref, qk_ref):
    def conv(part, u_ref):
        cols = slice(part * GDN_DIM, (part + 1) * GDN_DIM)
        acc = u_ref[...] * cw_ref[CONV_BUF:CONV_WIDTH, cols]
        for i in range(CONV_BUF):
            acc = acc + st_ref[i, :, cols] * cw_ref[i:i + 1, cols]
        return _silu(acc)

    qc = conv(0, q_ref)
    kc = conv(1, k_ref)
    vc_ref[...] = conv(2, v_ref)
    ab = ab_ref[...]
    g_all = -jnp.exp(alog_ref[...]) * _softplus(ab + dtb_ref[...])
    eg_all = jnp.exp(g_all)
    beta_all = jax.nn.sigmoid(ab)
    for h in range(HEADS):
        lanes = slice(h * HEAD_DIM, (h + 1) * HEAD_DIM)
        qh, kh = qc[:, lanes], kc[:, lanes]
        qn = qh * lax.rsqrt(jnp.sum(qh * qh, axis=-1, keepdims=True) + EPS) * (HEAD_DIM ** -0.5)
        kn = kh * lax.rsqrt(jnp.sum(kh * kh, axis=-1, keepdims=True) + EPS)
        qn_ref[:, lanes] = qn
        kn_ref[:, lanes] = kn
        shape = qn.shape
        qk_ref[:, lanes] = jnp.broadcast_to(jnp.sum(qn * kn, axis=-1, keepdims=True), shape)
        eg_ref[:, lanes] = jnp.broadcast_to(eg_all[:, h:h + 1], shape)
        beta_ref[:, lanes] = jnp.broadcast_to(beta_all[:, HEADS + h:HEADS + h + 1], shape)


def _gdn_sample_prep(proj, ab, conv_state_t, conv_w, alog_pad, dtb_pad):
    nb = proj.shape[0]

    def col_spec(col0):
        return pl.BlockSpec((nb, GDN_DIM), lambda i: (0, col0 // GDN_DIM))

    small = pl.BlockSpec((1, LANES), lambda i: (0, 0))
    full = pl.BlockSpec((nb, GDN_DIM), lambda i: (0, 0))
    return pl.pallas_call(
        _gdn_sample_prep_body,
        grid=(1,),
        in_specs=[col_spec(C_Q), col_spec(C_K), col_spec(C_V),
                  pl.BlockSpec((nb, LANES), lambda i: (0, 0)),
                  pl.BlockSpec((CONV_BUF, nb, QKV_DIM), lambda i: (0, 0, 0)),
                  pl.BlockSpec((CONV_WIDTH, QKV_DIM), lambda i: (0, 0)),
                  small, small],
        out_specs=[full] * 6,
        out_shape=[jax.ShapeDtypeStruct((nb, GDN_DIM), F32)] * 6,
        compiler_params=_cparams("arbitrary"),
        name="gdn_sample_prep",
    )(proj, proj, proj, ab, conv_state_t, conv_w, alog_pad, dtb_pad)


def _gdn_sample_step_body(qn_ref, kn_ref, vc_ref, eg_ref, beta_ref, qk_ref, z_ref, gn_ref, s_ref,
                          o_ref, snew_ref, *, bb):
    pad = jnp.zeros((LANES - HEADS, HEAD_DIM), F32)
    for b in range(bb):
        k16, q16 = kn_ref[b], qn_ref[b]
        cols = jnp.concatenate([k16, pad], axis=0).T
        kq = jnp.concatenate([k16, q16], axis=0).astype(BF16)
        outs = []
        for h in range(HEADS):
            kcol = cols[:, h:h + 1]
            s = s_ref[0, b, h]
            kq_s = _dot(kq, s.astype(BF16))
            k_s = kq_s[h:h + 1]
            q_s = kq_s[HEADS + h:HEADS + h + 1]
            eg = eg_ref[b, h:h + 1, :]
            v_new = beta_ref[b, h:h + 1, :] * (vc_ref[b, h:h + 1, :] - eg * k_s)
            snew_ref[0, b, h] = eg * s + kcol * v_new
            outs.append(eg * q_s + qk_ref[b, h:h + 1, :] * v_new)
        o = jnp.concatenate(outs, axis=0)
        o_ref[b] = (_rms(o, gn_ref[...]) * _silu(z_ref[b])).astype(BF16)


def _gdn_sample_step(qn, kn, vc, eg, beta, qk, z, gdn_norm, state_ssm, *, bb):
    nb = qn.shape[0]
    body = functools.partial(_gdn_sample_step_body, bb=bb)
    vec = pl.BlockSpec((bb, HEADS, HEAD_DIM), lambda i: (i, 0, 0))
    st = pl.BlockSpec((1, bb, HEADS, HEAD_DIM, HEAD_DIM), lambda i: (0, i, 0, 0, 0))
    return pl.pallas_call(
        body,
        grid=(nb // bb,),
        in_specs=[vec] * 7 + [pl.BlockSpec((1, LANES), lambda i: (0, 0)), st],
        out_specs=[vec, st],
        out_shape=[jax.ShapeDtypeStruct((nb, HEADS, HEAD_DIM), BF16),
                   jax.ShapeDtypeStruct(state_ssm.shape, F32)],
        compiler_params=_cparams("parallel"),
        name="gdn_sample_step",
    )(qn, kn, vc, eg, beta, qk, z, gdn_norm, state_ssm)


def _merge_body(pool_ref, gdn_ref, wpu_ref, wgu_ref, gp_ref, gg_ref, o_ref):
    o_ref[...] = (gp_ref[...] * _dot(pool_ref[...], wpu_ref[...])
                  + gg_ref[...] * _dot(gdn_ref[...], wgu_ref[...])).astype(BF16)


def _merge(proj, pool_out, gdn_out, w_pool_up, w_gdn_up, *, tm, tn=512):
    m = proj.shape[0]
    nj = D_MODEL // tn
    return pl.pallas_call(
        _merge_body,
        grid=(m // tm, nj),
        in_specs=[
            pl.BlockSpec((tm, POOL_DIM), lambda i, j: (i, 0)),
            pl.BlockSpec((tm, GDN_DIM), lambda i, j: (i, 0)),
            pl.BlockSpec((POOL_DIM, tn), lambda i, j: (0, j)),
            pl.BlockSpec((GDN_DIM, tn), lambda i, j: (0, j)),
            pl.BlockSpec((tm, tn), lambda i, j: (i, C_GP // tn + j)),
            pl.BlockSpec((tm, tn), lambda i, j: (i, C_GG // tn + j)),
        ],
        out_specs=pl.BlockSpec((tm, tn), lambda i, j: (i, j)),
        out_shape=jax.ShapeDtypeStruct((m, D_MODEL), BF16),
        compiler_params=_cparams("parallel", "parallel"),
        name="merge",
    )(pool_out, gdn_out, w_pool_up, w_gdn_up, proj, proj)


def _outproj_body(a_ref, w_ref, x_ref, o_ref):
    o_ref[...] = x_ref[...] + _dot(a_ref[...], w_ref[...])


def _outproj(merged, x, w_o, *, tm, tn=512):
    m = x.shape[0]
    return pl.pallas_call(
        _outproj_body,
        grid=(m // tm, D_MODEL // tn),
        in_specs=[
            pl.BlockSpec((tm, D_MODEL), lambda i, j: (i, 0)),
            pl.BlockSpec((D_MODEL, tn), lambda i, j: (0, j)),
            pl.BlockSpec((tm, tn), lambda i, j: (i, j)),
        ],
        out_specs=pl.BlockSpec((tm, tn), lambda i, j: (i, j)),
        out_shape=jax.ShapeDtypeStruct((m, D_MODEL), F32),
        compiler_params=_cparams("parallel", "parallel"),
        name="outproj",
    )(merged, w_o, x)


def _ffn_body(x_ref, gain_ref, wg_ref, wu_ref, wd_ref, o_ref, h_ref):
    f = pl.program_id(1)

    @pl.when(f == 0)
    def _():
        x = x_ref[...]
        h_ref[...] = _rms(x, gain_ref[...]).astype(BF16)
        o_ref[...] = x

    h = h_ref[...]
    act = (_silu(_dot(h, wg_ref[...])) * _dot(h, wu_ref[...])).astype(BF16)
    o_ref[...] += _dot(act, wd_ref[...])


def _ffn(x, gain, w_gate_up, w_down, *, tm, tf=512):
    m = x.shape[0]
    nf = D_FF // tf
    return pl.pallas_call(
        _ffn_body,
        grid=(m // tm, nf),
        in_specs=[
            pl.BlockSpec((tm, D_MODEL), lambda i, f: (i, 0)),
            pl.BlockSpec((1, D_MODEL), lambda i, f: (0, 0)),
            pl.BlockSpec((D_MODEL, tf), lambda i, f: (0, f)),
            pl.BlockSpec((D_MODEL, tf), lambda i, f: (0, nf + f)),
            pl.BlockSpec((tf, D_MODEL), lambda i, f: (f, 0)),
        ],
        out_specs=pl.BlockSpec((tm, D_MODEL), lambda i, f: (i, 0)),
        out_shape=jax.ShapeDtypeStruct((m, D_MODEL), F32),
        scratch_shapes=[pltpu.VMEM((tm, D_MODEL), BF16)],
        compiler_params=_cparams("parallel", "arbitrary"),
        name="ffn",
    )(x, gain, w_gate_up, w_gate_up, w_down)


def _final_body(x_ref, p_ref, wple_ref, wpg_ref, gain_ref, o_ref):
    x = x_ref[...]
    gate = jax.nn.sigmoid(_dot(x.astype(BF16), wpg_ref[...]))
    emb = _dot(p_ref[...].astype(BF16), wple_ref[...])
    o_ref[...] = _rms(x + emb * gate, gain_ref[...])


def _final(x, p, w_ple, w_ple_gate, gain, *, tm):
    m = x.shape[0]
    return pl.pallas_call(
        _final_body,
        grid=(m // tm,),
        in_specs=[
            pl.BlockSpec((tm, D_MODEL), lambda i: (i, 0)),
            pl.BlockSpec((tm, PLE_DIM), lambda i: (i, 0)),
            pl.BlockSpec((PLE_DIM, D_MODEL), lambda i: (0, 0)),
            pl.BlockSpec((D_MODEL, D_MODEL), lambda i: (0, 0)),
            pl.BlockSpec((1, D_MODEL), lambda i: (0, 0)),
        ],
        out_specs=pl.BlockSpec((tm, D_MODEL), lambda i: (i, 0)),
        out_shape=jax.ShapeDtypeStruct((m, D_MODEL), F32),
        compiler_params=_cparams("parallel"),
        name="final",
    )(x, p, w_ple, w_ple_gate, gain)


def _tile_rows(m, want):
    return want if m % want == 0 else m


def _dense_tail(x, p, proj, pool_out, gdn_out, wts):
    m = x.shape[0]
    tm = _tile_rows(m, 1024)
    merged = _merge(proj, pool_out, gdn_out, wts["w_pool_up"], wts["w_gdn_up"], tm=tm)
    x = _outproj(merged, x, wts["w_o"], tm=tm)
    x = _ffn(x, wts["norm_ffn"], wts["w_gate_up"], wts["w_down"], tm=tm)
    return _final(x, p, wts["w_ple"], wts["w_ple_gate"], wts["norm_final"], tm=_tile_rows(m, 256))


def kernel(x_prompt, x_sample, p_prompt, p_sample, state_pool, state_conv, state_ssm, norm_mix, w_in, pool_w,
           pool_scale, conv_w, a_log, dt_bias, gdn_norm, w_pool_up, w_gdn_up, w_o, norm_ffn, w_gate_up, w_down,
           w_ple, w_ple_gate, norm_final):
    batch, seq, _ = x_prompt.shape
    nb = x_sample.shape[0]
    w_in_t = jnp.swapaxes(w_in[0], 0, 1)
    wts = dict(
        w_pool_up=w_pool_up[0].astype(BF16), w_gdn_up=w_gdn_up[0].astype(BF16), w_o=w_o[0].astype(BF16),
        norm_ffn=norm_ffn, w_gate_up=w_gate_up[0].astype(BF16), w_down=w_down[0].astype(BF16),
        w_ple=w_ple[0].astype(BF16), w_ple_gate=w_ple_gate[0].astype(BF16), norm_final=norm_final[None, :],
    )
    pool_w_b = pool_w[0].astype(BF16)
    lane_pad = ((0, 0), (0, LANES - HEADS))
    alog_pad = jnp.pad(a_log, lane_pad)
    dtb_pad = jnp.pad(dt_bias, lane_pad)
    gn = gdn_norm

    mp = batch * seq
    xp = x_prompt.reshape(mp, D_MODEL)
    xs = x_sample.reshape(nb, D_MODEL)
    h_all, ab_all = _norm_in(xp, xs, norm_mix, w_in_t, tm=LANES)
    proj_all = _inproj(h_all, w_in_t, tm=_tile_rows(mp + nb, (mp + nb) // 8), tn=1024)

    pool_out_p = _pool_prompt(proj_all, pool_w_b, pool_scale, batch=batch, seq=seq, tt=_tile_rows(seq, 256))
    gdn_out_p, ssm_p = _gdn_prompt(proj_all, ab_all, conv_w[0], alog_pad, dtb_pad, gn, batch=batch, seq=seq,
                                   hg=4, tt=_tile_rows(seq, 512))
    y_p = _dense_tail(xp, p_prompt[0].reshape(mp, PLE_DIM), proj_all, pool_out_p, gdn_out_p, wts)
    seq_tail = lambda n, c0, c1: jnp.stack([proj_all[(b + 1) * seq - n:(b + 1) * seq, c0:c1] for b in range(batch)])
    pool_p = seq_tail(POOL_BUF, C_POOL, C_END)[None]
    conv_p = seq_tail(CONV_BUF, C_Q, C_Z)[None]

    proj_s, ab_s = proj_all[mp:], ab_all[mp:]
    pool_out_s = _pool_sample(proj_s, jnp.swapaxes(state_pool[0], 0, 1), pool_w_b, pool_scale)
    prep = _gdn_sample_prep(proj_s, ab_s, jnp.swapaxes(state_conv[0], 0, 1), conv_w[0], alog_pad, dtb_pad)
    to_heads = lambda a: a.reshape(nb, HEADS, HEAD_DIM)
    gdn_out_s, ssm_s = _gdn_sample_step(*[to_heads(a) for a in prep], to_heads(proj_s[:, C_Z:C_POOL]), gn,
                                        state_ssm, bb=4)
    y_s = _dense_tail(xs, p_sample[0].reshape(nb, PLE_DIM), proj_s, pool_out_s, gdn_out_s.reshape(nb, GDN_DIM), wts)
    pool_s = jnp.concatenate([state_pool[0][:, 1:], proj_s[:, None, C_POOL:C_END]], axis=1)[None]
    conv_s = jnp.concatenate([state_conv[0][:, 1:], proj_s[:, None, C_Q:C_Z]], axis=1)[None]

    return (y_p.reshape(batch, seq, D_MODEL), y_s.reshape(nb, 1, D_MODEL), pool_p, conv_p, ssm_p,
            pool_s, conv_s, ssm_s)
```

```python
import functools

import jax
import jax.numpy as jnp
from jax import lax
from jax.experimental import pallas as pl
from jax.experimental.pallas import tpu as pltpu

F32 = jnp.float32
BF16 = jnp.bfloat16

D_MODEL = 2048
PAST_LEN = 16384
POOL_WINDOWS = (2, 4, 8, 16)
POOL_GROUP_DIM = D_MODEL // 8
POOL_DIM = len(POOL_WINDOWS) * POOL_GROUP_DIM
POOL_BUF = max(POOL_WINDOWS) - 1
HEAD_DIM = 128
HEADS = D_MODEL // HEAD_DIM
GDN_DIM = HEADS * HEAD_DIM
QKV_DIM = 3 * GDN_DIM
CONV_WIDTH = 4
CONV_BUF = CONV_WIDTH - 1
D_FF = -(-8 * D_MODEL // (3 * 256)) * 256
PLE_DIM = 256
EPS = 1e-6

IN_QKV = POOL_DIM
IN_Z = IN_QKV + QKV_DIM
IN_AB = IN_Z + GDN_DIM
IN_GATES = IN_AB + 2 * HEADS
C_GP = 0
C_GG = C_GP + D_MODEL
C_Q = C_GG + D_MODEL
C_K = C_Q + GDN_DIM
C_V = C_K + GDN_DIM
C_Z = C_V + GDN_DIM
C_POOL = C_Z + GDN_DIM
C_END = C_POOL + POOL_DIM

LANES = 128
SUBLANES = 8
VMEM_LIMIT_BYTES = 56 * 1024 * 1024
CHUNK = 128


def _cparams(*sem):
    return pltpu.CompilerParams(dimension_semantics=sem, vmem_limit_bytes=VMEM_LIMIT_BYTES)


def _rms(x, gain):
    return x * lax.rsqrt(jnp.mean(x * x, axis=-1, keepdims=True) + EPS) * gain


def _dot(a, b):
    return jnp.dot(a, b, preferred_element_type=F32)


def _dot_nt(a, b):
    return lax.dot_general(a, b, (((1,), (1,)), ((), ())), preferred_element_type=F32)


def _silu(x):
    return x * jax.nn.sigmoid(x)


def _norm_in_body(x_ref, gain_ref, wab_ref, *rest):
    h_ref, ab_ref = rest[-2:]
    h = _rms(x_ref[...], gain_ref[...]).astype(BF16)
    h_ref[...] = h
    ab_ref[...] = _dot_nt(h, wab_ref[...].astype(BF16))


def _norm_in(x, gain, w_in_t, *, tm, total_rows, row0=0, into=None):
    m = x.shape[0]
    assert m % tm == 0 and row0 % tm == 0
    blk0 = row0 // tm
    in_specs = [
        pl.BlockSpec((tm, D_MODEL), lambda i: (i, 0)),
        pl.BlockSpec((1, D_MODEL), lambda i: (0, 0)),
        pl.BlockSpec((LANES, D_MODEL), lambda i: (IN_AB // LANES, 0)),
    ]
    args = (x, gain, w_in_t)
    aliases = {}
    if into is not None:
        in_specs += [pl.BlockSpec(memory_space=pl.ANY)] * 2
        args += tuple(into)
        aliases = {3: 0, 4: 1}
    return pl.pallas_call(
        _norm_in_body,
        grid=(m // tm,),
        in_specs=in_specs,
        out_specs=[
            pl.BlockSpec((tm, D_MODEL), lambda i: (blk0 + i, 0)),
            pl.BlockSpec((tm, LANES), lambda i: (blk0 + i, 0)),
        ],
        out_shape=[jax.ShapeDtypeStruct((total_rows, D_MODEL), BF16), jax.ShapeDtypeStruct((total_rows, LANES), F32)],
        input_output_aliases=aliases,
        compiler_params=_cparams("parallel"),
        name="norm_in",
    )(*args)


def _inproj_row_start(j, tn):
    n_gate = (2 * D_MODEL) // tn
    n_mid = (QKV_DIM + GDN_DIM) // tn
    row = jnp.where(j < n_gate, IN_GATES + j * tn,
                    jnp.where(j < n_gate + n_mid, IN_QKV + (j - n_gate) * tn, (j - n_gate - n_mid) * tn))
    return pl.multiple_of(row, SUBLANES)


def _inproj_body(h_ref, w_ref, o_ref, wb_ref, *, n_gate_tiles):
    j = pl.program_id(0)

    @pl.when(pl.program_id(1) == 0)
    def _():
        wb_ref[...] = w_ref[...].astype(BF16)

    @pl.when(j < n_gate_tiles)
    def _():
        o_ref[...] = jax.nn.sigmoid(_dot_nt(h_ref[...], wb_ref[...]))

    @pl.when(j >= n_gate_tiles)
    def _():
        o_ref[...] = _dot_nt(h_ref[...], wb_ref[...])


def _inproj(h, w_in_t, *, tm, tn):
    m = h.shape[0]
    body = functools.partial(_inproj_body, n_gate_tiles=(2 * D_MODEL) // tn)
    return pl.pallas_call(
        body,
        grid=(C_END // tn, m // tm),
        in_specs=[
            pl.BlockSpec((tm, D_MODEL), lambda j, i: (i, 0)),
            pl.BlockSpec((pl.Element(tn), pl.Element(D_MODEL)), lambda j, i: (_inproj_row_start(j, tn), 0)),
        ],
        out_specs=pl.BlockSpec((tm, tn), lambda j, i: (i, j)),
        out_shape=jax.ShapeDtypeStruct((m, C_END), F32),
        scratch_shapes=[pltpu.VMEM((tn, D_MODEL), BF16)],
        compiler_params=_cparams("arbitrary", "arbitrary"),
        name="inproj",
    )(h, w_in_t)


def _pool_group_out(mean, tok, pw_ref, scale_ref, gi):
    cols = slice(gi * POOL_GROUP_DIM, (gi + 1) * POOL_GROUP_DIM)
    y = _dot((mean - tok).astype(BF16), pw_ref[gi])
    return (y * scale_ref[:, cols]).astype(BF16)


def _pool_prompt_body(u_ref, pw_ref, scale_ref, o_ref, ext_ref, *, tt, start_pos):
    t = pl.program_id(1)
    halo = POOL_BUF + 1

    @pl.when(t == 0)
    def _():
        ext_ref[0:halo, :] = jnp.zeros((halo, POOL_DIM), F32)

    ext_ref[halo:halo + tt, :] = u_ref[...]
    pos = start_pos + t * tt + lax.broadcasted_iota(jnp.int32, (tt, 1), 0)
    for gi, w in enumerate(POOL_WINDOWS):
        cols = slice(gi * POOL_GROUP_DIM, (gi + 1) * POOL_GROUP_DIM)
        tok = ext_ref[halo:halo + tt, cols]
        tot = tok
        for i in range(1, w):
            tot = tot + ext_ref[pl.ds(halo - i, tt), cols]
        cnt = jnp.minimum(pos + 1, w).astype(F32)
        o_ref[:, cols] = _pool_group_out(tot / cnt, tok, pw_ref, scale_ref, gi)
    ext_ref[0:halo, :] = ext_ref[tt:tt + halo, :]


def _pool_prompt(proj, pool_w, pool_scale, *, batch, seq, tt):
    nt = seq // tt
    body = functools.partial(_pool_prompt_body, tt=tt, start_pos=0)
    return pl.pallas_call(
        body,
        grid=(batch, nt),
        in_specs=[
            pl.BlockSpec((tt, POOL_DIM), lambda b, t: (b * nt + t, C_POOL // POOL_DIM)),
            pl.BlockSpec((len(POOL_WINDOWS), POOL_GROUP_DIM, POOL_GROUP_DIM), lambda b, t: (0, 0, 0)),
            pl.BlockSpec((1, POOL_DIM), lambda b, t: (0, 0)),
        ],
        out_specs=pl.BlockSpec((tt, POOL_DIM), lambda b, t: (b * nt + t, 0)),
        out_shape=jax.ShapeDtypeStruct((batch * seq, POOL_DIM), BF16),
        scratch_shapes=[pltpu.VMEM((POOL_BUF + 1 + tt, POOL_DIM), F32)],
        compiler_params=_cparams("parallel", "arbitrary"),
        name="pool_prompt",
    )(proj, pool_w, pool_scale)


def _pool_sample_body(u_ref, st_ref, pw_ref, scale_ref, o_ref, *, start_pos):
    for gi, w in enumerate(POOL_WINDOWS):
        cols = slice(gi * POOL_GROUP_DIM, (gi + 1) * POOL_GROUP_DIM)
        tok = u_ref[:, cols]
        tot = tok
        for i in range(1, w):
            tot = tot + st_ref[POOL_BUF - i, :, cols]
        cnt = float(min(start_pos + 1, w))
        o_ref[:, cols] = _pool_group_out(tot / cnt, tok, pw_ref, scale_ref, gi)


def _pool_sample(proj, state_t, pool_w, pool_scale):
    nb = proj.shape[0]
    body = functools.partial(_pool_sample_body, start_pos=PAST_LEN)
    return pl.pallas_call(
        body,
        grid=(1,),
        in_specs=[
            pl.BlockSpec((nb, POOL_DIM), lambda i: (0, C_POOL // POOL_DIM)),
            pl.BlockSpec((POOL_BUF, nb, POOL_DIM), lambda i: (0, 0, 0)),
            pl.BlockSpec((len(POOL_WINDOWS), POOL_GROUP_DIM, POOL_GROUP_DIM), lambda i: (0, 0, 0)),
            pl.BlockSpec((1, POOL_DIM), lambda i: (0, 0)),
        ],
        out_specs=pl.BlockSpec((nb, POOL_DIM), lambda i: (0, 0)),
        out_shape=jax.ShapeDtypeStruct((nb, POOL_DIM), BF16),
        compiler_params=_cparams("arbitrary"),
        name="pool_sample",
    )(proj, state_t, pool_w, pool_scale)


def _split3(x):
    hi = x.astype(BF16)
    r1 = x - hi.astype(F32)
    mid = r1.astype(BF16)
    lo = (r1 - mid.astype(F32)).astype(BF16)
    return hi, mid, lo


def _split2(x):
    hi = x.astype(BF16)
    return hi, (x - hi.astype(F32)).astype(BF16)


def _softplus(x):
    return jnp.maximum(x, 0.0) + jnp.log1p(jnp.exp(-jnp.abs(x)))


def _unit_lower_inverse_minus_identity(mats):
    eye = jnp.where(lax.broadcasted_iota(jnp.int32, (CHUNK, CHUNK), 0)
                    == lax.broadcasted_iota(jnp.int32, (CHUNK, CHUNK), 1), 1.0, 0.0)
    levels = CHUNK.bit_length() - 1
    qbs = [(-a).astype(BF16) for a in mats]
    rs = [eye - a for a in mats]
    qs = [_dot(qb, qb) for qb in qbs]
    for j in range(1, levels):
        qbs = [q.astype(BF16) for q in qs]
        rbs = [r.astype(BF16) for r in rs]
        if j < levels - 1:
            ps = [_dot(qb, jnp.concatenate([qb, rb], axis=1)) for qb, rb in zip(qbs, rbs)]
            rs = [r + p[:, CHUNK:] for r, p in zip(rs, ps)]
            qs = [p[:, :CHUNK] for p in ps]
        else:
            rs = [r + _dot(qb, rb) for r, qb, rb in zip(rs, qbs, rbs)]
    return [r - eye for r in rs]


def _gdn_prompt_body(q_ref, k_ref, v_ref, z_ref, ab_ref, cwq_ref, cwk_ref, cwv_ref, alog_ref, dtb_ref, gn_ref,
                     o_ref, ssm_ref, ext_ref, s_ref, gct_ref, *, hg, tt):
    hgi = pl.program_id(1)
    t = pl.program_id(2)
    w = hg * HEAD_DIM
    halo = SUBLANES

    @pl.when(t == 0)
    def _():
        ext_ref[0:halo, :] = jnp.zeros((halo, 3 * w), F32)
        s_ref[...] = jnp.zeros_like(s_ref)

    ext_ref[halo:halo + tt, 0:w] = q_ref[...]
    ext_ref[halo:halo + tt, w:2 * w] = k_ref[...]
    ext_ref[halo:halo + tt, 2 * w:3 * w] = v_ref[...]

    def conv(part, cw_ref):
        cols = slice(part * w, (part + 1) * w)
        acc = ext_ref[halo:halo + tt, cols] * cw_ref[CONV_BUF:CONV_WIDTH, :]
        for i in range(CONV_BUF):
            acc = acc + ext_ref[pl.ds(halo - CONV_BUF + i, tt), cols] * cw_ref[i:i + 1, :]
        return _silu(acc)

    qc = conv(0, cwq_ref)
    kc = conv(1, cwk_ref)
    vc = conv(2, cwv_ref)
    ext_ref[0:halo, :] = ext_ref[tt:tt + halo, :]

    ab = ab_ref[...]
    g_all = -jnp.exp(alog_ref[...]) * _softplus(ab + dtb_ref[...])
    beta_all = jax.nn.sigmoid(ab)
    ri = lax.broadcasted_iota(jnp.int32, (tt, tt), 0)
    ci = lax.broadcasted_iota(jnp.int32, (tt, tt), 1)
    shift = CHUNK.bit_length() - 1
    tri = jnp.where(((ri >> shift) == (ci >> shift)) & (ci <= ri), 1.0, 0.0).astype(BF16)
    g_hi, g_mid, g_lo = _split3(g_all)
    gc_all = _dot(tri, g_hi) + _dot(tri, g_mid) + _dot(tri, g_lo)
    gct_ref[...] = gc_all.T

    lane = lax.broadcasted_iota(jnp.int32, (tt, LANES), 1)
    r128 = lax.broadcasted_iota(jnp.int32, (CHUNK, CHUNK), 0)
    c128 = lax.broadcasted_iota(jnp.int32, (CHUNK, CHUNK), 1)
    sub8 = lax.broadcasted_iota(jnp.int32, (SUBLANES, CHUNK), 0)
    n_chunks = tt // CHUNK
    chains = [(hh, c) for hh in range(hg) for c in range(n_chunks)]
    gcol_heads, bcol_heads = [], []
    for hh in range(hg):
        head = hgi * hg + hh
        gcol_heads.append(jnp.sum(jnp.where(lane == head, gc_all, 0.0), axis=-1, keepdims=True))
        bcol_heads.append(jnp.sum(jnp.where(lane == head + HEADS, beta_all, 0.0), axis=-1, keepdims=True))

    qn_l, kn_l, kb_l, rhs_l, eg_l, gcol_l, dec_l = [], [], [], [], [], [], []
    for hh, c in chains:
        head = hgi * hg + hh
        rows = slice(c * CHUNK, (c + 1) * CHUNK)
        lanes = slice(hh * HEAD_DIM, (hh + 1) * HEAD_DIM)
        qh, kh, vh = qc[rows, lanes], kc[rows, lanes], vc[rows, lanes]
        qn = qh * lax.rsqrt(jnp.sum(qh * qh, axis=-1, keepdims=True) + EPS) * (HEAD_DIM ** -0.5)
        kn = kh * lax.rsqrt(jnp.sum(kh * kh, axis=-1, keepdims=True) + EPS)
        gcol = gcol_heads[hh][rows]
        bcol = bcol_heads[hh][rows]
        gblk = gct_ref[pl.ds(pl.multiple_of((head >> 3) * SUBLANES, SUBLANES), SUBLANES), rows]
        grow = jnp.sum(jnp.where(sub8 == (head & (SUBLANES - 1)), gblk, 0.0), axis=0, keepdims=True)
        eg = jnp.exp(gcol)
        kb = kn * bcol
        qn_l.append(qn)
        kn_l.append(kn)
        kb_l.append(kb)
        eg_l.append(eg)
        gcol_l.append(gcol)
        dec_l.append(jnp.exp(jnp.minimum(gcol - grow, 0.0)))
        rhs_l.append(jnp.concatenate([vh * bcol, kb * eg], axis=1))

    a_l, qk_l = [], []
    for qn, kn, kb, decay in zip(qn_l, kn_l, kb_l, dec_l):
        knb = kn.astype(BF16)
        a_l.append(_dot_nt(kb.astype(BF16), knb) * jnp.where(c128 < r128, decay, 0.0))
        qk_l.append(_dot_nt(qn.astype(BF16), knb) * jnp.where(c128 <= r128, decay, 0.0))
    tinv_l = [x.astype(BF16) for x in _unit_lower_inverse_minus_identity(a_l)]
    sol_l = [rhs + _dot(tinv, rhs.astype(BF16)) for tinv, rhs in zip(tinv_l, rhs_l)]
    res_l = []
    for a, sol, rhs in zip(a_l, sol_l, rhs_l):
        a_hi, a_lo = _split2(a)
        s_hi, s_lo = _split2(sol)
        a_sol = _dot(jnp.concatenate([a_hi, a_lo, a_hi], axis=1), jnp.concatenate([s_hi, s_hi, s_lo], axis=0))
        res_l.append((rhs - sol) - a_sol)
    sol_l = [(sol + res + _dot(tinv, res.astype(BF16))).astype(BF16) for sol, res, tinv in zip(sol_l, res_l, tinv_l)]
    lhs_l, o_add_l, s_add_l, glast_l = [], [], [], []
    for qn, kn, eg, gcol, qk, sol_b in zip(qn_l, kn_l, eg_l, gcol_l, qk_l, sol_l):
        glast = gcol[CHUNK - 1:CHUNK, :]
        ktail = kn * jnp.exp(glast - gcol)
        kt_sol = _dot(ktail.T.astype(BF16), sol_b)
        qk_sol = _dot(qk.astype(BF16), sol_b)
        lhs_l.append(jnp.concatenate([qn * eg - qk_sol[:, HEAD_DIM:], kt_sol[:, HEAD_DIM:]], axis=0).astype(BF16))
        o_add_l.append(qk_sol[:, :HEAD_DIM])
        s_add_l.append(kt_sol[:, :HEAD_DIM])
        glast_l.append(jnp.exp(glast))
    states = [s_ref[hh] for hh in range(hg)]
    for c in range(n_chunks):
        for hh in range(hg):
            i = hh * n_chunks + c
            rows = slice(c * CHUNK, (c + 1) * CHUNK)
            lanes = slice(hh * HEAD_DIM, (hh + 1) * HEAD_DIM)
            on_s = _dot(lhs_l[i], states[hh].astype(BF16))
            o = on_s[:CHUNK] + o_add_l[i]
            states[hh] = states[hh] * glast_l[i] - on_s[CHUNK:] + s_add_l[i]
            o_ref[rows, lanes] = (_rms(o, gn_ref[...]) * _silu(z_ref[rows, lanes])).astype(BF16)
    for hh in range(hg):
        s_ref[hh] = states[hh]

    @pl.when(t == pl.num_programs(2) - 1)
    def _():
        ssm_ref[0, 0] = s_ref[...]


def _gdn_prompt(proj, ab, conv_w, alog_pad, dtb_pad, gdn_norm, *, batch, seq, hg, tt):
    nt = seq // tt
    w = hg * HEAD_DIM
    body = functools.partial(_gdn_prompt_body, hg=hg, tt=tt)

    def col_spec(col0):
        return pl.BlockSpec((tt, w), lambda b, h, t: (b * nt + t, col0 // w + h))

    def cw_spec(part):
        return pl.BlockSpec((CONV_WIDTH, w), lambda b, h, t: (0, part * GDN_DIM // w + h))

    small = pl.BlockSpec((1, LANES), lambda b, h, t: (0, 0))
    return pl.pallas_call(
        body,
        grid=(batch, HEADS // hg, nt),
        in_specs=[col_spec(C_Q), col_spec(C_K), col_spec(C_V), col_spec(C_Z),
                  pl.BlockSpec((tt, LANES), lambda b, h, t: (b * nt + t, 0)),
                  cw_spec(0), cw_spec(1), cw_spec(2), small, small, small],
        out_specs=[
            pl.BlockSpec((tt, w), lambda b, h, t: (b * nt + t, h)),
            pl.BlockSpec((1, 1, hg, HEAD_DIM, HEAD_DIM), lambda b, h, t: (0, b, h, 0, 0)),
        ],
        out_shape=[
            jax.ShapeDtypeStruct((batch * seq, GDN_DIM), BF16),
            jax.ShapeDtypeStruct((1, batch, HEADS, HEAD_DIM, HEAD_DIM), F32),
        ],
        scratch_shapes=[
            pltpu.VMEM((SUBLANES + tt, 3 * w), F32),
            pltpu.VMEM((hg, HEAD_DIM, HEAD_DIM), F32),
            pltpu.VMEM((LANES, tt), F32),
        ],
        compiler_params=_cparams("parallel", "parallel", "arbitrary"),
        name="gdn_prompt",
    )(proj, proj, proj, proj, ab, conv_w, conv_w, conv_w, alog_pad, dtb_pad, gdn_norm)


def _gdn_sample_prep_body(q_ref, k_ref, v_ref, ab_ref, st_ref, cw_ref, alog_ref, dtb_ref,
                          qn_ref, kn_ref, vc_ref, eg_ref, beta_ref, qk_ref):
    def conv(part, u_ref):
        cols = slice(part * GDN_DIM, (part + 1) * GDN_DIM)
        acc = u_ref[...] * cw_ref[CONV_BUF:CONV_WIDTH, cols]
        for i in range(CONV_BUF):
            acc = acc + st_ref[i, :, cols] * cw_ref[i:i + 1, cols]
        return _silu(acc)

    qc = conv(0, q_ref)
    kc = conv(1, k_ref)
    vc_ref[...] = conv(2, v_ref)
    ab = ab_ref[...]
    g_all = -jnp.exp(alog_ref[...]) * _softplus(ab + dtb_ref[...])
    eg_all = jnp.exp(g_all)
    beta_all = jax.nn.sigmoid(ab)
    for h in range(HEADS):
        lanes = slice(h * HEAD_DIM, (h + 1) * HEAD_DIM)
        qh, kh = qc[:, lanes], kc[:, lanes]
        qn = qh * lax.rsqrt(jnp.sum(qh * qh, axis=-1, keepdims=True) + EPS) * (HEAD_DIM ** -0.5)
        kn = kh * lax.rsqrt(jnp.sum(kh * kh, axis=-1, keepdims=True) + EPS)
        qn_ref[:, lanes] = qn
        kn_ref[:, lanes] = kn
        shape = qn.shape
        qk_ref[:, lanes] = jnp.broadcast_to(jnp.sum(qn * kn, axis=-1, keepdims=True), shape)
        eg_ref[:, lanes] = jnp.broadcast_to(eg_all[:, h:h + 1], shape)
        beta_ref[:, lanes] = jnp.broadcast_to(beta_all[:, HEADS + h:HEADS + h + 1], shape)


def _gdn_sample_prep(proj, ab, conv_state_t, conv_w, alog_pad, dtb_pad):
    nb = proj.shape[0]

    def col_spec(col0):
        return pl.BlockSpec((nb, GDN_DIM), lambda i: (0, col0 // GDN_DIM))

    small = pl.BlockSpec((1, LANES), lambda i: (0, 0))
    full = pl.BlockSpec((nb, GDN_DIM), lambda i: (0, 0))
    return pl.pallas_call(
        _gdn_sample_prep_body,
        grid=(1,),
        in_specs=[col_spec(C_Q), col_spec(C_K), col_spec(C_V),
                  pl.BlockSpec((nb, LANES), lambda i: (0, 0)),
                  pl.BlockSpec((CONV_BUF, nb, QKV_DIM), lambda i: (0, 0, 0)),
                  pl.BlockSpec((CONV_WIDTH, QKV_DIM), lambda i: (0, 0)),
                  small, small],
        out_specs=[full] * 6,
        out_shape=[jax.ShapeDtypeStruct((nb, GDN_DIM), F32)] * 6,
        compiler_params=_cparams("arbitrary"),
        name="gdn_sample_prep",
    )(proj, proj, proj, ab, conv_state_t, conv_w, alog_pad, dtb_pad)


def _gdn_sample_step_body(qn_ref, kn_ref, vc_ref, eg_ref, beta_ref, qk_ref, z_ref, gn_ref, s_ref,
                          o_ref, snew_ref, *, bb):
    pad = jnp.zeros((LANES - HEADS, HEAD_DIM), F32)
    for b in range(bb):
        k16, q16 = kn_ref[b], qn_ref[b]
        cols = jnp.concatenate([k16, pad], axis=0).T
        kq = jnp.concatenate([k16, q16], axis=0).astype(BF16)
        outs = []
        for h in range(HEADS):
            kcol = cols[:, h:h + 1]
            s = s_ref[0, b, h]
            kq_s = _dot(kq, s.astype(BF16))
            k_s = kq_s[h:h + 1]
            q_s = kq_s[HEADS + h:HEADS + h + 1]
            eg = eg_ref[b, h:h + 1, :]
            v_new = beta_ref[b, h:h + 1, :] * (vc_ref[b, h:h + 1, :] - eg * k_s)
            snew_ref[0, b, h] = eg * s + kcol * v_new
            outs.append(eg * q_s + qk_ref[b, h:h + 1, :] * v_new)
        o = jnp.concatenate(outs, axis=0)
        o_ref[b] = (_rms(o, gn_ref[...]) * _silu(z_ref[b])).astype(BF16)


def _gdn_sample_step(qn, kn, vc, eg, beta, qk, z, gdn_norm, state_ssm, *, bb):
    nb = qn.shape[0]
    body = functools.partial(_gdn_sample_step_body, bb=bb)
    vec = pl.BlockSpec((bb, HEADS, HEAD_DIM), lambda i: (i, 0, 0))
    st = pl.BlockSpec((1, bb, HEADS, HEAD_DIM, HEAD_DIM), lambda i: (0, i, 0, 0, 0))
    return pl.pallas_call(
        body,
        grid=(nb // bb,),
        in_specs=[vec] * 7 + [pl.BlockSpec((1, LANES), lambda i: (0, 0)), st],
        out_specs=[vec, st],
        out_shape=[jax.ShapeDtypeStruct((nb, HEADS, HEAD_DIM), BF16),
                   jax.ShapeDtypeStruct(state_ssm.shape, F32)],
        compiler_params=_cparams("parallel"),
        name="gdn_sample_step",
    )(qn, kn, vc, eg, beta, qk, z, gdn_norm, state_ssm)


def _merge_body(pool_ref, gdn_ref, wpu_ref, wgu_ref, gp_ref, gg_ref, o_ref):
    o_ref[...] = (gp_ref[...] * _dot(pool_ref[...], wpu_ref[...])
                  + gg_ref[...] * _dot(gdn_ref[...], wgu_ref[...])).astype(BF16)


def _merge(proj, pool_out, gdn_out, w_pool_up, w_gdn_up, *, tm, tn=512):
    m = proj.shape[0]
    nj = D_MODEL // tn
    return pl.pallas_call(
        _merge_body,
        grid=(m // tm, nj),
        in_specs=[
            pl.BlockSpec((tm, POOL_DIM), lambda i, j: (i, 0)),
            pl.BlockSpec((tm, GDN_DIM), lambda i, j: (i, 0)),
            pl.BlockSpec((POOL_DIM, tn), lambda i, j: (0, j)),
            pl.BlockSpec((GDN_DIM, tn), lambda i, j: (0, j)),
            pl.BlockSpec((tm, tn), lambda i, j: (i, C_GP // tn + j)),
            pl.BlockSpec((tm, tn), lambda i, j: (i, C_GG // tn + j)),
        ],
        out_specs=pl.BlockSpec((tm, tn), lambda i, j: (i, j)),
        out_shape=jax.ShapeDtypeStruct((m, D_MODEL), BF16),
        compiler_params=_cparams("parallel", "parallel"),
        name="merge",
    )(pool_out, gdn_out, w_pool_up, w_gdn_up, proj, proj)


def _outproj_body(a_ref, w_ref, x_ref, o_ref):
    o_ref[...] = x_ref[...] + _dot(a_ref[...], w_ref[...])


def _outproj(merged, x, w_o, *, tm, tn=512):
    m = x.shape[0]
    return pl.pallas_call(
        _outproj_body,
        grid=(m // tm, D_MODEL // tn),
        in_specs=[
            pl.BlockSpec((tm, D_MODEL), lambda i, j: (i, 0)),
            pl.BlockSpec((D_MODEL, tn), lambda i, j: (0, j)),
            pl.BlockSpec((tm, tn), lambda i, j: (i, j)),
        ],
        out_specs=pl.BlockSpec((tm, tn), lambda i, j: (i, j)),
        out_shape=jax.ShapeDtypeStruct((m, D_MODEL), F32),
        compiler_params=_cparams("parallel", "parallel"),
        name="outproj",
    )(merged, w_o, x)


def _ffn_body(x_ref, gain_ref, wg_ref, wu_ref, wd_ref, o_ref, h_ref):
    f = pl.program_id(1)

    @pl.when(f == 0)
    def _():
        x = x_ref[...]
        h_ref[...] = _rms(x, gain_ref[...]).astype(BF16)
        o_ref[...] = x

    h = h_ref[...]
    act = (_silu(_dot(h, wg_ref[...].astype(BF16))) * _dot(h, wu_ref[...].astype(BF16))).astype(BF16)
    o_ref[...] += _dot(act, wd_ref[...].astype(BF16))


def _ffn(x, gain, w_gate_up, w_down, *, tm, tf):
    m = x.shape[0]
    nf = D_FF // tf
    return pl.pallas_call(
        _ffn_body,
        grid=(m // tm, nf),
        in_specs=[
            pl.BlockSpec((tm, D_MODEL), lambda i, f: (i, 0)),
            pl.BlockSpec((1, D_MODEL), lambda i, f: (0, 0)),
            pl.BlockSpec((D_MODEL, tf), lambda i, f: (0, f)),
            pl.BlockSpec((D_MODEL, tf), lambda i, f: (0, nf + f)),
            pl.BlockSpec((tf, D_MODEL), lambda i, f: (f, 0)),
        ],
        out_specs=pl.BlockSpec((tm, D_MODEL), lambda i, f: (i, 0)),
        out_shape=jax.ShapeDtypeStruct((m, D_MODEL), F32),
        scratch_shapes=[pltpu.VMEM((tm, D_MODEL), BF16)],
        compiler_params=_cparams("parallel", "arbitrary"),
        name="ffn",
    )(x, gain, w_gate_up, w_gate_up, w_down)


def _final_body(x_ref, p_ref, wple_ref, wpg_ref, gain_ref, o_ref):
    x = x_ref[...]
    gate = jax.nn.sigmoid(_dot(x.astype(BF16), wpg_ref[...]))
    emb = _dot(p_ref[...].astype(BF16), wple_ref[...])
    o_ref[...] = _rms(x + emb * gate, gain_ref[...])


def _final(x, p, w_ple, w_ple_gate, gain, *, tm):
    m = x.shape[0]
    return pl.pallas_call(
        _final_body,
        grid=(m // tm,),
        in_specs=[
            pl.BlockSpec((tm, D_MODEL), lambda i: (i, 0)),
            pl.BlockSpec((tm, PLE_DIM), lambda i: (i, 0)),
            pl.BlockSpec((PLE_DIM, D_MODEL), lambda i: (0, 0)),
            pl.BlockSpec((D_MODEL, D_MODEL), lambda i: (0, 0)),
            pl.BlockSpec((1, D_MODEL), lambda i: (0, 0)),
        ],
        out_specs=pl.BlockSpec((tm, D_MODEL), lambda i: (i, 0)),
        out_shape=jax.ShapeDtypeStruct((m, D_MODEL), F32),
        compiler_params=_cparams("parallel"),
        name="final",
    )(x, p, w_ple, w_ple_gate, gain)


def _tile_rows(m, want):
    return want if m % want == 0 else m


def _dense_tail(x, p, proj, pool_out, gdn_out, wts):
    m = x.shape[0]
    tm = _tile_rows(m, 1024)
    merged = _merge(proj, pool_out, gdn_out, wts["w_pool_up"], wts["w_gdn_up"], tm=tm)
    x = _outproj(merged, x, wts["w_o"], tm=tm)
    x = _ffn(x, wts["norm_ffn"], wts["w_gate_up"], wts["w_down"], tm=tm, tf=256)
    return _final(x, p, wts["w_ple"], wts["w_ple_gate"], wts["norm_final"], tm=_tile_rows(m, 256))


def kernel(x_prompt, x_sample, p_prompt, p_sample, state_pool, state_conv, state_ssm, norm_mix, w_in, pool_w,
           pool_scale, conv_w, a_log, dt_bias, gdn_norm, w_pool_up, w_gdn_up, w_o, norm_ffn, w_gate_up, w_down,
           w_ple, w_ple_gate, norm_final):
    batch, seq, _ = x_prompt.shape
    nb = x_sample.shape[0]
    w_in_t = jnp.swapaxes(w_in[0], 0, 1)
    wts = dict(
        w_pool_up=w_pool_up[0].astype(BF16), w_gdn_up=w_gdn_up[0].astype(BF16), w_o=w_o[0].astype(BF16),
        norm_ffn=norm_ffn, w_gate_up=w_gate_up[0], w_down=w_down[0],
        w_ple=w_ple[0].astype(BF16), w_ple_gate=w_ple_gate[0].astype(BF16), norm_final=norm_final[None, :],
    )
    pool_w_b = pool_w[0].astype(BF16)
    lane_pad = ((0, 0), (0, LANES - HEADS))
    alog_pad = jnp.pad(a_log, lane_pad)
    dtb_pad = jnp.pad(dt_bias, lane_pad)
    gn = gdn_norm

    mp = batch * seq
    xp = x_prompt.reshape(mp, D_MODEL)
    xs = x_sample.reshape(nb, D_MODEL)
    h_ab = _norm_in(xp, norm_mix, w_in_t, tm=_tile_rows(mp, 512), total_rows=mp + nb)
    h_all, ab_all = _norm_in(xs, norm_mix, w_in_t, tm=nb, total_rows=mp + nb, row0=mp, into=h_ab)
    proj_all = _inproj(h_all, w_in_t, tm=_tile_rows(mp + nb, (mp + nb) // 8), tn=1024)

    pool_out_p = _pool_prompt(proj_all, pool_w_b, pool_scale, batch=batch, seq=seq, tt=_tile_rows(seq, 256))
    gdn_out_p, ssm_p = _gdn_prompt(proj_all, ab_all, conv_w[0], alog_pad, dtb_pad, gn, batch=batch, seq=seq,
                                   hg=4, tt=_tile_rows(seq, 512))
    y_p = _dense_tail(xp, p_prompt[0].reshape(mp, PLE_DIM), proj_all, pool_out_p, gdn_out_p, wts)
    seq_tail = lambda n, c0, c1: jnp.stack([proj_all[(b + 1) * seq - n:(b + 1) * seq, c0:c1] for b in range(batch)])
    pool_p = seq_tail(POOL_BUF, C_POOL, C_END)[None]
    conv_p = seq_tail(CONV_BUF, C_Q, C_Z)[None]

    proj_s, ab_s = proj_all[mp:], ab_all[mp:]
    pool_out_s = _pool_sample(proj_s, jnp.swapaxes(state_pool[0], 0, 1), pool_w_b, pool_scale)
    prep = _gdn_sample_prep(proj_s, ab_s, jnp.swapaxes(state_conv[0], 0, 1), conv_w[0], alog_pad, dtb_pad)
    to_heads = lambda a: a.reshape(nb, HEADS, HEAD_DIM)
    gdn_out_s, ssm_s = _gdn_sample_step(*[to_heads(a) for a in prep], to_heads(proj_s[:, C_Z:C_POOL]), gn,
                                        state_ssm, bb=4)
    y_s = _dense_tail(xs, p_sample[0].reshape(nb, PLE_DIM), proj_s, pool_out_s, gdn_out_s.reshape(nb, GDN_DIM), wts)
    pool_s = jnp.concatenate([state_pool[0][:, 1:], proj_s[:, None, C_POOL:C_END]], axis=1)[None]
    conv_s = jnp.concatenate([state_conv[0][:, 1:], proj_s[:, None, C_Q:C_Z]], axis=1)[None]

    return (y_p.reshape(batch, seq, D_MODEL), y_s.reshape(nb, 1, D_MODEL), pool_p, conv_p, ssm_p,
            pool_s, conv_s, ssm_s)
```

```python
import functools

import jax
import jax.numpy as jnp
from jax import lax
from jax.experimental import pallas as pl
from jax.experimental.pallas import tpu as pltpu

F32 = jnp.float32
BF16 = jnp.bfloat16

D_MODEL = 2048
PAST_LEN = 16384
POOL_WINDOWS = (2, 4, 8, 16)
POOL_GROUP_DIM = D_MODEL // 8
POOL_DIM = len(POOL_WINDOWS) * POOL_GROUP_DIM
POOL_BUF = max(POOL_WINDOWS) - 1
HEAD_DIM = 128
HEADS = D_MODEL // HEAD_DIM
GDN_DIM = HEADS * HEAD_DIM
QKV_DIM = 3 * GDN_DIM
CONV_WIDTH = 4
CONV_BUF = CONV_WIDTH - 1
D_FF = -(-8 * D_MODEL // (3 * 256)) * 256
PLE_DIM = 256
EPS = 1e-6

IN_QKV = POOL_DIM
IN_Z = IN_QKV + QKV_DIM
IN_AB = IN_Z + GDN_DIM
IN_GATES = IN_AB + 2 * HEADS
C_Q = 0
C_K = C_Q + GDN_DIM
C_V = C_K + GDN_DIM
C_Z = C_V + GDN_DIM
C_POOL = C_Z + GDN_DIM
C_END = C_POOL + POOL_DIM

LANES = 128
SUBLANES = 8
VMEM_LIMIT_BYTES = 56 * 1024 * 1024
CHUNK = 128


def _cparams(*sem):
    return pltpu.CompilerParams(dimension_semantics=sem, vmem_limit_bytes=VMEM_LIMIT_BYTES)


def _rms(x, gain):
    return x * lax.rsqrt(jnp.mean(x * x, axis=-1, keepdims=True) + EPS) * gain


def _dot(a, b):
    return jnp.dot(a, b, preferred_element_type=F32)


def _dot_nt(a, b):
    return lax.dot_general(a, b, (((1,), (1,)), ((), ())), preferred_element_type=F32)


def _silu(x):
    return x * jax.nn.sigmoid(x)


def _norm_in_body(x_ref, gain_ref, wab_ref, h_ref, ab_ref):
    h = _rms(x_ref[...], gain_ref[...]).astype(BF16)
    h_ref[...] = h
    ab_ref[...] = _dot_nt(h, wab_ref[...].astype(BF16))


def _norm_in(x, gain, w_in_t, *, tm):
    m = x.shape[0]
    return pl.pallas_call(
        _norm_in_body,
        grid=(m // tm,),
        in_specs=[
            pl.BlockSpec((tm, D_MODEL), lambda i: (i, 0)),
            pl.BlockSpec((1, D_MODEL), lambda i: (0, 0)),
            pl.BlockSpec((LANES, D_MODEL), lambda i: (IN_AB // LANES, 0)),
        ],
        out_specs=[
            pl.BlockSpec((tm, D_MODEL), lambda i: (i, 0)),
            pl.BlockSpec((tm, LANES), lambda i: (i, 0)),
        ],
        out_shape=[jax.ShapeDtypeStruct((m, D_MODEL), BF16), jax.ShapeDtypeStruct((m, LANES), F32)],
        compiler_params=_cparams("parallel"),
        name="norm_in",
    )(x, gain, w_in_t)


def _inproj_body(hp_ref, hs_ref, w_ref, op_ref, os_ref, wb_ref, *, gates, n_prompt_tiles):
    i = pl.program_id(1)

    @pl.when(i == 0)
    def _():
        wb_ref[...] = w_ref[...].astype(BF16)

    def emit(h_ref, o_ref):
        acc = _dot_nt(h_ref[...], wb_ref[...])
        o_ref[...] = jax.nn.sigmoid(acc).astype(o_ref.dtype) if gates else acc

    @pl.when(i < n_prompt_tiles)
    def _():
        emit(hp_ref, op_ref)

    @pl.when(i == n_prompt_tiles)
    def _():
        emit(hs_ref, os_ref)


def _inproj(h_prompt, h_sample, w_in_t, *, tm, tn, gates):
    mp, ms = h_prompt.shape[0], h_sample.shape[0]
    n_p = mp // tm
    if gates:
        n_out, out_dtype = 2 * D_MODEL, BF16
        row_start = lambda j: IN_GATES + j * tn
    else:
        n_out, out_dtype = C_END, F32
        n_mid = (QKV_DIM + GDN_DIM) // tn
        row_start = lambda j: jnp.where(j < n_mid, IN_QKV + j * tn, (j - n_mid) * tn)
    return pl.pallas_call(
        functools.partial(_inproj_body, gates=gates, n_prompt_tiles=n_p),
        grid=(n_out // tn, n_p + 1),
        in_specs=[
            pl.BlockSpec((tm, D_MODEL), lambda j, i: (jnp.minimum(i, n_p - 1), 0)),
            pl.BlockSpec((ms, D_MODEL), lambda j, i: (0, 0)),
            pl.BlockSpec((pl.Element(tn), pl.Element(D_MODEL)),
                         lambda j, i: (pl.multiple_of(row_start(j), SUBLANES), 0)),
        ],
        out_specs=[pl.BlockSpec((tm, tn), lambda j, i: (jnp.minimum(i, n_p - 1), j)),
                   pl.BlockSpec((ms, tn), lambda j, i: (0, j))],
        out_shape=[jax.ShapeDtypeStruct((mp, n_out), out_dtype), jax.ShapeDtypeStruct((ms, n_out), out_dtype)],
        scratch_shapes=[pltpu.VMEM((tn, D_MODEL), BF16)],
        compiler_params=_cparams("arbitrary", "arbitrary"),
        name="inproj_gates" if gates else "inproj",
    )(h_prompt, h_sample, w_in_t)


def _pool_group_out(mean, tok, pw_ref, scale_ref, gi):
    cols = slice(gi * POOL_GROUP_DIM, (gi + 1) * POOL_GROUP_DIM)
    y = _dot((mean - tok).astype(BF16), pw_ref[gi])
    return (y * scale_ref[:, cols]).astype(BF16)


def _pool_prompt_body(u_ref, pw_ref, scale_ref, o_ref, ext_ref, *, tt, start_pos):
    t = pl.program_id(1)
    halo = POOL_BUF + 1

    @pl.when(t == 0)
    def _():
        ext_ref[0:halo, :] = jnp.zeros((halo, POOL_DIM), F32)

    ext_ref[halo:halo + tt, :] = u_ref[...]
    pos = start_pos + t * tt + lax.broadcasted_iota(jnp.int32, (tt, 1), 0)
    for gi, w in enumerate(POOL_WINDOWS):
        cols = slice(gi * POOL_GROUP_DIM, (gi + 1) * POOL_GROUP_DIM)
        tok = ext_ref[halo:halo + tt, cols]
        tot = tok
        for i in range(1, w):
            tot = tot + ext_ref[pl.ds(halo - i, tt), cols]
        cnt = jnp.minimum(pos + 1, w).astype(F32)
        o_ref[:, cols] = _pool_group_out(tot / cnt, tok, pw_ref, scale_ref, gi)
    ext_ref[0:halo, :] = ext_ref[tt:tt + halo, :]


def _pool_prompt(proj, pool_w, pool_scale, *, batch, seq, tt):
    nt = seq // tt
    body = functools.partial(_pool_prompt_body, tt=tt, start_pos=0)
    return pl.pallas_call(
        body,
        grid=(batch, nt),
        in_specs=[
            pl.BlockSpec((tt, POOL_DIM), lambda b, t: (b * nt + t, C_POOL // POOL_DIM)),
            pl.BlockSpec((len(POOL_WINDOWS), POOL_GROUP_DIM, POOL_GROUP_DIM), lambda b, t: (0, 0, 0)),
            pl.BlockSpec((1, POOL_DIM), lambda b, t: (0, 0)),
        ],
        out_specs=pl.BlockSpec((tt, POOL_DIM), lambda b, t: (b * nt + t, 0)),
        out_shape=jax.ShapeDtypeStruct((batch * seq, POOL_DIM), BF16),
        scratch_shapes=[pltpu.VMEM((POOL_BUF + 1 + tt, POOL_DIM), F32)],
        compiler_params=_cparams("parallel", "arbitrary"),
        name="pool_prompt",
    )(proj, pool_w, pool_scale)


def _pool_sample_body(u_ref, st_ref, pw_ref, scale_ref, o_ref, *, start_pos):
    for gi, w in enumerate(POOL_WINDOWS):
        cols = slice(gi * POOL_GROUP_DIM, (gi + 1) * POOL_GROUP_DIM)
        tok = u_ref[:, cols]
        tot = tok
        for i in range(1, w):
            tot = tot + st_ref[POOL_BUF - i, :, cols]
        cnt = float(min(start_pos + 1, w))
        o_ref[:, cols] = _pool_group_out(tot / cnt, tok, pw_ref, scale_ref, gi)


def _pool_sample(proj, state_t, pool_w, pool_scale):
    nb = proj.shape[0]
    body = functools.partial(_pool_sample_body, start_pos=PAST_LEN)
    return pl.pallas_call(
        body,
        grid=(1,),
        in_specs=[
            pl.BlockSpec((nb, POOL_DIM), lambda i: (0, C_POOL // POOL_DIM)),
            pl.BlockSpec((POOL_BUF, nb, POOL_DIM), lambda i: (0, 0, 0)),
            pl.BlockSpec((len(POOL_WINDOWS), POOL_GROUP_DIM, POOL_GROUP_DIM), lambda i: (0, 0, 0)),
            pl.BlockSpec((1, POOL_DIM), lambda i: (0, 0)),
        ],
        out_specs=pl.BlockSpec((nb, POOL_DIM), lambda i: (0, 0)),
        out_shape=jax.ShapeDtypeStruct((nb, POOL_DIM), BF16),
        compiler_params=_cparams("arbitrary"),
        name="pool_sample",
    )(proj, state_t, pool_w, pool_scale)


def _split3(x):
    hi = x.astype(BF16)
    r1 = x - hi.astype(F32)
    mid = r1.astype(BF16)
    lo = (r1 - mid.astype(F32)).astype(BF16)
    return hi, mid, lo


def _split2(x):
    hi = x.astype(BF16)
    return hi, (x - hi.astype(F32)).astype(BF16)


def _softplus(x):
    return jnp.maximum(x, 0.0) + jnp.log1p(jnp.exp(-jnp.abs(x)))


def _unit_lower_inverse_minus_identity(mats):
    eye = jnp.where(lax.broadcasted_iota(jnp.int32, (CHUNK, CHUNK), 0)
                    == lax.broadcasted_iota(jnp.int32, (CHUNK, CHUNK), 1), 1.0, 0.0)
    levels = CHUNK.bit_length() - 1
    qbs = [(-a).astype(BF16) for a in mats]
    rs = [eye - a for a in mats]
    qs = [_dot(qb, qb) for qb in qbs]
    for j in range(1, levels):
        qbs = [q.astype(BF16) for q in qs]
        rbs = [r.astype(BF16) for r in rs]
        if j < levels - 1:
            ps = [_dot(qb, jnp.concatenate([qb, rb], axis=1)) for qb, rb in zip(qbs, rbs)]
            rs = [r + p[:, CHUNK:] for r, p in zip(rs, ps)]
            qs = [p[:, :CHUNK] for p in ps]
        else:
            rs = [r + _dot(qb, rb) for r, qb, rb in zip(rs, qbs, rbs)]
    return [r - eye for r in rs]


def _gdn_prompt_body(q_ref, k_ref, v_ref, z_ref, ab_ref, cwq_ref, cwk_ref, cwv_ref, alog_ref, dtb_ref, gn_ref,
                     o_ref, ssm_ref, ext_ref, s_ref, gct_ref, *, hg, tt):
    hgi = pl.program_id(1)
    t = pl.program_id(2)
    w = hg * HEAD_DIM
    halo = SUBLANES

    @pl.when(t == 0)
    def _():
        ext_ref[0:halo, :] = jnp.zeros((halo, 3 * w), F32)
        s_ref[...] = jnp.zeros_like(s_ref)

    ext_ref[halo:halo + tt, 0:w] = q_ref[...]
    ext_ref[halo:halo + tt, w:2 * w] = k_ref[...]
    ext_ref[halo:halo + tt, 2 * w:3 * w] = v_ref[...]

    def conv(part, cw_ref):
        cols = slice(part * w, (part + 1) * w)
        acc = ext_ref[halo:halo + tt, cols] * cw_ref[CONV_BUF:CONV_WIDTH, :]
        for i in range(CONV_BUF):
            acc = acc + ext_ref[pl.ds(halo - CONV_BUF + i, tt), cols] * cw_ref[i:i + 1, :]
        return _silu(acc)

    qc = conv(0, cwq_ref)
    kc = conv(1, cwk_ref)
    vc = conv(2, cwv_ref)
    ext_ref[0:halo, :] = ext_ref[tt:tt + halo, :]

    ab = ab_ref[...]
    g_all = -jnp.exp(alog_ref[...]) * _softplus(ab + dtb_ref[...])
    beta_all = jax.nn.sigmoid(ab)
    ri = lax.broadcasted_iota(jnp.int32, (tt, tt), 0)
    ci = lax.broadcasted_iota(jnp.int32, (tt, tt), 1)
    shift = CHUNK.bit_length() - 1
    tri = jnp.where(((ri >> shift) == (ci >> shift)) & (ci <= ri), 1.0, 0.0).astype(BF16)
    g_hi, g_mid, g_lo = _split3(g_all)
    gc_all = _dot(tri, g_hi) + _dot(tri, g_mid) + _dot(tri, g_lo)
    gct_ref[...] = gc_all.T

    lane = lax.broadcasted_iota(jnp.int32, (tt, LANES), 1)
    r128 = lax.broadcasted_iota(jnp.int32, (CHUNK, CHUNK), 0)
    c128 = lax.broadcasted_iota(jnp.int32, (CHUNK, CHUNK), 1)
    sub8 = lax.broadcasted_iota(jnp.int32, (SUBLANES, CHUNK), 0)
    n_chunks = tt // CHUNK
    chains = [(hh, c) for hh in range(hg) for c in range(n_chunks)]
    gcol_heads, bcol_heads = [], []
    for hh in range(hg):
        head = hgi * hg + hh
        gcol_heads.append(jnp.sum(jnp.where(lane == head, gc_all, 0.0), axis=-1, keepdims=True))
        bcol_heads.append(jnp.sum(jnp.where(lane == head + HEADS, beta_all, 0.0), axis=-1, keepdims=True))

    qn_l, kn_l, kb_l, rhs_l, eg_l, gcol_l, dec_l = [], [], [], [], [], [], []
    for hh, c in chains:
        head = hgi * hg + hh
        rows = slice(c * CHUNK, (c + 1) * CHUNK)
        lanes = slice(hh * HEAD_DIM, (hh + 1) * HEAD_DIM)
        qh, kh, vh = qc[rows, lanes], kc[rows, lanes], vc[rows, lanes]
        qn = qh * lax.rsqrt(jnp.sum(qh * qh, axis=-1, keepdims=True) + EPS) * (HEAD_DIM ** -0.5)
        kn = kh * lax.rsqrt(jnp.sum(kh * kh, axis=-1, keepdims=True) + EPS)
        gcol = gcol_heads[hh][rows]
        bcol = bcol_heads[hh][rows]
        gblk = gct_ref[pl.ds(pl.multiple_of((head >> 3) * SUBLANES, SUBLANES), SUBLANES), rows]
        grow = jnp.sum(jnp.where(sub8 == (head & (SUBLANES - 1)), gblk, 0.0), axis=0, keepdims=True)
        eg = jnp.exp(gcol)
        kb = kn * bcol
        qn_l.append(qn)
        kn_l.append(kn)
        kb_l.append(kb)
        eg_l.append(eg)
        gcol_l.append(gcol)
        dec_l.append(jnp.exp(jnp.minimum(gcol - grow, 0.0)))
        rhs_l.append(jnp.concatenate([vh * bcol, kb * eg], axis=1))

    a_l, qk_l = [], []
    for qn, kn, kb, decay in zip(qn_l, kn_l, kb_l, dec_l):
        knb = kn.astype(BF16)
        a_l.append(_dot_nt(kb.astype(BF16), knb) * jnp.where(c128 < r128, decay, 0.0))
        qk_l.append(_dot_nt(qn.astype(BF16), knb) * jnp.where(c128 <= r128, decay, 0.0))
    tinv_l = [x.astype(BF16) for x in _unit_lower_inverse_minus_identity(a_l)]
    sol_l = [rhs + _dot(tinv, rhs.astype(BF16)) for tinv, rhs in zip(tinv_l, rhs_l)]
    res_l = []
    for a, sol, rhs in zip(a_l, sol_l, rhs_l):
        a_hi, a_lo = _split2(a)
        s_hi, s_lo = _split2(sol)
        a_sol = _dot(jnp.concatenate([a_hi, a_lo, a_hi], axis=1), jnp.concatenate([s_hi, s_hi, s_lo], axis=0))
        res_l.append((rhs - sol) - a_sol)
    sol_l = [(sol + res + _dot(tinv, res.astype(BF16))).astype(BF16) for sol, res, tinv in zip(sol_l, res_l, tinv_l)]
    lhs_l, o_add_l, s_add_l, glast_l = [], [], [], []
    for qn, kn, eg, gcol, qk, sol_b in zip(qn_l, kn_l, eg_l, gcol_l, qk_l, sol_l):
        glast = gcol[CHUNK - 1:CHUNK, :]
        ktail = kn * jnp.exp(glast - gcol)
        kt_sol = _dot(ktail.T.astype(BF16), sol_b)
        qk_sol = _dot(qk.astype(BF16), sol_b)
        lhs_l.append(jnp.concatenate([qn * eg - qk_sol[:, HEAD_DIM:], kt_sol[:, HEAD_DIM:]], axis=0).astype(BF16))
        o_add_l.append(qk_sol[:, :HEAD_DIM])
        s_add_l.append(kt_sol[:, :HEAD_DIM])
        glast_l.append(jnp.exp(glast))
    states = [s_ref[hh] for hh in range(hg)]
    for c in range(n_chunks):
        for hh in range(hg):
            i = hh * n_chunks + c
            rows = slice(c * CHUNK, (c + 1) * CHUNK)
            lanes = slice(hh * HEAD_DIM, (hh + 1) * HEAD_DIM)
            on_s = _dot(lhs_l[i], states[hh].astype(BF16))
            o = on_s[:CHUNK] + o_add_l[i]
            states[hh] = states[hh] * glast_l[i] - on_s[CHUNK:] + s_add_l[i]
            o_ref[rows, lanes] = (_rms(o, gn_ref[...]) * _silu(z_ref[rows, lanes])).astype(BF16)
    for hh in range(hg):
        s_ref[hh] = states[hh]

    @pl.when(t == pl.num_programs(2) - 1)
    def _():
        ssm_ref[0, 0] = s_ref[...]


def _gdn_prompt(proj, ab, conv_w, alog_pad, dtb_pad, gdn_norm, *, batch, seq, hg, tt):
    nt = seq // tt
    w = hg * HEAD_DIM
    body = functools.partial(_gdn_prompt_body, hg=hg, tt=tt)

    def col_spec(col0):
        return pl.BlockSpec((tt, w), lambda b, h, t: (b * nt + t, col0 // w + h))

    def cw_spec(part):
        return pl.BlockSpec((CONV_WIDTH, w), lambda b, h, t: (0, part * GDN_DIM // w + h))

    small = pl.BlockSpec((1, LANES), lambda b, h, t: (0, 0))
    return pl.pallas_call(
        body,
        grid=(batch, HEADS // hg, nt),
        in_specs=[col_spec(C_Q), col_spec(C_K), col_spec(C_V), col_spec(C_Z),
                  pl.BlockSpec((tt, LANES), lambda b, h, t: (b * nt + t, 0)),
                  cw_spec(0), cw_spec(1), cw_spec(2), small, small, small],
        out_specs=[
            pl.BlockSpec((tt, w), lambda b, h, t: (b * nt + t, h)),
            pl.BlockSpec((1, 1, hg, HEAD_DIM, HEAD_DIM), lambda b, h, t: (0, b, h, 0, 0)),
        ],
        out_shape=[
            jax.ShapeDtypeStruct((batch * seq, GDN_DIM), BF16),
            jax.ShapeDtypeStruct((1, batch, HEADS, HEAD_DIM, HEAD_DIM), F32),
        ],
        scratch_shapes=[
            pltpu.VMEM((SUBLANES + tt, 3 * w), F32),
            pltpu.VMEM((hg, HEAD_DIM, HEAD_DIM), F32),
            pltpu.VMEM((LANES, tt), F32),
        ],
        compiler_params=_cparams("parallel", "parallel", "arbitrary"),
        name="gdn_prompt",
    )(proj, proj, proj, proj, ab, conv_w, conv_w, conv_w, alog_pad, dtb_pad, gdn_norm)


def _gdn_sample_prep_body(q_ref, k_ref, v_ref, ab_ref, st_ref, cw_ref, alog_ref, dtb_ref,
                          qn_ref, kn_ref, vc_ref, eg_ref, beta_ref, qk_ref):
    def conv(part, u_ref):
        cols = slice(part * GDN_DIM, (part + 1) * GDN_DIM)
        acc = u_ref[...] * cw_ref[CONV_BUF:CONV_WIDTH, cols]
        for i in range(CONV_BUF):
            acc = acc + st_ref[i, :, cols] * cw_ref[i:i + 1, cols]
        return _silu(acc)

    qc = conv(0, q_ref)
    kc = conv(1, k_ref)
    vc_ref[...] = conv(2, v_ref)
    ab = ab_ref[...]
    g_all = -jnp.exp(alog_ref[...]) * _softplus(ab + dtb_ref[...])
    eg_all = jnp.exp(g_all)
    beta_all = jax.nn.sigmoid(ab)
    for h in range(HEADS):
        lanes = slice(h * HEAD_DIM, (h + 1) * HEAD_DIM)
        qh, kh = qc[:, lanes], kc[:, lanes]
        qn = qh * lax.rsqrt(jnp.sum(qh * qh, axis=-1, keepdims=True) + EPS) * (HEAD_DIM ** -0.5)
        kn = kh * lax.rsqrt(jnp.sum(kh * kh, axis=-1, keepdims=True) + EPS)
        qn_ref[:, lanes] = qn
        kn_ref[:, lanes] = kn
        shape = qn.shape
        qk_ref[:, lanes] = jnp.broadcast_to(jnp.sum(qn * kn, axis=-1, keepdims=True), shape)
        eg_ref[:, lanes] = jnp.broadcast_to(eg_all[:, h:h + 1], shape)
        beta_ref[:, lanes] = jnp.broadcast_to(beta_all[:, HEADS + h:HEADS + h + 1], shape)


def _gdn_sample_prep(proj, ab, conv_state_t, conv_w, alog_pad, dtb_pad):
    nb = proj.shape[0]

    def col_spec(col0):
        return pl.BlockSpec((nb, GDN_DIM), lambda i: (0, col0 // GDN_DIM))

    small = pl.BlockSpec((1, LANES), lambda i: (0, 0))
    full = pl.BlockSpec((nb, GDN_DIM), lambda i: (0, 0))
    return pl.pallas_call(
        _gdn_sample_prep_body,
        grid=(1,),
        in_specs=[col_spec(C_Q), col_spec(C_K), col_spec(C_V),
                  pl.BlockSpec((nb, LANES), lambda i: (0, 0)),
                  pl.BlockSpec((CONV_BUF, nb, QKV_DIM), lambda i: (0, 0, 0)),
                  pl.BlockSpec((CONV_WIDTH, QKV_DIM), lambda i: (0, 0)),
                  small, small],
        out_specs=[full] * 6,
        out_shape=[jax.ShapeDtypeStruct((nb, GDN_DIM), F32)] * 6,
        compiler_params=_cparams("arbitrary"),
        name="gdn_sample_prep",
    )(proj, proj, proj, ab, conv_state_t, conv_w, alog_pad, dtb_pad)


def _gdn_sample_step_body(qn_ref, kn_ref, vc_ref, eg_ref, beta_ref, qk_ref, z_ref, gn_ref, s_ref,
                          o_ref, snew_ref, *, bb):
    pad = jnp.zeros((LANES - HEADS, HEAD_DIM), F32)
    for b in range(bb):
        k16, q16 = kn_ref[b], qn_ref[b]
        cols = jnp.concatenate([k16, pad], axis=0).T
        kq = jnp.concatenate([k16, q16], axis=0).astype(BF16)
        outs = []
        for h in range(HEADS):
            kcol = cols[:, h:h + 1]
            s = s_ref[0, b, h]
            kq_s = _dot(kq, s.astype(BF16))
            k_s = kq_s[h:h + 1]
            q_s = kq_s[HEADS + h:HEADS + h + 1]
            eg = eg_ref[b, h:h + 1, :]
            v_new = beta_ref[b, h:h + 1, :] * (vc_ref[b, h:h + 1, :] - eg * k_s)
            snew_ref[0, b, h] = eg * s + kcol * v_new
            outs.append(eg * q_s + qk_ref[b, h:h + 1, :] * v_new)
        o = jnp.concatenate(outs, axis=0)
        o_ref[b] = (_rms(o, gn_ref[...]) * _silu(z_ref[b])).astype(BF16)


def _gdn_sample_step(qn, kn, vc, eg, beta, qk, z, gdn_norm, state_ssm, *, bb):
    nb = qn.shape[0]
    body = functools.partial(_gdn_sample_step_body, bb=bb)
    vec = pl.BlockSpec((bb, HEADS, HEAD_DIM), lambda i: (i, 0, 0))
    st = pl.BlockSpec((1, bb, HEADS, HEAD_DIM, HEAD_DIM), lambda i: (0, i, 0, 0, 0))
    return pl.pallas_call(
        body,
        grid=(nb // bb,),
        in_specs=[vec] * 7 + [pl.BlockSpec((1, LANES), lambda i: (0, 0)), st],
        out_specs=[vec, st],
        out_shape=[jax.ShapeDtypeStruct((nb, HEADS, HEAD_DIM), BF16),
                   jax.ShapeDtypeStruct(state_ssm.shape, F32)],
        compiler_params=_cparams("parallel"),
        name="gdn_sample_step",
    )(qn, kn, vc, eg, beta, qk, z, gdn_norm, state_ssm)


def _merge_body(pool_ref, gdn_ref, wpu_ref, wgu_ref, gp_ref, gg_ref, o_ref):
    o_ref[...] = (gp_ref[...].astype(F32) * _dot(pool_ref[...], wpu_ref[...])
                  + gg_ref[...].astype(F32) * _dot(gdn_ref[...], wgu_ref[...])).astype(BF16)


def _merge(gates, pool_out, gdn_out, w_pool_up, w_gdn_up, *, tm, tn=512):
    m = pool_out.shape[0]
    nj = D_MODEL // tn
    return pl.pallas_call(
        _merge_body,
        grid=(m // tm, nj),
        in_specs=[
            pl.BlockSpec((tm, POOL_DIM), lambda i, j: (i, 0)),
            pl.BlockSpec((tm, GDN_DIM), lambda i, j: (i, 0)),
            pl.BlockSpec((POOL_DIM, tn), lambda i, j: (0, j)),
            pl.BlockSpec((GDN_DIM, tn), lambda i, j: (0, j)),
            pl.BlockSpec((tm, tn), lambda i, j: (i, j)),
            pl.BlockSpec((tm, tn), lambda i, j: (i, nj + j)),
        ],
        out_specs=pl.BlockSpec((tm, tn), lambda i, j: (i, j)),
        out_shape=jax.ShapeDtypeStruct((m, D_MODEL), BF16),
        compiler_params=_cparams("parallel", "parallel"),
        name="merge",
    )(pool_out, gdn_out, w_pool_up, w_gdn_up, gates, gates)


def _outproj_body(a_ref, w_ref, x_ref, o_ref):
    o_ref[...] = x_ref[...] + _dot(a_ref[...], w_ref[...])


def _outproj(merged, x, w_o, *, tm, tn=512):
    m = x.shape[0]
    return pl.pallas_call(
        _outproj_body,
        grid=(m // tm, D_MODEL // tn),
        in_specs=[
            pl.BlockSpec((tm, D_MODEL), lambda i, j: (i, 0)),
            pl.BlockSpec((D_MODEL, tn), lambda i, j: (0, j)),
            pl.BlockSpec((tm, tn), lambda i, j: (i, j)),
        ],
        out_specs=pl.BlockSpec((tm, tn), lambda i, j: (i, j)),
        out_shape=jax.ShapeDtypeStruct((m, D_MODEL), F32),
        compiler_params=_cparams("parallel", "parallel"),
        name="outproj",
    )(merged, w_o, x)


def _ffn_body(x_ref, gain_ref, wg_ref, wu_ref, wd_ref, o_ref, h_ref):
    f = pl.program_id(1)

    @pl.when(f == 0)
    def _():
        x = x_ref[...]
        h_ref[...] = _rms(x, gain_ref[...]).astype(BF16)
        o_ref[...] = x

    h = h_ref[...]
    act = (_silu(_dot(h, wg_ref[...].astype(BF16))) * _dot(h, wu_ref[...].astype(BF16))).astype(BF16)
    o_ref[...] += _dot(act, wd_ref[...].astype(BF16))


def _ffn(x, gain, w_gate_up, w_down, *, tm, tf):
    m = x.shape[0]
    nf = D_FF // tf
    return pl.pallas_call(
        _ffn_body,
        grid=(m // tm, nf),
        in_specs=[
            pl.BlockSpec((tm, D_MODEL), lambda i, f: (i, 0)),
            pl.BlockSpec((1, D_MODEL), lambda i, f: (0, 0)),
            pl.BlockSpec((D_MODEL, tf), lambda i, f: (0, f)),
            pl.BlockSpec((D_MODEL, tf), lambda i, f: (0, nf + f)),
            pl.BlockSpec((tf, D_MODEL), lambda i, f: (f, 0)),
        ],
        out_specs=pl.BlockSpec((tm, D_MODEL), lambda i, f: (i, 0)),
        out_shape=jax.ShapeDtypeStruct((m, D_MODEL), F32),
        scratch_shapes=[pltpu.VMEM((tm, D_MODEL), BF16)],
        compiler_params=_cparams("parallel", "arbitrary"),
        name="ffn",
    )(x, gain, w_gate_up, w_gate_up, w_down)


def _final_body(x_ref, p_ref, wple_ref, wpg_ref, gain_ref, o_ref):
    x = x_ref[...]
    gate = jax.nn.sigmoid(_dot(x.astype(BF16), wpg_ref[...]))
    emb = _dot(p_ref[...].astype(BF16), wple_ref[...])
    o_ref[...] = _rms(x + emb * gate, gain_ref[...])


def _final(x, p, w_ple, w_ple_gate, gain, *, tm):
    m = x.shape[0]
    return pl.pallas_call(
        _final_body,
        grid=(m // tm,),
        in_specs=[
            pl.BlockSpec((tm, D_MODEL), lambda i: (i, 0)),
            pl.BlockSpec((tm, PLE_DIM), lambda i: (i, 0)),
            pl.BlockSpec((PLE_DIM, D_MODEL), lambda i: (0, 0)),
            pl.BlockSpec((D_MODEL, D_MODEL), lambda i: (0, 0)),
            pl.BlockSpec((1, D_MODEL), lambda i: (0, 0)),
        ],
        out_specs=pl.BlockSpec((tm, D_MODEL), lambda i: (i, 0)),
        out_shape=jax.ShapeDtypeStruct((m, D_MODEL), F32),
        compiler_params=_cparams("parallel"),
        name="final",
    )(x, p, w_ple, w_ple_gate, gain)


def _tile_rows(m, want):
    return want if m % want == 0 else m


def _dense_tail(x, p, gates, pool_out, gdn_out, wts):
    m = x.shape[0]
    tm = _tile_rows(m, 1024)
    merged = _merge(gates, pool_out, gdn_out, wts["w_pool_up"], wts["w_gdn_up"], tm=tm)
    x = _outproj(merged, x, wts["w_o"], tm=tm)
    x = _ffn(x, wts["norm_ffn"], wts["w_gate_up"], wts["w_down"], tm=tm, tf=256)
    return _final(x, p, wts["w_ple"], wts["w_ple_gate"], wts["norm_final"], tm=_tile_rows(m, 512))


def kernel(x_prompt, x_sample, p_prompt, p_sample, state_pool, state_conv, state_ssm, norm_mix, w_in, pool_w,
           pool_scale, conv_w, a_log, dt_bias, gdn_norm, w_pool_up, w_gdn_up, w_o, norm_ffn, w_gate_up, w_down,
           w_ple, w_ple_gate, norm_final):
    batch, seq, _ = x_prompt.shape
    nb = x_sample.shape[0]
    w_in_t = jnp.swapaxes(w_in[0], 0, 1)
    wts = dict(
        w_pool_up=w_pool_up[0].astype(BF16), w_gdn_up=w_gdn_up[0].astype(BF16), w_o=w_o[0].astype(BF16),
        norm_ffn=norm_ffn, w_gate_up=w_gate_up[0], w_down=w_down[0],
        w_ple=w_ple[0].astype(BF16), w_ple_gate=w_ple_gate[0].astype(BF16), norm_final=norm_final[None, :],
    )
    pool_w_b = pool_w[0].astype(BF16)
    lane_pad = ((0, 0), (0, LANES - HEADS))
    alog_pad = jnp.pad(a_log, lane_pad)
    dtb_pad = jnp.pad(dt_bias, lane_pad)
    gn = gdn_norm

    mp = batch * seq
    xp = x_prompt.reshape(mp, D_MODEL)
    xs = x_sample.reshape(nb, D_MODEL)
    h_p, ab_p = _norm_in(xp, norm_mix, w_in_t, tm=_tile_rows(mp, 512))
    h_s, ab_s = _norm_in(xs, norm_mix, w_in_t, tm=nb)
    tm_in = _tile_rows(mp, 1024)
    proj_p, proj_s = _inproj(h_p, h_s, w_in_t, tm=tm_in, tn=1024, gates=False)
    gates_p, gates_s = _inproj(h_p, h_s, w_in_t, tm=tm_in, tn=1024, gates=True)

    pool_out_p = _pool_prompt(proj_p, pool_w_b, pool_scale, batch=batch, seq=seq, tt=_tile_rows(seq, 512))
    gdn_out_p, ssm_p = _gdn_prompt(proj_p, ab_p, conv_w[0], alog_pad, dtb_pad, gn, batch=batch, seq=seq,
                                   hg=4, tt=_tile_rows(seq, 512))
    y_p = _dense_tail(xp, p_prompt[0].reshape(mp, PLE_DIM), gates_p, pool_out_p, gdn_out_p, wts)
    proj_p3 = proj_p.reshape(batch, seq, C_END)
    pool_p = proj_p3[:, seq - POOL_BUF:, C_POOL:C_END][None]
    conv_p = proj_p3[:, seq - CONV_BUF:, C_Q:C_Z][None]

    pool_out_s = _pool_sample(proj_s, jnp.swapaxes(state_pool[0], 0, 1), pool_w_b, pool_scale)
    prep = _gdn_sample_prep(proj_s, ab_s, jnp.swapaxes(state_conv[0], 0, 1), conv_w[0], alog_pad, dtb_pad)
    to_heads = lambda a: a.reshape(nb, HEADS, HEAD_DIM)
    gdn_out_s, ssm_s = _gdn_sample_step(*[to_heads(a) for a in prep], to_heads(proj_s[:, C_Z:C_POOL]), gn,
                                        state_ssm, bb=4)
    y_s = _dense_tail(xs, p_sample[0].reshape(nb, PLE_DIM), gates_s, pool_out_s, gdn_out_s.reshape(nb, GDN_DIM), wts)
    pool_s = jnp.concatenate([state_pool[0][:, 1:], proj_s[:, None, C_POOL:C_END]], axis=1)[None]
    conv_s = jnp.concatenate([state_conv[0][:, 1:], proj_s[:, None, C_Q:C_Z]], axis=1)[None]

    return (y_p.reshape(batch, seq, D_MODEL), y_s.reshape(nb, 1, D_MODEL), pool_p, conv_p, ssm_p,
            pool_s, conv_s, ssm_s)
```

```python
import functools

import jax
import jax.numpy as jnp
from jax import lax
from jax.experimental import pallas as pl
from jax.experimental.pallas import tpu as pltpu

F32 = jnp.float32
BF16 = jnp.bfloat16

D_MODEL = 2048
PAST_LEN = 16384
POOL_WINDOWS = (2, 4, 8, 16)
POOL_GROUP_DIM = D_MODEL // 8
POOL_DIM = len(POOL_WINDOWS) * POOL_GROUP_DIM
POOL_BUF = max(POOL_WINDOWS) - 1
HEAD_DIM = 128
HEADS = D_MODEL // HEAD_DIM
GDN_DIM = HEADS * HEAD_DIM
QKV_DIM = 3 * GDN_DIM
CONV_WIDTH = 4
CONV_BUF = CONV_WIDTH - 1
D_FF = -(-8 * D_MODEL // (3 * 256)) * 256
PLE_DIM = 256
EPS = 1e-6

IN_QKV = POOL_DIM
IN_Z = IN_QKV + QKV_DIM
IN_AB = IN_Z + GDN_DIM
IN_GATES = IN_AB + 2 * HEADS
C_Q = 0
C_K = C_Q + GDN_DIM
C_V = C_K + GDN_DIM
C_Z = C_V + GDN_DIM
C_POOL = C_Z + GDN_DIM
C_END = C_POOL + POOL_DIM
C_GP = C_END
C_GG = C_GP + D_MODEL
C_ALL = C_GG + D_MODEL

LANES = 128
SUBLANES = 8
VMEM_LIMIT_BYTES = 56 * 1024 * 1024
CHUNK = 128


def _cparams(*sem):
    return pltpu.CompilerParams(dimension_semantics=sem, vmem_limit_bytes=VMEM_LIMIT_BYTES)


def _rms(x, gain):
    return x * lax.rsqrt(jnp.mean(x * x, axis=-1, keepdims=True) + EPS) * gain


def _dot(a, b):
    return jnp.dot(a, b, preferred_element_type=F32)


def _dot_nt(a, b):
    return lax.dot_general(a, b, (((1,), (1,)), ((), ())), preferred_element_type=F32)


def _silu(x):
    return x * jax.nn.sigmoid(x)


def _norm_in_body(x_ref, gain_ref, wab_ref, h_ref, ab_ref):
    h = _rms(x_ref[...], gain_ref[...]).astype(BF16)
    h_ref[...] = h
    ab_ref[...] = _dot_nt(h, wab_ref[...].astype(BF16))


def _norm_in(x, gain, w_in_t, *, tm):
    m = x.shape[0]
    return pl.pallas_call(
        _norm_in_body,
        grid=(m // tm,),
        in_specs=[
            pl.BlockSpec((tm, D_MODEL), lambda i: (i, 0)),
            pl.BlockSpec((1, D_MODEL), lambda i: (0, 0)),
            pl.BlockSpec((LANES, D_MODEL), lambda i: (IN_AB // LANES, 0)),
        ],
        out_specs=[
            pl.BlockSpec((tm, D_MODEL), lambda i: (i, 0)),
            pl.BlockSpec((tm, LANES), lambda i: (i, 0)),
        ],
        out_shape=[jax.ShapeDtypeStruct((m, D_MODEL), BF16), jax.ShapeDtypeStruct((m, LANES), F32)],
        compiler_params=_cparams("parallel"),
        name="norm_in",
    )(x, gain, w_in_t)


def _inproj_body(hp_ref, hs_ref, w_ref, op_ref, os_ref, wb_ref, *, n_prompt_tiles):
    i = pl.program_id(1)

    @pl.when(i == 0)
    def _():
        wb_ref[...] = w_ref[...].astype(BF16)

    @pl.when(i < n_prompt_tiles)
    def _():
        op_ref[...] = _dot_nt(hp_ref[...], wb_ref[...])

    @pl.when(i == n_prompt_tiles)
    def _():
        os_ref[...] = _dot_nt(hs_ref[...], wb_ref[...])


def _inproj(h_prompt, h_sample, w_in_t, *, tm, tn):
    mp, ms = h_prompt.shape[0], h_sample.shape[0]
    n_p = mp // tm
    n_out, out_dtype = C_ALL, F32
    n_mid = (QKV_DIM + GDN_DIM) // tn
    n_pool = POOL_DIM // tn

    def row_start(j):
        return jnp.where(j < n_mid, IN_QKV + j * tn,
                         jnp.where(j < n_mid + n_pool, (j - n_mid) * tn, IN_GATES + (j - n_mid - n_pool) * tn))

    return pl.pallas_call(
        functools.partial(_inproj_body, n_prompt_tiles=n_p),
        grid=(n_out // tn, n_p + 1),
        in_specs=[
            pl.BlockSpec((tm, D_MODEL), lambda j, i: (jnp.minimum(i, n_p - 1), 0)),
            pl.BlockSpec((ms, D_MODEL), lambda j, i: (0, 0)),
            pl.BlockSpec((pl.Element(tn), pl.Element(D_MODEL)),
                         lambda j, i: (pl.multiple_of(row_start(j), SUBLANES), 0)),
        ],
        out_specs=[pl.BlockSpec((tm, tn), lambda j, i: (jnp.minimum(i, n_p - 1), j)),
                   pl.BlockSpec((ms, tn), lambda j, i: (0, j))],
        out_shape=[jax.ShapeDtypeStruct((mp, n_out), out_dtype), jax.ShapeDtypeStruct((ms, n_out), out_dtype)],
        scratch_shapes=[pltpu.VMEM((tn, D_MODEL), BF16)],
        compiler_params=_cparams("arbitrary", "arbitrary"),
        name="inproj",
    )(h_prompt, h_sample, w_in_t)


def _pool_group_out(mean, tok, pw_ref, scale_ref, gi):
    cols = slice(gi * POOL_GROUP_DIM, (gi + 1) * POOL_GROUP_DIM)
    y = _dot((mean - tok).astype(BF16), pw_ref[gi])
    return (y * scale_ref[:, cols]).astype(BF16)


def _pool_prompt_body(u_ref, pw_ref, scale_ref, o_ref, ext_ref, *, tt, start_pos):
    t = pl.program_id(1)
    halo = POOL_BUF + 1

    @pl.when(t == 0)
    def _():
        ext_ref[0:halo, :] = jnp.zeros((halo, POOL_DIM), F32)

    ext_ref[halo:halo + tt, :] = u_ref[...]
    pos = start_pos + t * tt + lax.broadcasted_iota(jnp.int32, (tt, 1), 0)
    for gi, w in enumerate(POOL_WINDOWS):
        cols = slice(gi * POOL_GROUP_DIM, (gi + 1) * POOL_GROUP_DIM)
        tok = ext_ref[halo:halo + tt, cols]
        tot = tok
        for i in range(1, w):
            tot = tot + ext_ref[pl.ds(halo - i, tt), cols]
        cnt = jnp.minimum(pos + 1, w).astype(F32)
        o_ref[:, cols] = _pool_group_out(tot / cnt, tok, pw_ref, scale_ref, gi)
    ext_ref[0:halo, :] = ext_ref[tt:tt + halo, :]


def _pool_prompt(proj, pool_w, pool_scale, *, batch, seq, tt):
    nt = seq // tt
    body = functools.partial(_pool_prompt_body, tt=tt, start_pos=0)
    return pl.pallas_call(
        body,
        grid=(batch, nt),
        in_specs=[
            pl.BlockSpec((tt, POOL_DIM), lambda b, t: (b * nt + t, C_POOL // POOL_DIM)),
            pl.BlockSpec((len(POOL_WINDOWS), POOL_GROUP_DIM, POOL_GROUP_DIM), lambda b, t: (0, 0, 0)),
            pl.BlockSpec((1, POOL_DIM), lambda b, t: (0, 0)),
        ],
        out_specs=pl.BlockSpec((tt, POOL_DIM), lambda b, t: (b * nt + t, 0)),
        out_shape=jax.ShapeDtypeStruct((batch * seq, POOL_DIM), BF16),
        scratch_shapes=[pltpu.VMEM((POOL_BUF + 1 + tt, POOL_DIM), F32)],
        compiler_params=_cparams("parallel", "arbitrary"),
        name="pool_prompt",
    )(proj, pool_w, pool_scale)


def _pool_sample_body(u_ref, st_ref, pw_ref, scale_ref, o_ref, *, start_pos):
    for gi, w in enumerate(POOL_WINDOWS):
        cols = slice(gi * POOL_GROUP_DIM, (gi + 1) * POOL_GROUP_DIM)
        tok = u_ref[:, cols]
        tot = tok
        for i in range(1, w):
            tot = tot + st_ref[POOL_BUF - i, :, cols]
        cnt = float(min(start_pos + 1, w))
        o_ref[:, cols] = _pool_group_out(tot / cnt, tok, pw_ref, scale_ref, gi)


def _pool_sample(proj, state_t, pool_w, pool_scale):
    nb = proj.shape[0]
    body = functools.partial(_pool_sample_body, start_pos=PAST_LEN)
    return pl.pallas_call(
        body,
        grid=(1,),
        in_specs=[
            pl.BlockSpec((nb, POOL_DIM), lambda i: (0, C_POOL // POOL_DIM)),
            pl.BlockSpec((POOL_BUF, nb, POOL_DIM), lambda i: (0, 0, 0)),
            pl.BlockSpec((len(POOL_WINDOWS), POOL_GROUP_DIM, POOL_GROUP_DIM), lambda i: (0, 0, 0)),
            pl.BlockSpec((1, POOL_DIM), lambda i: (0, 0)),
        ],
        out_specs=pl.BlockSpec((nb, POOL_DIM), lambda i: (0, 0)),
        out_shape=jax.ShapeDtypeStruct((nb, POOL_DIM), BF16),
        compiler_params=_cparams("arbitrary"),
        name="pool_sample",
    )(proj, state_t, pool_w, pool_scale)


def _split3(x):
    hi = x.astype(BF16)
    r1 = x - hi.astype(F32)
    mid = r1.astype(BF16)
    lo = (r1 - mid.astype(F32)).astype(BF16)
    return hi, mid, lo


def _split2(x):
    hi = x.astype(BF16)
    return hi, (x - hi.astype(F32)).astype(BF16)


def _softplus(x):
    return jnp.maximum(x, 0.0) + jnp.log1p(jnp.exp(-jnp.abs(x)))


def _unit_lower_inverse_minus_identity(mats):
    eye = jnp.where(lax.broadcasted_iota(jnp.int32, (CHUNK, CHUNK), 0)
                    == lax.broadcasted_iota(jnp.int32, (CHUNK, CHUNK), 1), 1.0, 0.0)
    levels = CHUNK.bit_length() - 1
    qbs = [(-a).astype(BF16) for a in mats]
    rs = [eye - a for a in mats]
    qs = [_dot(qb, qb) for qb in qbs]
    for j in range(1, levels):
        qbs = [q.astype(BF16) for q in qs]
        rbs = [r.astype(BF16) for r in rs]
        if j < levels - 1:
            ps = [_dot(qb, jnp.concatenate([qb, rb], axis=1)) for qb, rb in zip(qbs, rbs)]
            rs = [r + p[:, CHUNK:] for r, p in zip(rs, ps)]
            qs = [p[:, :CHUNK] for p in ps]
        else:
            rs = [r + _dot(qb, rb) for r, qb, rb in zip(rs, qbs, rbs)]
    return [r - eye for r in rs]


def _gdn_prompt_body(q_ref, k_ref, v_ref, z_ref, ab_ref, cwq_ref, cwk_ref, cwv_ref, alog_ref, dtb_ref, gn_ref,
                     o_ref, ssm_ref, ext_ref, s_ref, gct_ref, *, hg, tt):
    hgi = pl.program_id(1)
    t = pl.program_id(2)
    w = hg * HEAD_DIM
    halo = SUBLANES

    @pl.when(t == 0)
    def _():
        ext_ref[0:halo, :] = jnp.zeros((halo, 3 * w), F32)
        s_ref[...] = jnp.zeros_like(s_ref)

    ext_ref[halo:halo + tt, 0:w] = q_ref[...]
    ext_ref[halo:halo + tt, w:2 * w] = k_ref[...]
    ext_ref[halo:halo + tt, 2 * w:3 * w] = v_ref[...]

    def conv(part, cw_ref):
        cols = slice(part * w, (part + 1) * w)
        acc = ext_ref[halo:halo + tt, cols] * cw_ref[CONV_BUF:CONV_WIDTH, :]
        for i in range(CONV_BUF):
            acc = acc + ext_ref[pl.ds(halo - CONV_BUF + i, tt), cols] * cw_ref[i:i + 1, :]
        return _silu(acc)

    qc = conv(0, cwq_ref)
    kc = conv(1, cwk_ref)
    vc = conv(2, cwv_ref)
    ext_ref[0:halo, :] = ext_ref[tt:tt + halo, :]

    ab = ab_ref[...]
    g_all = -jnp.exp(alog_ref[...]) * _softplus(ab + dtb_ref[...])
    beta_all = jax.nn.sigmoid(ab)
    ri = lax.broadcasted_iota(jnp.int32, (tt, tt), 0)
    ci = lax.broadcasted_iota(jnp.int32, (tt, tt), 1)
    shift = CHUNK.bit_length() - 1
    tri = jnp.where(((ri >> shift) == (ci >> shift)) & (ci <= ri), 1.0, 0.0).astype(BF16)
    g_hi, g_mid, g_lo = _split3(g_all)
    gc_all = _dot(tri, g_hi) + _dot(tri, g_mid) + _dot(tri, g_lo)
    gct_ref[...] = gc_all.T

    lane = lax.broadcasted_iota(jnp.int32, (tt, LANES), 1)
    r128 = lax.broadcasted_iota(jnp.int32, (CHUNK, CHUNK), 0)
    c128 = lax.broadcasted_iota(jnp.int32, (CHUNK, CHUNK), 1)
    sub8 = lax.broadcasted_iota(jnp.int32, (SUBLANES, CHUNK), 0)
    n_chunks = tt // CHUNK
    chains = [(hh, c) for hh in range(hg) for c in range(n_chunks)]
    gcol_heads, bcol_heads = [], []
    for hh in range(hg):
        head = hgi * hg + hh
        gcol_heads.append(jnp.sum(jnp.where(lane == head, gc_all, 0.0), axis=-1, keepdims=True))
        bcol_heads.append(jnp.sum(jnp.where(lane == head + HEADS, beta_all, 0.0), axis=-1, keepdims=True))

    qn_l, kn_l, kb_l, rhs_l, eg_l, gcol_l, dec_l = [], [], [], [], [], [], []
    for hh, c in chains:
        head = hgi * hg + hh
        rows = slice(c * CHUNK, (c + 1) * CHUNK)
        lanes = slice(hh * HEAD_DIM, (hh + 1) * HEAD_DIM)
        qh, kh, vh = qc[rows, lanes], kc[rows, lanes], vc[rows, lanes]
        qn = qh * lax.rsqrt(jnp.sum(qh * qh, axis=-1, keepdims=True) + EPS) * (HEAD_DIM ** -0.5)
        kn = kh * lax.rsqrt(jnp.sum(kh * kh, axis=-1, keepdims=True) + EPS)
        gcol = gcol_heads[hh][rows]
        bcol = bcol_heads[hh][rows]
        gblk = gct_ref[pl.ds(pl.multiple_of((head >> 3) * SUBLANES, SUBLANES), SUBLANES), rows]
        grow = jnp.sum(jnp.where(sub8 == (head & (SUBLANES - 1)), gblk, 0.0), axis=0, keepdims=True)
        eg = jnp.exp(gcol)
        kb = kn * bcol
        qn_l.append(qn)
        kn_l.append(kn)
        kb_l.append(kb)
        eg_l.append(eg)
        gcol_l.append(gcol)
        dec_l.append(jnp.exp(jnp.minimum(gcol - grow, 0.0)))
        rhs_l.append(jnp.concatenate([vh * bcol, kb * eg], axis=1))

    a_l, qk_l = [], []
    for qn, kn, kb, decay in zip(qn_l, kn_l, kb_l, dec_l):
        knb = kn.astype(BF16)
        a_l.append(_dot_nt(kb.astype(BF16), knb) * jnp.where(c128 < r128, decay, 0.0))
        qk_l.append(_dot_nt(qn.astype(BF16), knb) * jnp.where(c128 <= r128, decay, 0.0))
    tinv_l = [x.astype(BF16) for x in _unit_lower_inverse_minus_identity(a_l)]
    sol_l = [rhs + _dot(tinv, rhs.astype(BF16)) for tinv, rhs in zip(tinv_l, rhs_l)]
    res_l = []
    for a, sol, rhs in zip(a_l, sol_l, rhs_l):
        a_hi, a_lo = _split2(a)
        s_hi, s_lo = _split2(sol)
        a_sol = _dot(jnp.concatenate([a_hi, a_lo, a_hi], axis=1), jnp.concatenate([s_hi, s_hi, s_lo], axis=0))
        res_l.append((rhs - sol) - a_sol)
    sol_l = [(sol + res + _dot(tinv, res.astype(BF16))).astype(BF16) for sol, res, tinv in zip(sol_l, res_l, tinv_l)]
    lhs_l, o_add_l, s_add_l, glast_l = [], [], [], []
    for qn, kn, eg, gcol, qk, sol_b in zip(qn_l, kn_l, eg_l, gcol_l, qk_l, sol_l):
        glast = gcol[CHUNK - 1:CHUNK, :]
        ktail = kn * jnp.exp(glast - gcol)
        kt_sol = _dot(ktail.T.astype(BF16), sol_b)
        qk_sol = _dot(qk.astype(BF16), sol_b)
        lhs_l.append(jnp.concatenate([qn * eg - qk_sol[:, HEAD_DIM:], kt_sol[:, HEAD_DIM:]], axis=0).astype(BF16))
        o_add_l.append(qk_sol[:, :HEAD_DIM])
        s_add_l.append(kt_sol[:, :HEAD_DIM])
        glast_l.append(jnp.exp(glast))
    states = [s_ref[hh] for hh in range(hg)]
    for c in range(n_chunks):
        for hh in range(hg):
            i = hh * n_chunks + c
            rows = slice(c * CHUNK, (c + 1) * CHUNK)
            lanes = slice(hh * HEAD_DIM, (hh + 1) * HEAD_DIM)
            on_s = _dot(lhs_l[i], states[hh].astype(BF16))
            o = on_s[:CHUNK] + o_add_l[i]
            states[hh] = states[hh] * glast_l[i] - on_s[CHUNK:] + s_add_l[i]
            o_ref[rows, lanes] = (_rms(o, gn_ref[...]) * _silu(z_ref[rows, lanes])).astype(BF16)
    for hh in range(hg):
        s_ref[hh] = states[hh]

    @pl.when(t == pl.num_programs(2) - 1)
    def _():
        ssm_ref[0, 0] = s_ref[...]


def _gdn_prompt(proj, ab, conv_w, alog_pad, dtb_pad, gdn_norm, *, batch, seq, hg, tt):
    nt = seq // tt
    w = hg * HEAD_DIM
    body = functools.partial(_gdn_prompt_body, hg=hg, tt=tt)

    def col_spec(col0):
        return pl.BlockSpec((tt, w), lambda b, h, t: (b * nt + t, col0 // w + h))

    def cw_spec(part):
        return pl.BlockSpec((CONV_WIDTH, w), lambda b, h, t: (0, part * GDN_DIM // w + h))

    small = pl.BlockSpec((1, LANES), lambda b, h, t: (0, 0))
    return pl.pallas_call(
        body,
        grid=(batch, HEADS // hg, nt),
        in_specs=[col_spec(C_Q), col_spec(C_K), col_spec(C_V), col_spec(C_Z),
                  pl.BlockSpec((tt, LANES), lambda b, h, t: (b * nt + t, 0)),
                  cw_spec(0), cw_spec(1), cw_spec(2), small, small, small],
        out_specs=[
            pl.BlockSpec((tt, w), lambda b, h, t: (b * nt + t, h)),
            pl.BlockSpec((1, 1, hg, HEAD_DIM, HEAD_DIM), lambda b, h, t: (0, b, h, 0, 0)),
        ],
        out_shape=[
            jax.ShapeDtypeStruct((batch * seq, GDN_DIM), BF16),
            jax.ShapeDtypeStruct((1, batch, HEADS, HEAD_DIM, HEAD_DIM), F32),
        ],
        scratch_shapes=[
            pltpu.VMEM((SUBLANES + tt, 3 * w), F32),
            pltpu.VMEM((hg, HEAD_DIM, HEAD_DIM), F32),
            pltpu.VMEM((LANES, tt), F32),
        ],
        compiler_params=_cparams("parallel", "parallel", "arbitrary"),
        name="gdn_prompt",
    )(proj, proj, proj, proj, ab, conv_w, conv_w, conv_w, alog_pad, dtb_pad, gdn_norm)


def _gdn_sample_prep_body(q_ref, k_ref, v_ref, ab_ref, st_ref, cw_ref, alog_ref, dtb_ref,
                          qn_ref, kn_ref, vc_ref, eg_ref, beta_ref, qk_ref):
    def conv(part, u_ref):
        cols = slice(part * GDN_DIM, (part + 1) * GDN_DIM)
        acc = u_ref[...] * cw_ref[CONV_BUF:CONV_WIDTH, cols]
        for i in range(CONV_BUF):
            acc = acc + st_ref[i, :, cols] * cw_ref[i:i + 1, cols]
        return _silu(acc)

    qc = conv(0, q_ref)
    kc = conv(1, k_ref)
    vc_ref[...] = conv(2, v_ref)
    ab = ab_ref[...]
    g_all = -jnp.exp(alog_ref[...]) * _softplus(ab + dtb_ref[...])
    eg_all = jnp.exp(g_all)
    beta_all = jax.nn.sigmoid(ab)
    for h in range(HEADS):
        lanes = slice(h * HEAD_DIM, (h + 1) * HEAD_DIM)
        qh, kh = qc[:, lanes], kc[:, lanes]
        qn = qh * lax.rsqrt(jnp.sum(qh * qh, axis=-1, keepdims=True) + EPS) * (HEAD_DIM ** -0.5)
        kn = kh * lax.rsqrt(jnp.sum(kh * kh, axis=-1, keepdims=True) + EPS)
        qn_ref[:, lanes] = qn
        kn_ref[:, lanes] = kn
        shape = qn.shape
        qk_ref[:, lanes] = jnp.broadcast_to(jnp.sum(qn * kn, axis=-1, keepdims=True), shape)
        eg_ref[:, lanes] = jnp.broadcast_to(eg_all[:, h:h + 1], shape)
        beta_ref[:, lanes] = jnp.broadcast_to(beta_all[:, HEADS + h:HEADS + h + 1], shape)


def _gdn_sample_prep(proj, ab, conv_state_t, conv_w, alog_pad, dtb_pad):
    nb = proj.shape[0]

    def col_spec(col0):
        return pl.BlockSpec((nb, GDN_DIM), lambda i: (0, col0 // GDN_DIM))

    small = pl.BlockSpec((1, LANES), lambda i: (0, 0))
    full = pl.BlockSpec((nb, GDN_DIM), lambda i: (0, 0))
    return pl.pallas_call(
        _gdn_sample_prep_body,
        grid=(1,),
        in_specs=[col_spec(C_Q), col_spec(C_K), col_spec(C_V),
                  pl.BlockSpec((nb, LANES), lambda i: (0, 0)),
                  pl.BlockSpec((CONV_BUF, nb, QKV_DIM), lambda i: (0, 0, 0)),
                  pl.BlockSpec((CONV_WIDTH, QKV_DIM), lambda i: (0, 0)),
                  small, small],
        out_specs=[full] * 6,
        out_shape=[jax.ShapeDtypeStruct((nb, GDN_DIM), F32)] * 6,
        compiler_params=_cparams("arbitrary"),
        name="gdn_sample_prep",
    )(proj, proj, proj, ab, conv_state_t, conv_w, alog_pad, dtb_pad)


def _gdn_sample_step_body(qn_ref, kn_ref, vc_ref, eg_ref, beta_ref, qk_ref, z_ref, gn_ref, s_ref,
                          o_ref, snew_ref, *, bb):
    pad = jnp.zeros((LANES - HEADS, HEAD_DIM), F32)
    for b in range(bb):
        k16, q16 = kn_ref[b], qn_ref[b]
        cols = jnp.concatenate([k16, pad], axis=0).T
        kq = jnp.concatenate([k16, q16], axis=0).astype(BF16)
        outs = []
        for h in range(HEADS):
            kcol = cols[:, h:h + 1]
            s = s_ref[0, b, h]
            kq_s = _dot(kq, s.astype(BF16))
            k_s = kq_s[h:h + 1]
            q_s = kq_s[HEADS + h:HEADS + h + 1]
            eg = eg_ref[b, h:h + 1, :]
            v_new = beta_ref[b, h:h + 1, :] * (vc_ref[b, h:h + 1, :] - eg * k_s)
            snew_ref[0, b, h] = eg * s + kcol * v_new
            outs.append(eg * q_s + qk_ref[b, h:h + 1, :] * v_new)
        o = jnp.concatenate(outs, axis=0)
        o_ref[b] = (_rms(o, gn_ref[...]) * _silu(z_ref[b])).astype(BF16)


def _gdn_sample_step(qn, kn, vc, eg, beta, qk, z, gdn_norm, state_ssm, *, bb):
    nb = qn.shape[0]
    body = functools.partial(_gdn_sample_step_body, bb=bb)
    vec = pl.BlockSpec((bb, HEADS, HEAD_DIM), lambda i: (i, 0, 0))
    st = pl.BlockSpec((1, bb, HEADS, HEAD_DIM, HEAD_DIM), lambda i: (0, i, 0, 0, 0))
    return pl.pallas_call(
        body,
        grid=(nb // bb,),
        in_specs=[vec] * 7 + [pl.BlockSpec((1, LANES), lambda i: (0, 0)), st],
        out_specs=[vec, st],
        out_shape=[jax.ShapeDtypeStruct((nb, HEADS, HEAD_DIM), BF16),
                   jax.ShapeDtypeStruct(state_ssm.shape, F32)],
        compiler_params=_cparams("parallel"),
        name="gdn_sample_step",
    )(qn, kn, vc, eg, beta, qk, z, gdn_norm, state_ssm)


def _merge_body(pool_ref, gdn_ref, wpu_ref, wgu_ref, gp_ref, gg_ref, o_ref):
    o_ref[...] = (jax.nn.sigmoid(gp_ref[...]) * _dot(pool_ref[...], wpu_ref[...])
                  + jax.nn.sigmoid(gg_ref[...]) * _dot(gdn_ref[...], wgu_ref[...])).astype(BF16)


def _merge(proj, pool_out, gdn_out, w_pool_up, w_gdn_up, *, tm, tn=512):
    m = pool_out.shape[0]
    nj = D_MODEL // tn
    return pl.pallas_call(
        _merge_body,
        grid=(m // tm, nj),
        in_specs=[
            pl.BlockSpec((tm, POOL_DIM), lambda i, j: (i, 0)),
            pl.BlockSpec((tm, GDN_DIM), lambda i, j: (i, 0)),
            pl.BlockSpec((POOL_DIM, tn), lambda i, j: (0, j)),
            pl.BlockSpec((GDN_DIM, tn), lambda i, j: (0, j)),
            pl.BlockSpec((tm, tn), lambda i, j: (i, C_GP // tn + j)),
            pl.BlockSpec((tm, tn), lambda i, j: (i, C_GG // tn + j)),
        ],
        out_specs=pl.BlockSpec((tm, tn), lambda i, j: (i, j)),
        out_shape=jax.ShapeDtypeStruct((m, D_MODEL), BF16),
        compiler_params=_cparams("parallel", "parallel"),
        name="merge",
    )(pool_out, gdn_out, w_pool_up, w_gdn_up, proj, proj)


def _outproj_body(a_ref, w_ref, x_ref, o_ref):
    o_ref[...] = x_ref[...] + _dot(a_ref[...], w_ref[...])


def _outproj(merged, x, w_o, *, tm, tn=512):
    m = x.shape[0]
    return pl.pallas_call(
        _outproj_body,
        grid=(m // tm, D_MODEL // tn),
        in_specs=[
            pl.BlockSpec((tm, D_MODEL), lambda i, j: (i, 0)),
            pl.BlockSpec((D_MODEL, tn), lambda i, j: (0, j)),
            pl.BlockSpec((tm, tn), lambda i, j: (i, j)),
        ],
        out_specs=pl.BlockSpec((tm, tn), lambda i, j: (i, j)),
        out_shape=jax.ShapeDtypeStruct((m, D_MODEL), F32),
        compiler_params=_cparams("parallel", "parallel"),
        name="outproj",
    )(merged, w_o, x)


def _ffn_body(x_ref, gain_ref, wg_ref, wu_ref, wd_ref, o_ref, h_ref):
    f = pl.program_id(1)

    @pl.when(f == 0)
    def _():
        x = x_ref[...]
        h_ref[...] = _rms(x, gain_ref[...]).astype(BF16)
        o_ref[...] = x

    h = h_ref[...]
    act = (_silu(_dot(h, wg_ref[...].astype(BF16))) * _dot(h, wu_ref[...].astype(BF16))).astype(BF16)
    o_ref[...] += _dot(act, wd_ref[...].astype(BF16))


def _ffn(x, gain, w_gate_up, w_down, *, tm, tf):
    m = x.shape[0]
    nf = D_FF // tf
    return pl.pallas_call(
        _ffn_body,
        grid=(m // tm, nf),
        in_specs=[
            pl.BlockSpec((tm, D_MODEL), lambda i, f: (i, 0)),
            pl.BlockSpec((1, D_MODEL), lambda i, f: (0, 0)),
            pl.BlockSpec((D_MODEL, tf), lambda i, f: (0, f)),
            pl.BlockSpec((D_MODEL, tf), lambda i, f: (0, nf + f)),
            pl.BlockSpec((tf, D_MODEL), lambda i, f: (f, 0)),
        ],
        out_specs=pl.BlockSpec((tm, D_MODEL), lambda i, f: (i, 0)),
        out_shape=jax.ShapeDtypeStruct((m, D_MODEL), F32),
        scratch_shapes=[pltpu.VMEM((tm, D_MODEL), BF16)],
        compiler_params=_cparams("parallel", "arbitrary"),
        name="ffn",
    )(x, gain, w_gate_up, w_gate_up, w_down)


def _final_body(x_ref, p_ref, wple_ref, wpg_ref, gain_ref, o_ref):
    x = x_ref[...]
    gate = jax.nn.sigmoid(_dot(x.astype(BF16), wpg_ref[...]))
    emb = _dot(p_ref[...].astype(BF16), wple_ref[...])
    o_ref[...] = _rms(x + emb * gate, gain_ref[...])


def _final(x, p, w_ple, w_ple_gate, gain, *, tm):
    m = x.shape[0]
    return pl.pallas_call(
        _final_body,
        grid=(m // tm,),
        in_specs=[
            pl.BlockSpec((tm, D_MODEL), lambda i: (i, 0)),
            pl.BlockSpec((tm, PLE_DIM), lambda i: (i, 0)),
            pl.BlockSpec((PLE_DIM, D_MODEL), lambda i: (0, 0)),
            pl.BlockSpec((D_MODEL, D_MODEL), lambda i: (0, 0)),
            pl.BlockSpec((1, D_MODEL), lambda i: (0, 0)),
        ],
        out_specs=pl.BlockSpec((tm, D_MODEL), lambda i: (i, 0)),
        out_shape=jax.ShapeDtypeStruct((m, D_MODEL), F32),
        compiler_params=_cparams("parallel"),
        name="final",
    )(x, p, w_ple, w_ple_gate, gain)


def _tile_rows(m, want):
    return want if m % want == 0 else m


def _dense_tail(x, p, proj, pool_out, gdn_out, wts):
    m = x.shape[0]
    tm = _tile_rows(m, 1024)
    merged = _merge(proj, pool_out, gdn_out, wts["w_pool_up"], wts["w_gdn_up"], tm=tm)
    x = _outproj(merged, x, wts["w_o"], tm=tm)
    x = _ffn(x, wts["norm_ffn"], wts["w_gate_up"], wts["w_down"], tm=tm, tf=256)
    return _final(x, p, wts["w_ple"], wts["w_ple_gate"], wts["norm_final"], tm=_tile_rows(m, 512))


def kernel(x_prompt, x_sample, p_prompt, p_sample, state_pool, state_conv, state_ssm, norm_mix, w_in, pool_w,
           pool_scale, conv_w, a_log, dt_bias, gdn_norm, w_pool_up, w_gdn_up, w_o, norm_ffn, w_gate_up, w_down,
           w_ple, w_ple_gate, norm_final):
    batch, seq, _ = x_prompt.shape
    nb = x_sample.shape[0]
    w_in_t = jnp.swapaxes(w_in[0], 0, 1)
    wts = dict(
        w_pool_up=w_pool_up[0].astype(BF16), w_gdn_up=w_gdn_up[0].astype(BF16), w_o=w_o[0].astype(BF16),
        norm_ffn=norm_ffn, w_gate_up=w_gate_up[0], w_down=w_down[0],
        w_ple=w_ple[0].astype(BF16), w_ple_gate=w_ple_gate[0].astype(BF16), norm_final=norm_final[None, :],
    )
    pool_w_b = pool_w[0].astype(BF16)
    lane_pad = ((0, 0), (0, LANES - HEADS))
    alog_pad = jnp.pad(a_log, lane_pad)
    dtb_pad = jnp.pad(dt_bias, lane_pad)
    gn = gdn_norm

    mp = batch * seq
    xp = x_prompt.reshape(mp, D_MODEL)
    xs = x_sample.reshape(nb, D_MODEL)
    h_p, ab_p = _norm_in(xp, norm_mix, w_in_t, tm=_tile_rows(mp, 512))
    h_s, ab_s = _norm_in(xs, norm_mix, w_in_t, tm=nb)
    tm_in = _tile_rows(mp, 1024)
    proj_p, proj_s = _inproj(h_p, h_s, w_in_t, tm=tm_in, tn=1024)

    pool_out_p = _pool_prompt(proj_p, pool_w_b, pool_scale, batch=batch, seq=seq, tt=_tile_rows(seq, 512))
    gdn_out_p, ssm_p = _gdn_prompt(proj_p, ab_p, conv_w[0], alog_pad, dtb_pad, gn, batch=batch, seq=seq,
                                   hg=4, tt=_tile_rows(seq, 512))
    y_p = _dense_tail(xp, p_prompt[0].reshape(mp, PLE_DIM), proj_p, pool_out_p, gdn_out_p, wts)
    proj_p3 = proj_p.reshape(batch, seq, C_ALL)
    pool_p = proj_p3[:, seq - POOL_BUF:, C_POOL:C_END][None]
    conv_p = proj_p3[:, seq - CONV_BUF:, C_Q:C_Z][None]

    pool_out_s = _pool_sample(proj_s, jnp.swapaxes(state_pool[0], 0, 1), pool_w_b, pool_scale)
    prep = _gdn_sample_prep(proj_s, ab_s, jnp.swapaxes(state_conv[0], 0, 1), conv_w[0], alog_pad, dtb_pad)
    to_heads = lambda a: a.reshape(nb, HEADS, HEAD_DIM)
    gdn_out_s, ssm_s = _gdn_sample_step(*[to_heads(a) for a in prep], to_heads(proj_s[:, C_Z:C_POOL]), gn,
                                        state_ssm, bb=4)
    y_s = _dense_tail(xs, p_sample[0].reshape(nb, PLE_DIM), proj_s, pool_out_s, gdn_out_s.reshape(nb, GDN_DIM), wts)
    pool_s = jnp.concatenate([state_pool[0][:, 1:], proj_s[:, None, C_POOL:C_END]], axis=1)[None]
    conv_s = jnp.concatenate([state_conv[0][:, 1:], proj_s[:, None, C_Q:C_Z]], axis=1)[None]

    return (y_p.reshape(batch, seq, D_MODEL), y_s.reshape(nb, 1, D_MODEL), pool_p, conv_p, ssm_p,
            pool_s, conv_s, ssm_s)
```

```python
import functools

import jax
import jax.numpy as jnp
from jax import lax
from jax.experimental import pallas as pl
from jax.experimental.pallas import tpu as pltpu

F32 = jnp.float32
BF16 = jnp.bfloat16

D_MODEL = 2048
PAST_LEN = 16384
POOL_WINDOWS = (2, 4, 8, 16)
POOL_GROUP_DIM = D_MODEL // 8
POOL_DIM = len(POOL_WINDOWS) * POOL_GROUP_DIM
POOL_BUF = max(POOL_WINDOWS) - 1
HEAD_DIM = 128
HEADS = D_MODEL // HEAD_DIM
GDN_DIM = HEADS * HEAD_DIM
QKV_DIM = 3 * GDN_DIM
CONV_WIDTH = 4
CONV_BUF = CONV_WIDTH - 1
D_FF = -(-8 * D_MODEL // (3 * 256)) * 256
PLE_DIM = 256
EPS = 1e-6

IN_QKV = POOL_DIM
IN_Z = IN_QKV + QKV_DIM
IN_AB = IN_Z + GDN_DIM
IN_GATES = IN_AB + 2 * HEADS
C_Q = 0
C_K = C_Q + GDN_DIM
C_V = C_K + GDN_DIM
C_Z = C_V + GDN_DIM
C_POOL = C_Z + GDN_DIM
C_END = C_POOL + POOL_DIM
C_GP = C_END
C_GG = C_GP + D_MODEL
C_ALL = C_GG + D_MODEL

LANES = 128
SUBLANES = 8
VMEM_LIMIT_BYTES = 56 * 1024 * 1024
CHUNK = 128


def _cparams(*sem):
    return pltpu.CompilerParams(dimension_semantics=sem, vmem_limit_bytes=VMEM_LIMIT_BYTES)


def _rms(x, gain):
    return x * lax.rsqrt(jnp.mean(x * x, axis=-1, keepdims=True) + EPS) * gain


def _dot(a, b):
    return jnp.dot(a, b, preferred_element_type=F32)


def _dot_nt(a, b):
    return lax.dot_general(a, b, (((1,), (1,)), ((), ())), preferred_element_type=F32)


def _silu(x):
    return x * jax.nn.sigmoid(x)


def _norm_in_body(x_ref, gain_ref, wab_ref, h_ref, ab_ref):
    h = _rms(x_ref[...], gain_ref[...]).astype(BF16)
    h_ref[...] = h
    ab_ref[...] = _dot_nt(h, wab_ref[...].astype(BF16))


def _norm_in(x, gain, w_in_t, *, tm):
    m = x.shape[0]
    return pl.pallas_call(
        _norm_in_body,
        grid=(m // tm,),
        in_specs=[
            pl.BlockSpec((tm, D_MODEL), lambda i: (i, 0)),
            pl.BlockSpec((1, D_MODEL), lambda i: (0, 0)),
            pl.BlockSpec((LANES, D_MODEL), lambda i: (IN_AB // LANES, 0)),
        ],
        out_specs=[
            pl.BlockSpec((tm, D_MODEL), lambda i: (i, 0)),
            pl.BlockSpec((tm, LANES), lambda i: (i, 0)),
        ],
        out_shape=[jax.ShapeDtypeStruct((m, D_MODEL), BF16), jax.ShapeDtypeStruct((m, LANES), F32)],
        compiler_params=_cparams("parallel"),
        name="norm_in",
    )(x, gain, w_in_t)


def _inproj_body(hp_ref, hs_ref, w_ref, op_ref, os_ref, wb_ref):
    i = pl.program_id(1)

    @pl.when(i == 0)
    def _():
        wb_ref[...] = w_ref[...].astype(BF16)
        os_ref[...] = _dot_nt(hs_ref[...], wb_ref[...])

    @pl.when(i > 0)
    def _():
        op_ref[...] = _dot_nt(hp_ref[...], wb_ref[...])


def _inproj(h_prompt, h_sample, w_in_t, *, tm, tn):
    mp, ms = h_prompt.shape[0], h_sample.shape[0]
    n_p = mp // tm
    n_out, out_dtype = C_ALL, F32
    n_mid = (QKV_DIM + GDN_DIM) // tn
    n_pool = POOL_DIM // tn

    def row_start(j):
        return jnp.where(j < n_mid, IN_QKV + j * tn,
                         jnp.where(j < n_mid + n_pool, (j - n_mid) * tn, IN_GATES + (j - n_mid - n_pool) * tn))

    return pl.pallas_call(
        _inproj_body,
        grid=(n_out // tn, n_p + 1),
        in_specs=[
            pl.BlockSpec((tm, D_MODEL), lambda j, i: (jnp.maximum(i - 1, 0), 0)),
            pl.BlockSpec((ms, D_MODEL), lambda j, i: (0, 0)),
            pl.BlockSpec((pl.Element(tn), pl.Element(D_MODEL)),
                         lambda j, i: (pl.multiple_of(row_start(j), SUBLANES), 0)),
        ],
        out_specs=[pl.BlockSpec((tm, tn), lambda j, i: (jnp.maximum(i - 1, 0), j)),
                   pl.BlockSpec((ms, tn), lambda j, i: (0, j))],
        out_shape=[jax.ShapeDtypeStruct((mp, n_out), out_dtype), jax.ShapeDtypeStruct((ms, n_out), out_dtype)],
        scratch_shapes=[pltpu.VMEM((tn, D_MODEL), BF16)],
        compiler_params=_cparams("arbitrary", "arbitrary"),
        name="inproj",
    )(h_prompt, h_sample, w_in_t)


def _pool_group_out(mean, tok, pw_ref, scale_ref, gi):
    cols = slice(gi * POOL_GROUP_DIM, (gi + 1) * POOL_GROUP_DIM)
    y = _dot((mean - tok).astype(BF16), pw_ref[gi])
    return (y * scale_ref[:, cols]).astype(BF16)


def _pool_prompt_body(u_ref, pw_ref, scale_ref, o_ref, ext_ref, *, tt, start_pos):
    t = pl.program_id(1)
    halo = POOL_BUF + 1

    @pl.when(t == 0)
    def _():
        ext_ref[0:halo, :] = jnp.zeros((halo, POOL_DIM), F32)

    ext_ref[halo:halo + tt, :] = u_ref[...]
    pos = start_pos + t * tt + lax.broadcasted_iota(jnp.int32, (tt, 1), 0)
    for gi, w in enumerate(POOL_WINDOWS):
        cols = slice(gi * POOL_GROUP_DIM, (gi + 1) * POOL_GROUP_DIM)
        tok = ext_ref[halo:halo + tt, cols]
        tot = tok
        for i in range(1, w):
            tot = tot + ext_ref[pl.ds(halo - i, tt), cols]
        cnt = jnp.minimum(pos + 1, w).astype(F32)
        o_ref[:, cols] = _pool_group_out(tot / cnt, tok, pw_ref, scale_ref, gi)
    ext_ref[0:halo, :] = ext_ref[tt:tt + halo, :]


def _pool_prompt(proj, pool_w, pool_scale, *, batch, seq, tt):
    nt = seq // tt
    body = functools.partial(_pool_prompt_body, tt=tt, start_pos=0)
    return pl.pallas_call(
        body,
        grid=(batch, nt),
        in_specs=[
            pl.BlockSpec((tt, POOL_DIM), lambda b, t: (b * nt + t, C_POOL // POOL_DIM)),
            pl.BlockSpec((len(POOL_WINDOWS), POOL_GROUP_DIM, POOL_GROUP_DIM), lambda b, t: (0, 0, 0)),
            pl.BlockSpec((1, POOL_DIM), lambda b, t: (0, 0)),
        ],
        out_specs=pl.BlockSpec((tt, POOL_DIM), lambda b, t: (b * nt + t, 0)),
        out_shape=jax.ShapeDtypeStruct((batch * seq, POOL_DIM), BF16),
        scratch_shapes=[pltpu.VMEM((POOL_BUF + 1 + tt, POOL_DIM), F32)],
        compiler_params=_cparams("parallel", "arbitrary"),
        name="pool_prompt",
    )(proj, pool_w, pool_scale)


def _pool_sample_body(u_ref, st_ref, pw_ref, scale_ref, o_ref, *, start_pos):
    for gi, w in enumerate(POOL_WINDOWS):
        cols = slice(gi * POOL_GROUP_DIM, (gi + 1) * POOL_GROUP_DIM)
        tok = u_ref[:, cols]
        tot = tok
        for i in range(1, w):
            tot = tot + st_ref[POOL_BUF - i, :, cols]
        cnt = float(min(start_pos + 1, w))
        o_ref[:, cols] = _pool_group_out(tot / cnt, tok, pw_ref, scale_ref, gi)


def _pool_sample(proj, state_t, pool_w, pool_scale):
    nb = proj.shape[0]
    body = functools.partial(_pool_sample_body, start_pos=PAST_LEN)
    return pl.pallas_call(
        body,
        grid=(1,),
        in_specs=[
            pl.BlockSpec((nb, POOL_DIM), lambda i: (0, C_POOL // POOL_DIM)),
            pl.BlockSpec((POOL_BUF, nb, POOL_DIM), lambda i: (0, 0, 0)),
            pl.BlockSpec((len(POOL_WINDOWS), POOL_GROUP_DIM, POOL_GROUP_DIM), lambda i: (0, 0, 0)),
            pl.BlockSpec((1, POOL_DIM), lambda i: (0, 0)),
        ],
        out_specs=pl.BlockSpec((nb, POOL_DIM), lambda i: (0, 0)),
        out_shape=jax.ShapeDtypeStruct((nb, POOL_DIM), BF16),
        compiler_params=_cparams("arbitrary"),
        name="pool_sample",
    )(proj, state_t, pool_w, pool_scale)


def _split3(x):
    hi = x.astype(BF16)
    r1 = x - hi.astype(F32)
    mid = r1.astype(BF16)
    lo = (r1 - mid.astype(F32)).astype(BF16)
    return hi, mid, lo


def _softplus(x):
    return jnp.maximum(x, 0.0) + jnp.log1p(jnp.exp(-jnp.abs(x)))


def _unit_lower_inverse_minus_identity(mats):
    eye = jnp.where(lax.broadcasted_iota(jnp.int32, (CHUNK, CHUNK), 0)
                    == lax.broadcasted_iota(jnp.int32, (CHUNK, CHUNK), 1), 1.0, 0.0)
    levels = CHUNK.bit_length() - 1
    qbs = [(-a).astype(BF16) for a in mats]
    rs = [eye - a for a in mats]
    qs = [_dot(qb, qb) for qb in qbs]
    for j in range(1, levels):
        qbs = [q.astype(BF16) for q in qs]
        rbs = [r.astype(BF16) for r in rs]
        if j < levels - 1:
            ps = [_dot(qb, jnp.concatenate([qb, rb], axis=1)) for qb, rb in zip(qbs, rbs)]
            rs = [r + p[:, CHUNK:] for r, p in zip(rs, ps)]
            qs = [p[:, :CHUNK] for p in ps]
        else:
            rs = [r + _dot(qb, rb) for r, qb, rb in zip(rs, qbs, rbs)]
    return [r - eye for r in rs]


def _gdn_prompt_body(q_ref, k_ref, v_ref, z_ref, ab_ref, cwq_ref, cwk_ref, cwv_ref, alog_ref, dtb_ref, gn_ref,
                     o_ref, ssm_ref, ext_ref, s_ref, gct_ref, *, hg, tt):
    hgi = pl.program_id(1)
    t = pl.program_id(2)
    w = hg * HEAD_DIM
    halo = SUBLANES

    @pl.when(t == 0)
    def _():
        ext_ref[0:halo, :] = jnp.zeros((halo, 3 * w), F32)
        s_ref[...] = jnp.zeros_like(s_ref)

    ext_ref[halo:halo + tt, 0:w] = q_ref[...]
    ext_ref[halo:halo + tt, w:2 * w] = k_ref[...]
    ext_ref[halo:halo + tt, 2 * w:3 * w] = v_ref[...]

    def conv(part, cw_ref):
        cols = slice(part * w, (part + 1) * w)
        acc = ext_ref[halo:halo + tt, cols] * cw_ref[CONV_BUF:CONV_WIDTH, :]
        for i in range(CONV_BUF):
            acc = acc + ext_ref[pl.ds(halo - CONV_BUF + i, tt), cols] * cw_ref[i:i + 1, :]
        return _silu(acc)

    qc = conv(0, cwq_ref)
    kc = conv(1, cwk_ref)
    vc = conv(2, cwv_ref)
    ext_ref[0:halo, :] = ext_ref[tt:tt + halo, :]

    ab = ab_ref[...]
    g_all = -jnp.exp(alog_ref[...]) * _softplus(ab + dtb_ref[...])
    beta_all = jax.nn.sigmoid(ab)
    ri = lax.broadcasted_iota(jnp.int32, (tt, tt), 0)
    ci = lax.broadcasted_iota(jnp.int32, (tt, tt), 1)
    shift = CHUNK.bit_length() - 1
    tri = jnp.where(((ri >> shift) == (ci >> shift)) & (ci <= ri), 1.0, 0.0).astype(BF16)
    g_hi, g_mid, g_lo = _split3(g_all)
    gc_all = _dot(tri, g_hi) + _dot(tri, g_mid) + _dot(tri, g_lo)
    gct_ref[...] = gc_all.T

    lane = lax.broadcasted_iota(jnp.int32, (tt, LANES), 1)
    r128 = lax.broadcasted_iota(jnp.int32, (CHUNK, CHUNK), 0)
    c128 = lax.broadcasted_iota(jnp.int32, (CHUNK, CHUNK), 1)
    sub8 = lax.broadcasted_iota(jnp.int32, (SUBLANES, CHUNK), 0)
    n_chunks = tt // CHUNK
    chains = [(hh, c) for hh in range(hg) for c in range(n_chunks)]
    gcol_heads, bcol_heads = [], []
    for hh in range(hg):
        head = hgi * hg + hh
        gcol_heads.append(jnp.sum(jnp.where(lane == head, gc_all, 0.0), axis=-1, keepdims=True))
        bcol_heads.append(jnp.sum(jnp.where(lane == head + HEADS, beta_all, 0.0), axis=-1, keepdims=True))

    qn_l, kn_l, kb_l, rhs_l, eg_l, gcol_l, dec_l = [], [], [], [], [], [], []
    for hh, c in chains:
        head = hgi * hg + hh
        rows = slice(c * CHUNK, (c + 1) * CHUNK)
        lanes = slice(hh * HEAD_DIM, (hh + 1) * HEAD_DIM)
        qh, kh, vh = qc[rows, lanes], kc[rows, lanes], vc[rows, lanes]
        qn = qh * lax.rsqrt(jnp.sum(qh * qh, axis=-1, keepdims=True) + EPS) * (HEAD_DIM ** -0.5)
        kn = kh * lax.rsqrt(jnp.sum(kh * kh, axis=-1, keepdims=True) + EPS)
        gcol = gcol_heads[hh][rows]
        bcol = bcol_heads[hh][rows]
        gblk = gct_ref[pl.ds(pl.multiple_of((head >> 3) * SUBLANES, SUBLANES), SUBLANES), rows]
        grow = jnp.sum(jnp.where(sub8 == (head & (SUBLANES - 1)), gblk, 0.0), axis=0, keepdims=True)
        eg = jnp.exp(gcol)
        kb = kn * bcol
        qn_l.append(qn)
        kn_l.append(kn)
        kb_l.append(kb)
        eg_l.append(eg)
        gcol_l.append(gcol)
        dec_l.append(jnp.exp(jnp.minimum(gcol - grow, 0.0)))
        rhs_l.append(jnp.concatenate([vh * bcol, kb * eg], axis=1))

    a_l, qk_l = [], []
    for qn, kn, kb, decay in zip(qn_l, kn_l, kb_l, dec_l):
        knb = kn.astype(BF16)
        a_l.append(_dot_nt(kb.astype(BF16), knb) * jnp.where(c128 < r128, decay, 0.0))
        qk_l.append(_dot_nt(qn.astype(BF16), knb) * jnp.where(c128 <= r128, decay, 0.0))
    tinv_l = [x.astype(BF16) for x in _unit_lower_inverse_minus_identity(a_l)]
    sol_l = [rhs + _dot(tinv, rhs.astype(BF16)) for tinv, rhs in zip(tinv_l, rhs_l)]
    res_l = []
    for a, sol, rhs in zip(a_l, sol_l, rhs_l):
        a_sol = _dot(a.astype(BF16), sol.astype(BF16))
        res_l.append((rhs - sol) - a_sol)
    sol_l = [(sol + res + _dot(tinv, res.astype(BF16))).astype(BF16) for sol, res, tinv in zip(sol_l, res_l, tinv_l)]
    lhs_l, o_add_l, s_add_l, glast_l = [], [], [], []
    for qn, kn, eg, gcol, qk, sol_b in zip(qn_l, kn_l, eg_l, gcol_l, qk_l, sol_l):
        glast = gcol[CHUNK - 1:CHUNK, :]
        ktail = kn * jnp.exp(glast - gcol)
        kt_sol = _dot(ktail.T.astype(BF16), sol_b)
        qk_sol = _dot(qk.astype(BF16), sol_b)
        lhs_l.append(jnp.concatenate([qn * eg - qk_sol[:, HEAD_DIM:], kt_sol[:, HEAD_DIM:]], axis=0).astype(BF16))
        o_add_l.append(qk_sol[:, :HEAD_DIM])
        s_add_l.append(kt_sol[:, :HEAD_DIM])
        glast_l.append(jnp.exp(glast))
    states = [s_ref[hh] for hh in range(hg)]
    for c in range(n_chunks):
        for hh in range(hg):
            i = hh * n_chunks + c
            rows = slice(c * CHUNK, (c + 1) * CHUNK)
            lanes = slice(hh * HEAD_DIM, (hh + 1) * HEAD_DIM)
            on_s = _dot(lhs_l[i], states[hh].astype(BF16))
            o = on_s[:CHUNK] + o_add_l[i]
            states[hh] = states[hh] * glast_l[i] - on_s[CHUNK:] + s_add_l[i]
            o_ref[rows, lanes] = (_rms(o, gn_ref[...]) * _silu(z_ref[rows, lanes])).astype(BF16)
    for hh in range(hg):
        s_ref[hh] = states[hh]

    @pl.when(t == pl.num_programs(2) - 1)
    def _():
        ssm_ref[0, 0] = s_ref[...]


def _gdn_prompt(proj, ab, conv_w, alog_pad, dtb_pad, gdn_norm, *, batch, seq, hg, tt):
    nt = seq // tt
    w = hg * HEAD_DIM
    body = functools.partial(_gdn_prompt_body, hg=hg, tt=tt)

    def col_spec(col0):
        return pl.BlockSpec((tt, w), lambda b, h, t: (b * nt + t, col0 // w + h))

    def cw_spec(part):
        return pl.BlockSpec((CONV_WIDTH, w), lambda b, h, t: (0, part * GDN_DIM // w + h))

    small = pl.BlockSpec((1, LANES), lambda b, h, t: (0, 0))
    return pl.pallas_call(
        body,
        grid=(batch, HEADS // hg, nt),
        in_specs=[col_spec(C_Q), col_spec(C_K), col_spec(C_V), col_spec(C_Z),
                  pl.BlockSpec((tt, LANES), lambda b, h, t: (b * nt + t, 0)),
                  cw_spec(0), cw_spec(1), cw_spec(2), small, small, small],
        out_specs=[
            pl.BlockSpec((tt, w), lambda b, h, t: (b * nt + t, h)),
            pl.BlockSpec((1, 1, hg, HEAD_DIM, HEAD_DIM), lambda b, h, t: (0, b, h, 0, 0)),
        ],
        out_shape=[
            jax.ShapeDtypeStruct((batch * seq, GDN_DIM), BF16),
            jax.ShapeDtypeStruct((1, batch, HEADS, HEAD_DIM, HEAD_DIM), F32),
        ],
        scratch_shapes=[
            pltpu.VMEM((SUBLANES + tt, 3 * w), F32),
            pltpu.VMEM((hg, HEAD_DIM, HEAD_DIM), F32),
            pltpu.VMEM((LANES, tt), F32),
        ],
        compiler_params=_cparams("parallel", "parallel", "arbitrary"),
        name="gdn_prompt",
    )(proj, proj, proj, proj, ab, conv_w, conv_w, conv_w, alog_pad, dtb_pad, gdn_norm)


def _gdn_sample_prep_body(q_ref, k_ref, v_ref, ab_ref, st_ref, cw_ref, alog_ref, dtb_ref,
                          qn_ref, kn_ref, vc_ref, eg_ref, beta_ref, qk_ref):
    def conv(part, u_ref):
        cols = slice(part * GDN_DIM, (part + 1) * GDN_DIM)
        acc = u_ref[...] * cw_ref[CONV_BUF:CONV_WIDTH, cols]
        for i in range(CONV_BUF):
            acc = acc + st_ref[i, :, cols] * cw_ref[i:i + 1, cols]
        return _silu(acc)

    qc = conv(0, q_ref)
    kc = conv(1, k_ref)
    vc_ref[...] = conv(2, v_ref)
    ab = ab_ref[...]
    g_all = -jnp.exp(alog_ref[...]) * _softplus(ab + dtb_ref[...])
    eg_all = jnp.exp(g_all)
    beta_all = jax.nn.sigmoid(ab)
    for h in range(HEADS):
        lanes = slice(h * HEAD_DIM, (h + 1) * HEAD_DIM)
        qh, kh = qc[:, lanes], kc[:, lanes]
        qn = qh * lax.rsqrt(jnp.sum(qh * qh, axis=-1, keepdims=True) + EPS) * (HEAD_DIM ** -0.5)
        kn = kh * lax.rsqrt(jnp.sum(kh * kh, axis=-1, keepdims=True) + EPS)
        qn_ref[:, lanes] = qn
        kn_ref[:, lanes] = kn
        shape = qn.shape
        qk_ref[:, lanes] = jnp.broadcast_to(jnp.sum(qn * kn, axis=-1, keepdims=True), shape)
        eg_ref[:, lanes] = jnp.broadcast_to(eg_all[:, h:h + 1], shape)
        beta_ref[:, lanes] = jnp.broadcast_to(beta_all[:, HEADS + h:HEADS + h + 1], shape)


def _gdn_sample_prep(proj, ab, conv_state_t, conv_w, alog_pad, dtb_pad):
    nb = proj.shape[0]

    def col_spec(col0):
        return pl.BlockSpec((nb, GDN_DIM), lambda i: (0, col0 // GDN_DIM))

    small = pl.BlockSpec((1, LANES), lambda i: (0, 0))
    full = pl.BlockSpec((nb, GDN_DIM), lambda i: (0, 0))
    return pl.pallas_call(
        _gdn_sample_prep_body,
        grid=(1,),
        in_specs=[col_spec(C_Q), col_spec(C_K), col_spec(C_V),
                  pl.BlockSpec((nb, LANES), lambda i: (0, 0)),
                  pl.BlockSpec((CONV_BUF, nb, QKV_DIM), lambda i: (0, 0, 0)),
                  pl.BlockSpec((CONV_WIDTH, QKV_DIM), lambda i: (0, 0)),
                  small, small],
        out_specs=[full] * 6,
        out_shape=[jax.ShapeDtypeStruct((nb, GDN_DIM), F32)] * 6,
        compiler_params=_cparams("arbitrary"),
        name="gdn_sample_prep",
    )(proj, proj, proj, ab, conv_state_t, conv_w, alog_pad, dtb_pad)


def _gdn_sample_step_body(qn_ref, kn_ref, vc_ref, eg_ref, beta_ref, qk_ref, z_ref, gn_ref, s_ref,
                          o_ref, snew_ref, *, bb):
    pad = jnp.zeros((LANES - HEADS, HEAD_DIM), F32)
    for b in range(bb):
        k16, q16 = kn_ref[b], qn_ref[b]
        cols = jnp.concatenate([k16, pad], axis=0).T
        kq = jnp.concatenate([k16, q16], axis=0).astype(BF16)
        outs = []
        for h in range(HEADS):
            kcol = cols[:, h:h + 1]
            s = s_ref[0, b, h]
            kq_s = _dot(kq, s.astype(BF16))
            k_s = kq_s[h:h + 1]
            q_s = kq_s[HEADS + h:HEADS + h + 1]
            eg = eg_ref[b, h:h + 1, :]
            v_new = beta_ref[b, h:h + 1, :] * (vc_ref[b, h:h + 1, :] - eg * k_s)
            snew_ref[0, b, h] = eg * s + kcol * v_new
            outs.append(eg * q_s + qk_ref[b, h:h + 1, :] * v_new)
        o = jnp.concatenate(outs, axis=0)
        o_ref[b] = (_rms(o, gn_ref[...]) * _silu(z_ref[b])).astype(BF16)


def _gdn_sample_step(qn, kn, vc, eg, beta, qk, z, gdn_norm, state_ssm, *, bb):
    nb = qn.shape[0]
    body = functools.partial(_gdn_sample_step_body, bb=bb)
    vec = pl.BlockSpec((bb, HEADS, HEAD_DIM), lambda i: (i, 0, 0))
    st = pl.BlockSpec((1, bb, HEADS, HEAD_DIM, HEAD_DIM), lambda i: (0, i, 0, 0, 0))
    return pl.pallas_call(
        body,
        grid=(nb // bb,),
        in_specs=[vec] * 7 + [pl.BlockSpec((1, LANES), lambda i: (0, 0)), st],
        out_specs=[vec, st],
        out_shape=[jax.ShapeDtypeStruct((nb, HEADS, HEAD_DIM), BF16),
                   jax.ShapeDtypeStruct(state_ssm.shape, F32)],
        compiler_params=_cparams("parallel"),
        name="gdn_sample_step",
    )(qn, kn, vc, eg, beta, qk, z, gdn_norm, state_ssm)


def _merge_body(pool_ref, gdn_ref, wpu_ref, wgu_ref, gp_ref, gg_ref, o_ref):
    o_ref[...] = (jax.nn.sigmoid(gp_ref[...]) * _dot(pool_ref[...], wpu_ref[...])
                  + jax.nn.sigmoid(gg_ref[...]) * _dot(gdn_ref[...], wgu_ref[...])).astype(BF16)


def _merge(proj, pool_out, gdn_out, w_pool_up, w_gdn_up, *, tm, tn=512):
    m = pool_out.shape[0]
    nj = D_MODEL // tn
    return pl.pallas_call(
        _merge_body,
        grid=(m // tm, nj),
        in_specs=[
            pl.BlockSpec((tm, POOL_DIM), lambda i, j: (i, 0)),
            pl.BlockSpec((tm, GDN_DIM), lambda i, j: (i, 0)),
            pl.BlockSpec((POOL_DIM, tn), lambda i, j: (0, j)),
            pl.BlockSpec((GDN_DIM, tn), lambda i, j: (0, j)),
            pl.BlockSpec((tm, tn), lambda i, j: (i, C_GP // tn + j)),
            pl.BlockSpec((tm, tn), lambda i, j: (i, C_GG // tn + j)),
        ],
        out_specs=pl.BlockSpec((tm, tn), lambda i, j: (i, j)),
        out_shape=jax.ShapeDtypeStruct((m, D_MODEL), BF16),
        compiler_params=_cparams("parallel", "parallel"),
        name="merge",
    )(pool_out, gdn_out, w_pool_up, w_gdn_up, proj, proj)


def _outproj_body(a_ref, w_ref, x_ref, o_ref):
    o_ref[...] = x_ref[...] + _dot(a_ref[...], w_ref[...])


def _outproj(merged, x, w_o, *, tm, tn=512):
    m = x.shape[0]
    return pl.pallas_call(
        _outproj_body,
        grid=(m // tm, D_MODEL // tn),
        in_specs=[
            pl.BlockSpec((tm, D_MODEL), lambda i, j: (i, 0)),
            pl.BlockSpec((D_MODEL, tn), lambda i, j: (0, j)),
            pl.BlockSpec((tm, tn), lambda i, j: (i, j)),
        ],
        out_specs=pl.BlockSpec((tm, tn), lambda i, j: (i, j)),
        out_shape=jax.ShapeDtypeStruct((m, D_MODEL), F32),
        compiler_params=_cparams("parallel", "parallel"),
        name="outproj",
    )(merged, w_o, x)


def _ffn_body(x_ref, gain_ref, wg_ref, wu_ref, wd_ref, o_ref, h_ref):
    f = pl.program_id(1)

    @pl.when(f == 0)
    def _():
        x = x_ref[...]
        h_ref[...] = _rms(x, gain_ref[...]).astype(BF16)
        o_ref[...] = x

    h = h_ref[...]
    act = (_silu(_dot(h, wg_ref[...].astype(BF16))) * _dot(h, wu_ref[...].astype(BF16))).astype(BF16)
    o_ref[...] += _dot(act, wd_ref[...].astype(BF16))


def _ffn(x, gain, w_gate_up, w_down, *, tm, tf):
    m = x.shape[0]
    nf = D_FF // tf
    return pl.pallas_call(
        _ffn_body,
        grid=(m // tm, nf),
        in_specs=[
            pl.BlockSpec((tm, D_MODEL), lambda i, f: (i, 0)),
            pl.BlockSpec((1, D_MODEL), lambda i, f: (0, 0)),
            pl.BlockSpec((D_MODEL, tf), lambda i, f: (0, f)),
            pl.BlockSpec((D_MODEL, tf), lambda i, f: (0, nf + f)),
            pl.BlockSpec((tf, D_MODEL), lambda i, f: (f, 0)),
        ],
        out_specs=pl.BlockSpec((tm, D_MODEL), lambda i, f: (i, 0)),
        out_shape=jax.ShapeDtypeStruct((m, D_MODEL), F32),
        scratch_shapes=[pltpu.VMEM((tm, D_MODEL), BF16)],
        compiler_params=_cparams("parallel", "arbitrary"),
        name="ffn",
    )(x, gain, w_gate_up, w_gate_up, w_down)


def _final_body(x_ref, p_ref, wple_ref, wpg_ref, gain_ref, o_ref):
    x = x_ref[...]
    gate = jax.nn.sigmoid(_dot(x.astype(BF16), wpg_ref[...]))
    emb = _dot(p_ref[...].astype(BF16), wple_ref[...])
    o_ref[...] = _rms(x + emb * gate, gain_ref[...])


def _final(x, p, w_ple, w_ple_gate, gain, *, tm):
    m = x.shape[0]
    return pl.pallas_call(
        _final_body,
        grid=(m // tm,),
        in_specs=[
            pl.BlockSpec((tm, D_MODEL), lambda i: (i, 0)),
            pl.BlockSpec((tm, PLE_DIM), lambda i: (i, 0)),
            pl.BlockSpec((PLE_DIM, D_MODEL), lambda i: (0, 0)),
            pl.BlockSpec((D_MODEL, D_MODEL), lambda i: (0, 0)),
            pl.BlockSpec((1, D_MODEL), lambda i: (0, 0)),
        ],
        out_specs=pl.BlockSpec((tm, D_MODEL), lambda i: (i, 0)),
        out_shape=jax.ShapeDtypeStruct((m, D_MODEL), F32),
        compiler_params=_cparams("parallel"),
        name="final",
    )(x, p, w_ple, w_ple_gate, gain)


def _tile_rows(m, want):
    return want if m % want == 0 else m


def _dense_tail(x, p, proj, pool_out, gdn_out, wts):
    m = x.shape[0]
    tm = _tile_rows(m, 1024)
    merged = _merge(proj, pool_out, gdn_out, wts["w_pool_up"], wts["w_gdn_up"], tm=tm)
    x = _outproj(merged, x, wts["w_o"], tm=tm)
    x = _ffn(x, wts["norm_ffn"], wts["w_gate_up"], wts["w_down"], tm=tm, tf=256)
    return _final(x, p, wts["w_ple"], wts["w_ple_gate"], wts["norm_final"], tm=_tile_rows(m, 512))


def kernel(x_prompt, x_sample, p_prompt, p_sample, state_pool, state_conv, state_ssm, norm_mix, w_in, pool_w,
           pool_scale, conv_w, a_log, dt_bias, gdn_norm, w_pool_up, w_gdn_up, w_o, norm_ffn, w_gate_up, w_down,
           w_ple, w_ple_gate, norm_final):
    batch, seq, _ = x_prompt.shape
    nb = x_sample.shape[0]
    w_in_t = jnp.swapaxes(w_in[0], 0, 1)
    wts = dict(
        w_pool_up=w_pool_up[0].astype(BF16), w_gdn_up=w_gdn_up[0].astype(BF16), w_o=w_o[0].astype(BF16),
        norm_ffn=norm_ffn, w_gate_up=w_gate_up[0], w_down=w_down[0],
        w_ple=w_ple[0].astype(BF16), w_ple_gate=w_ple_gate[0].astype(BF16), norm_final=norm_final[None, :],
    )
    pool_w_b = pool_w[0].astype(BF16)
    lane_pad = ((0, 0), (0, LANES - HEADS))
    alog_pad = jnp.pad(a_log, lane_pad)
    dtb_pad = jnp.pad(dt_bias, lane_pad)
    gn = gdn_norm

    mp = batch * seq
    xp = x_prompt.reshape(mp, D_MODEL)
    xs = x_sample.reshape(nb, D_MODEL)
    h_p, ab_p = _norm_in(xp, norm_mix, w_in_t, tm=_tile_rows(mp, 512))
    h_s, ab_s = _norm_in(xs, norm_mix, w_in_t, tm=nb)
    tm_in = _tile_rows(mp, 1024)
    proj_p, proj_s = _inproj(h_p, h_s, w_in_t, tm=tm_in, tn=1024)

    pool_out_p = _pool_prompt(proj_p, pool_w_b, pool_scale, batch=batch, seq=seq, tt=_tile_rows(seq, 512))
    gdn_out_p, ssm_p = _gdn_prompt(proj_p, ab_p, conv_w[0], alog_pad, dtb_pad, gn, batch=batch, seq=seq,
                                   hg=8, tt=_tile_rows(seq, 256))
    y_p = _dense_tail(xp, p_prompt[0].reshape(mp, PLE_DIM), proj_p, pool_out_p, gdn_out_p, wts)
    proj_p3 = proj_p.reshape(batch, seq, C_ALL)
    pool_p = proj_p3[:, seq - POOL_BUF:, C_POOL:C_END][None]
    conv_p = proj_p3[:, seq - CONV_BUF:, C_Q:C_Z][None]

    pool_out_s = _pool_sample(proj_s, jnp.swapaxes(state_pool[0], 0, 1), pool_w_b, pool_scale)
    prep = _gdn_sample_prep(proj_s, ab_s, jnp.swapaxes(state_conv[0], 0, 1), conv_w[0], alog_pad, dtb_pad)
    to_heads = lambda a: a.reshape(nb, HEADS, HEAD_DIM)
    gdn_out_s, ssm_s = _gdn_sample_step(*[to_heads(a) for a in prep], to_heads(proj_s[:, C_Z:C_POOL]), gn,
                                        state_ssm, bb=4)
    y_s = _dense_tail(xs, p_sample[0].reshape(nb, PLE_DIM), proj_s, pool_out_s, gdn_out_s.reshape(nb, GDN_DIM), wts)
    pool_s = jnp.concatenate([state_pool[0][:, 1:], proj_s[:, None, C_POOL:C_END]], axis=1)[None]
    conv_s = jnp.concatenate([state_conv[0][:, 1:], proj_s[:, None, C_Q:C_Z]], axis=1)[None]

    return (y_p.reshape(batch, seq, D_MODEL), y_s.reshape(nb, 1, D_MODEL), pool_p, conv_p, ssm_p,
            pool_s, conv_s, ssm_s)
```

```python
import functools

import jax
import jax.numpy as jnp
from jax import lax
from jax.experimental import pallas as pl
from jax.experimental.pallas import tpu as pltpu

F32 = jnp.float32
BF16 = jnp.bfloat16

D_MODEL = 2048
PAST_LEN = 16384
POOL_WINDOWS = (2, 4, 8, 16)
POOL_GROUP_DIM = D_MODEL // 8
POOL_DIM = len(POOL_WINDOWS) * POOL_GROUP_DIM
POOL_BUF = max(POOL_WINDOWS) - 1
HEAD_DIM = 128
HEADS = D_MODEL // HEAD_DIM
GDN_DIM = HEADS * HEAD_DIM
QKV_DIM = 3 * GDN_DIM
CONV_WIDTH = 4
CONV_BUF = CONV_WIDTH - 1
D_FF = -(-8 * D_MODEL // (3 * 256)) * 256
PLE_DIM = 256
EPS = 1e-6

IN_QKV = POOL_DIM
IN_Z = IN_QKV + QKV_DIM
IN_AB = IN_Z + GDN_DIM
IN_GATES = IN_AB + 2 * HEADS
C_Q = 0
C_K = C_Q + GDN_DIM
C_V = C_K + GDN_DIM
C_Z = C_V + GDN_DIM
C_POOL = C_Z + GDN_DIM
C_END = C_POOL + POOL_DIM
C_GP = C_END
C_GG = C_GP + D_MODEL
C_ALL = C_GG + D_MODEL

LANES = 128
SUBLANES = 8
VMEM_LIMIT_BYTES = 56 * 1024 * 1024
CHUNK = 128


def _cparams(*sem):
    return pltpu.CompilerParams(dimension_semantics=sem, vmem_limit_bytes=VMEM_LIMIT_BYTES)


def _rms(x, gain):
    return x * lax.rsqrt(jnp.mean(x * x, axis=-1, keepdims=True) + EPS) * gain


def _dot(a, b):
    return jnp.dot(a, b, preferred_element_type=F32)


def _dot_nt(a, b):
    return lax.dot_general(a, b, (((1,), (1,)), ((), ())), preferred_element_type=F32)


def _silu(x):
    return x * jax.nn.sigmoid(x)


def _norm_in_body(x_ref, gain_ref, wab_ref, h_ref, ab_ref):
    h = _rms(x_ref[...], gain_ref[...]).astype(BF16)
    h_ref[...] = h
    ab_ref[...] = _dot_nt(h, wab_ref[...].astype(BF16))


def _norm_in(x, gain, w_in_t, *, tm):
    m = x.shape[0]
    return pl.pallas_call(
        _norm_in_body,
        grid=(m // tm,),
        in_specs=[
            pl.BlockSpec((tm, D_MODEL), lambda i: (i, 0)),
            pl.BlockSpec((1, D_MODEL), lambda i: (0, 0)),
            pl.BlockSpec((LANES, D_MODEL), lambda i: (IN_AB // LANES, 0)),
        ],
        out_specs=[
            pl.BlockSpec((tm, D_MODEL), lambda i: (i, 0)),
            pl.BlockSpec((tm, LANES), lambda i: (i, 0)),
        ],
        out_shape=[jax.ShapeDtypeStruct((m, D_MODEL), BF16), jax.ShapeDtypeStruct((m, LANES), F32)],
        compiler_params=_cparams("parallel"),
        name="norm_in",
    )(x, gain, w_in_t)


def _inproj_body(hp_ref, hs_ref, w_ref, op_ref, os_ref, wb_ref):
    i = pl.program_id(1)

    @pl.when(i == 0)
    def _():
        wb_ref[...] = w_ref[...].astype(BF16)
        os_ref[...] = _dot_nt(hs_ref[...], wb_ref[...])

    @pl.when(i > 0)
    def _():
        op_ref[...] = _dot_nt(hp_ref[...], wb_ref[...])


def _inproj(h_prompt, h_sample, w_in_t, *, tm, tn):
    mp, ms = h_prompt.shape[0], h_sample.shape[0]
    n_p = mp // tm
    n_out, out_dtype = C_ALL, F32
    n_mid = (QKV_DIM + GDN_DIM) // tn
    n_pool = POOL_DIM // tn

    def row_start(j):
        return jnp.where(j < n_mid, IN_QKV + j * tn,
                         jnp.where(j < n_mid + n_pool, (j - n_mid) * tn, IN_GATES + (j - n_mid - n_pool) * tn))

    return pl.pallas_call(
        _inproj_body,
        grid=(n_out // tn, n_p + 1),
        in_specs=[
            pl.BlockSpec((tm, D_MODEL), lambda j, i: (jnp.maximum(i - 1, 0), 0)),
            pl.BlockSpec((ms, D_MODEL), lambda j, i: (0, 0)),
            pl.BlockSpec((pl.Element(tn), pl.Element(D_MODEL)),
                         lambda j, i: (pl.multiple_of(row_start(j), SUBLANES), 0)),
        ],
        out_specs=[pl.BlockSpec((tm, tn), lambda j, i: (jnp.maximum(i - 1, 0), j)),
                   pl.BlockSpec((ms, tn), lambda j, i: (0, j))],
        out_shape=[jax.ShapeDtypeStruct((mp, n_out), out_dtype), jax.ShapeDtypeStruct((ms, n_out), out_dtype)],
        scratch_shapes=[pltpu.VMEM((tn, D_MODEL), BF16)],
        compiler_params=_cparams("arbitrary", "arbitrary"),
        name="inproj",
    )(h_prompt, h_sample, w_in_t)


def _pool_group_out(mean, tok, pw_ref, scale_ref, gi):
    cols = slice(gi * POOL_GROUP_DIM, (gi + 1) * POOL_GROUP_DIM)
    y = _dot((mean - tok).astype(BF16), pw_ref[gi])
    return (y * scale_ref[:, cols]).astype(BF16)


def _pool_prompt_body(u_ref, pw_ref, scale_ref, o_ref, ext_ref, *, tt, start_pos):
    t = pl.program_id(1)
    halo = POOL_BUF + 1

    @pl.when(t == 0)
    def _():
        ext_ref[0:halo, :] = jnp.zeros((halo, POOL_DIM), F32)

    ext_ref[halo:halo + tt, :] = u_ref[...]
    pos = start_pos + t * tt + lax.broadcasted_iota(jnp.int32, (tt, 1), 0)
    for gi, w in enumerate(POOL_WINDOWS):
        cols = slice(gi * POOL_GROUP_DIM, (gi + 1) * POOL_GROUP_DIM)
        tok = ext_ref[halo:halo + tt, cols]
        tot = tok
        for i in range(1, w):
            tot = tot + ext_ref[pl.ds(halo - i, tt), cols]
        cnt = jnp.minimum(pos + 1, w).astype(F32)
        o_ref[:, cols] = _pool_group_out(tot / cnt, tok, pw_ref, scale_ref, gi)
    ext_ref[0:halo, :] = ext_ref[tt:tt + halo, :]


def _pool_prompt(proj, pool_w, pool_scale, *, batch, seq, tt):
    nt = seq // tt
    body = functools.partial(_pool_prompt_body, tt=tt, start_pos=0)
    return pl.pallas_call(
        body,
        grid=(batch, nt),
        in_specs=[
            pl.BlockSpec((tt, POOL_DIM), lambda b, t: (b * nt + t, C_POOL // POOL_DIM)),
            pl.BlockSpec((len(POOL_WINDOWS), POOL_GROUP_DIM, POOL_GROUP_DIM), lambda b, t: (0, 0, 0)),
            pl.BlockSpec((1, POOL_DIM), lambda b, t: (0, 0)),
        ],
        out_specs=pl.BlockSpec((tt, POOL_DIM), lambda b, t: (b * nt + t, 0)),
        out_shape=jax.ShapeDtypeStruct((batch * seq, POOL_DIM), BF16),
        scratch_shapes=[pltpu.VMEM((POOL_BUF + 1 + tt, POOL_DIM), F32)],
        compiler_params=_cparams("parallel", "arbitrary"),
        name="pool_prompt",
    )(proj, pool_w, pool_scale)


def _pool_sample_body(u_ref, st_ref, pw_ref, scale_ref, o_ref, *, start_pos):
    for gi, w in enumerate(POOL_WINDOWS):
        cols = slice(gi * POOL_GROUP_DIM, (gi + 1) * POOL_GROUP_DIM)
        tok = u_ref[:, cols]
        tot = tok
        for i in range(1, w):
            tot = tot + st_ref[POOL_BUF - i, :, cols]
        cnt = float(min(start_pos + 1, w))
        o_ref[:, cols] = _pool_group_out(tot / cnt, tok, pw_ref, scale_ref, gi)


def _pool_sample(proj, state_t, pool_w, pool_scale):
    nb = proj.shape[0]
    body = functools.partial(_pool_sample_body, start_pos=PAST_LEN)
    return pl.pallas_call(
        body,
        grid=(1,),
        in_specs=[
            pl.BlockSpec((nb, POOL_DIM), lambda i: (0, C_POOL // POOL_DIM)),
            pl.BlockSpec((POOL_BUF, nb, POOL_DIM), lambda i: (0, 0, 0)),
            pl.BlockSpec((len(POOL_WINDOWS), POOL_GROUP_DIM, POOL_GROUP_DIM), lambda i: (0, 0, 0)),
            pl.BlockSpec((1, POOL_DIM), lambda i: (0, 0)),
        ],
        out_specs=pl.BlockSpec((nb, POOL_DIM), lambda i: (0, 0)),
        out_shape=jax.ShapeDtypeStruct((nb, POOL_DIM), BF16),
        compiler_params=_cparams("arbitrary"),
        name="pool_sample",
    )(proj, state_t, pool_w, pool_scale)


def _split3(x):
    hi = x.astype(BF16)
    r1 = x - hi.astype(F32)
    mid = r1.astype(BF16)
    lo = (r1 - mid.astype(F32)).astype(BF16)
    return hi, mid, lo


def _softplus(x):
    return jnp.maximum(x, 0.0) + jnp.log1p(jnp.exp(-jnp.abs(x)))


def _unit_lower_inverse_minus_identity(mats):
    eye = jnp.where(lax.broadcasted_iota(jnp.int32, (CHUNK, CHUNK), 0)
                    == lax.broadcasted_iota(jnp.int32, (CHUNK, CHUNK), 1), 1.0, 0.0)
    levels = CHUNK.bit_length() - 1
    qbs = [(-a).astype(BF16) for a in mats]
    rs = [eye - a for a in mats]
    qs = [_dot(qb, qb) for qb in qbs]
    for j in range(1, levels):
        qbs = [q.astype(BF16) for q in qs]
        rbs = [r.astype(BF16) for r in rs]
        if j < levels - 1:
            ps = [_dot(qb, jnp.concatenate([qb, rb], axis=1)) for qb, rb in zip(qbs, rbs)]
            rs = [r + p[:, CHUNK:] for r, p in zip(rs, ps)]
            qs = [p[:, :CHUNK] for p in ps]
        else:
            rs = [r + _dot(qb, rb) for r, qb, rb in zip(rs, qbs, rbs)]
    return [r - eye for r in rs]


def _gdn_prompt_body(q_ref, k_ref, v_ref, z_ref, ab_ref, cwq_ref, cwk_ref, cwv_ref, alog_ref, dtb_ref, gn_ref,
                     o_ref, ssm_ref, ext_ref, s_ref, gct_ref, *, hg, tt):
    hgi = pl.program_id(1)
    t = pl.program_id(2)
    w = hg * HEAD_DIM
    halo = SUBLANES

    @pl.when(t == 0)
    def _():
        ext_ref[0:halo, :] = jnp.zeros((halo, 3 * w), F32)
        s_ref[...] = jnp.zeros_like(s_ref)

    ext_ref[halo:halo + tt, 0:w] = q_ref[...]
    ext_ref[halo:halo + tt, w:2 * w] = k_ref[...]
    ext_ref[halo:halo + tt, 2 * w:3 * w] = v_ref[...]

    def conv(part, cw_ref):
        cols = slice(part * w, (part + 1) * w)
        acc = ext_ref[halo:halo + tt, cols] * cw_ref[CONV_BUF:CONV_WIDTH, :]
        for i in range(CONV_BUF):
            acc = acc + ext_ref[pl.ds(halo - CONV_BUF + i, tt), cols] * cw_ref[i:i + 1, :]
        return _silu(acc)

    qc = conv(0, cwq_ref)
    kc = conv(1, cwk_ref)
    vc = conv(2, cwv_ref)
    ext_ref[0:halo, :] = ext_ref[tt:tt + halo, :]

    ab = ab_ref[...]
    g_all = -jnp.exp(alog_ref[...]) * _softplus(ab + dtb_ref[...])
    beta_all = jax.nn.sigmoid(ab)
    ri = lax.broadcasted_iota(jnp.int32, (tt, tt), 0)
    ci = lax.broadcasted_iota(jnp.int32, (tt, tt), 1)
    shift = CHUNK.bit_length() - 1
    tri = jnp.where(((ri >> shift) == (ci >> shift)) & (ci <= ri), 1.0, 0.0).astype(BF16)
    g_hi, g_mid, g_lo = _split3(g_all)
    gc_all = _dot(tri, g_hi) + _dot(tri, g_mid) + _dot(tri, g_lo)
    gct_ref[...] = gc_all.T

    lane = lax.broadcasted_iota(jnp.int32, (tt, LANES), 1)
    r128 = lax.broadcasted_iota(jnp.int32, (CHUNK, CHUNK), 0)
    c128 = lax.broadcasted_iota(jnp.int32, (CHUNK, CHUNK), 1)
    sub8 = lax.broadcasted_iota(jnp.int32, (SUBLANES, CHUNK), 0)
    n_chunks = tt // CHUNK
    chains = [(hh, c) for hh in range(hg) for c in range(n_chunks)]
    gcol_heads, bcol_heads = [], []
    for hh in range(hg):
        head = hgi * hg + hh
        gcol_heads.append(jnp.sum(jnp.where(lane == head, gc_all, 0.0), axis=-1, keepdims=True))
        bcol_heads.append(jnp.sum(jnp.where(lane == head + HEADS, beta_all, 0.0), axis=-1, keepdims=True))

    qn_l, kn_l, kb_l, rhs_l, eg_l, gcol_l, dec_l = [], [], [], [], [], [], []
    for hh, c in chains:
        head = hgi * hg + hh
        rows = slice(c * CHUNK, (c + 1) * CHUNK)
        lanes = slice(hh * HEAD_DIM, (hh + 1) * HEAD_DIM)
        qh, kh, vh = qc[rows, lanes], kc[rows, lanes], vc[rows, lanes]
        qn = qh * lax.rsqrt(jnp.sum(qh * qh, axis=-1, keepdims=True) + EPS) * (HEAD_DIM ** -0.5)
        kn = kh * lax.rsqrt(jnp.sum(kh * kh, axis=-1, keepdims=True) + EPS)
        gcol = gcol_heads[hh][rows]
        bcol = bcol_heads[hh][rows]
        gblk = gct_ref[pl.ds(pl.multiple_of((head // SUBLANES) * SUBLANES, SUBLANES), SUBLANES), rows]
        grow = jnp.sum(jnp.where(sub8 == (head & (SUBLANES - 1)), gblk, 0.0), axis=0, keepdims=True)
        eg = jnp.exp(gcol)
        kb = kn * bcol
        qn_l.append(qn)
        kn_l.append(kn)
        kb_l.append(kb)
        eg_l.append(eg)
        gcol_l.append(gcol)
        dec_l.append(jnp.exp(jnp.minimum(gcol - grow, 0.0)))
        rhs_l.append(jnp.concatenate([vh * bcol, kb * eg], axis=1))

    a_l, qk_l = [], []
    for qn, kn, kb, decay in zip(qn_l, kn_l, kb_l, dec_l):
        knb = kn.astype(BF16)
        a_l.append(_dot_nt(kb.astype(BF16), knb) * jnp.where(c128 < r128, decay, 0.0))
        qk_l.append(_dot_nt(qn.astype(BF16), knb) * jnp.where(c128 <= r128, decay, 0.0))
    tinv_l = [x.astype(BF16) for x in _unit_lower_inverse_minus_identity(a_l)]
    sol_l = [rhs + _dot(tinv, rhs.astype(BF16)) for tinv, rhs in zip(tinv_l, rhs_l)]
    res_l = []
    for a, sol, rhs in zip(a_l, sol_l, rhs_l):
        a_sol = _dot(a.astype(BF16), sol.astype(BF16))
        res_l.append((rhs - sol) - a_sol)
    sol_l = [(sol + res + _dot(tinv, res.astype(BF16))).astype(BF16) for sol, res, tinv in zip(sol_l, res_l, tinv_l)]
    lhs_l, o_add_l, s_add_l, glast_l = [], [], [], []
    for qn, kn, eg, gcol, qk, sol_b in zip(qn_l, kn_l, eg_l, gcol_l, qk_l, sol_l):
        glast = gcol[CHUNK - 1:CHUNK, :]
        ktail = kn * jnp.exp(glast - gcol)
        kt_sol = _dot(ktail.T.astype(BF16), sol_b)
        qk_sol = _dot(qk.astype(BF16), sol_b)
        lhs_l.append(jnp.concatenate([qn * eg - qk_sol[:, HEAD_DIM:], kt_sol[:, HEAD_DIM:]], axis=0).astype(BF16))
        o_add_l.append(qk_sol[:, :HEAD_DIM])
        s_add_l.append(kt_sol[:, :HEAD_DIM])
        glast_l.append(jnp.exp(glast))
    states = [s_ref[hh] for hh in range(hg)]
    for c in range(n_chunks):
        for hh in range(hg):
            i = hh * n_chunks + c
            rows = slice(c * CHUNK, (c + 1) * CHUNK)
            lanes = slice(hh * HEAD_DIM, (hh + 1) * HEAD_DIM)
            on_s = _dot(lhs_l[i], states[hh].astype(BF16))
            o = on_s[:CHUNK] + o_add_l[i]
            states[hh] = states[hh] * glast_l[i] - on_s[CHUNK:] + s_add_l[i]
            o_ref[rows, lanes] = (_rms(o, gn_ref[...]) * _silu(z_ref[rows, lanes])).astype(BF16)
    for hh in range(hg):
        s_ref[hh] = states[hh]

    @pl.when(t == pl.num_programs(2) - 1)
    def _():
        ssm_ref[0, 0] = s_ref[...]


def _gdn_prompt(proj, ab, conv_w, alog_pad, dtb_pad, gdn_norm, *, batch, seq, hg, tt):
    nt = seq // tt
    w = hg * HEAD_DIM
    body = functools.partial(_gdn_prompt_body, hg=hg, tt=tt)

    def col_spec(col0):
        return pl.BlockSpec((tt, w), lambda b, h, t: (b * nt + t, col0 // w + h))

    def cw_spec(part):
        return pl.BlockSpec((CONV_WIDTH, w), lambda b, h, t: (0, part * GDN_DIM // w + h))

    small = pl.BlockSpec((1, LANES), lambda b, h, t: (0, 0))
    return pl.pallas_call(
        body,
        grid=(batch, HEADS // hg, nt),
        in_specs=[col_spec(C_Q), col_spec(C_K), col_spec(C_V), col_spec(C_Z),
                  pl.BlockSpec((tt, LANES), lambda b, h, t: (b * nt + t, 0)),
                  cw_spec(0), cw_spec(1), cw_spec(2), small, small, small],
        out_specs=[
            pl.BlockSpec((tt, w), lambda b, h, t: (b * nt + t, h)),
            pl.BlockSpec((1, 1, hg, HEAD_DIM, HEAD_DIM), lambda b, h, t: (0, b, h, 0, 0)),
        ],
        out_shape=[
            jax.ShapeDtypeStruct((batch * seq, GDN_DIM), BF16),
            jax.ShapeDtypeStruct((1, batch, HEADS, HEAD_DIM, HEAD_DIM), F32),
        ],
        scratch_shapes=[
            pltpu.VMEM((SUBLANES + tt, 3 * w), F32),
            pltpu.VMEM((hg, HEAD_DIM, HEAD_DIM), F32),
            pltpu.VMEM((LANES, tt), F32),
        ],
        compiler_params=_cparams("parallel", "parallel", "arbitrary"),
        name="gdn_prompt",
    )(proj, proj, proj, proj, ab, conv_w, conv_w, conv_w, alog_pad, dtb_pad, gdn_norm)


def _gdn_sample_prep_body(q_ref, k_ref, v_ref, ab_ref, st_ref, cw_ref, alog_ref, dtb_ref,
                          qn_ref, kn_ref, vc_ref, eg_ref, beta_ref, qk_ref):
    def conv(part, u_ref):
        cols = slice(part * GDN_DIM, (part + 1) * GDN_DIM)
        acc = u_ref[...] * cw_ref[CONV_BUF:CONV_WIDTH, cols]
        for i in range(CONV_BUF):
            acc = acc + st_ref[i, :, cols] * cw_ref[i:i + 1, cols]
        return _silu(acc)

    qc = conv(0, q_ref)
    kc = conv(1, k_ref)
    vc_ref[...] = conv(2, v_ref)
    ab = ab_ref[...]
    g_all = -jnp.exp(alog_ref[...]) * _softplus(ab + dtb_ref[...])
    eg_all = jnp.exp(g_all)
    beta_all = jax.nn.sigmoid(ab)
    for h in range(HEADS):
        lanes = slice(h * HEAD_DIM, (h + 1) * HEAD_DIM)
        qh, kh = qc[:, lanes], kc[:, lanes]
        qn = qh * lax.rsqrt(jnp.sum(qh * qh, axis=-1, keepdims=True) + EPS) * (HEAD_DIM ** -0.5)
        kn = kh * lax.rsqrt(jnp.sum(kh * kh, axis=-1, keepdims=True) + EPS)
        qn_ref[:, lanes] = qn
        kn_ref[:, lanes] = kn
        shape = qn.shape
        qk_ref[:, lanes] = jnp.broadcast_to(jnp.sum(qn * kn, axis=-1, keepdims=True), shape)
        eg_ref[:, lanes] = jnp.broadcast_to(eg_all[:, h:h + 1], shape)
        beta_ref[:, lanes] = jnp.broadcast_to(beta_all[:, HEADS + h:HEADS + h + 1], shape)


def _gdn_sample_prep(proj, ab, conv_state_t, conv_w, alog_pad, dtb_pad):
    nb = proj.shape[0]

    def col_spec(col0):
        return pl.BlockSpec((nb, GDN_DIM), lambda i: (0, col0 // GDN_DIM))

    small = pl.BlockSpec((1, LANES), lambda i: (0, 0))
    full = pl.BlockSpec((nb, GDN_DIM), lambda i: (0, 0))
    return pl.pallas_call(
        _gdn_sample_prep_body,
        grid=(1,),
        in_specs=[col_spec(C_Q), col_spec(C_K), col_spec(C_V),
                  pl.BlockSpec((nb, LANES), lambda i: (0, 0)),
                  pl.BlockSpec((CONV_BUF, nb, QKV_DIM), lambda i: (0, 0, 0)),
                  pl.BlockSpec((CONV_WIDTH, QKV_DIM), lambda i: (0, 0)),
                  small, small],
        out_specs=[full] * 6,
        out_shape=[jax.ShapeDtypeStruct((nb, GDN_DIM), F32)] * 6,
        compiler_params=_cparams("arbitrary"),
        name="gdn_sample_prep",
    )(proj, proj, proj, ab, conv_state_t, conv_w, alog_pad, dtb_pad)


def _gdn_sample_step_body(qn_ref, kn_ref, vc_ref, eg_ref, beta_ref, qk_ref, z_ref, gn_ref, s_ref,
                          o_ref, snew_ref, *, bb):
    pad = jnp.zeros((LANES - HEADS, HEAD_DIM), F32)
    for b in range(bb):
        k16, q16 = kn_ref[b], qn_ref[b]
        cols = jnp.concatenate([k16, pad], axis=0).T
        kq = jnp.concatenate([k16, q16], axis=0).astype(BF16)
        outs = []
        for h in range(HEADS):
            kcol = cols[:, h:h + 1]
            s = s_ref[0, b, h]
            kq_s = _dot(kq, s.astype(BF16))
            k_s = kq_s[h:h + 1]
            q_s = kq_s[HEADS + h:HEADS + h + 1]
            eg = eg_ref[b, h:h + 1, :]
            v_new = beta_ref[b, h:h + 1, :] * (vc_ref[b, h:h + 1, :] - eg * k_s)
            snew_ref[0, b, h] = eg * s + kcol * v_new
            outs.append(eg * q_s + qk_ref[b, h:h + 1, :] * v_new)
        o = jnp.concatenate(outs, axis=0)
        o_ref[b] = (_rms(o, gn_ref[...]) * _silu(z_ref[b])).astype(BF16)


def _gdn_sample_step(qn, kn, vc, eg, beta, qk, z, gdn_norm, state_ssm, *, bb):
    nb = qn.shape[0]
    body = functools.partial(_gdn_sample_step_body, bb=bb)
    vec = pl.BlockSpec((bb, HEADS, HEAD_DIM), lambda i: (i, 0, 0))
    st = pl.BlockSpec((1, bb, HEADS, HEAD_DIM, HEAD_DIM), lambda i: (0, i, 0, 0, 0))
    return pl.pallas_call(
        body,
        grid=(nb // bb,),
        in_specs=[vec] * 7 + [pl.BlockSpec((1, LANES), lambda i: (0, 0)), st],
        out_specs=[vec, st],
        out_shape=[jax.ShapeDtypeStruct((nb, HEADS, HEAD_DIM), BF16),
                   jax.ShapeDtypeStruct(state_ssm.shape, F32)],
        compiler_params=_cparams("parallel"),
        name="gdn_sample_step",
    )(qn, kn, vc, eg, beta, qk, z, gdn_norm, state_ssm)


def _merge_body(pool_ref, gdn_ref, wpu_ref, wgu_ref, gp_ref, gg_ref, o_ref):
    o_ref[...] = (jax.nn.sigmoid(gp_ref[...]) * _dot(pool_ref[...], wpu_ref[...])
                  + jax.nn.sigmoid(gg_ref[...]) * _dot(gdn_ref[...], wgu_ref[...])).astype(BF16)


def _merge(proj, pool_out, gdn_out, w_pool_up, w_gdn_up, *, tm, tn=512):
    m = pool_out.shape[0]
    nj = D_MODEL // tn
    return pl.pallas_call(
        _merge_body,
        grid=(m // tm, nj),
        in_specs=[
            pl.BlockSpec((tm, POOL_DIM), lambda i, j: (i, 0)),
            pl.BlockSpec((tm, GDN_DIM), lambda i, j: (i, 0)),
            pl.BlockSpec((POOL_DIM, tn), lambda i, j: (0, j)),
            pl.BlockSpec((GDN_DIM, tn), lambda i, j: (0, j)),
            pl.BlockSpec((tm, tn), lambda i, j: (i, C_GP // tn + j)),
            pl.BlockSpec((tm, tn), lambda i, j: (i, C_GG // tn + j)),
        ],
        out_specs=pl.BlockSpec((tm, tn), lambda i, j: (i, j)),
        out_shape=jax.ShapeDtypeStruct((m, D_MODEL), BF16),
        compiler_params=_cparams("parallel", "parallel"),
        name="merge",
    )(pool_out, gdn_out, w_pool_up, w_gdn_up, proj, proj)


def _outproj_body(a_ref, w_ref, x_ref, o_ref):
    o_ref[...] = x_ref[...] + _dot(a_ref[...], w_ref[...])


def _outproj(merged, x, w_o, *, tm, tn=512):
    m = x.shape[0]
    return pl.pallas_call(
        _outproj_body,
        grid=(m // tm, D_MODEL // tn),
        in_specs=[
            pl.BlockSpec((tm, D_MODEL), lambda i, j: (i, 0)),
            pl.BlockSpec((D_MODEL, tn), lambda i, j: (0, j)),
            pl.BlockSpec((tm, tn), lambda i, j: (i, j)),
        ],
        out_specs=pl.BlockSpec((tm, tn), lambda i, j: (i, j)),
        out_shape=jax.ShapeDtypeStruct((m, D_MODEL), F32),
        compiler_params=_cparams("parallel", "parallel"),
        name="outproj",
    )(merged, w_o, x)


def _ffn_body(x_ref, gain_ref, wg_ref, wu_ref, wd_ref, o_ref, h_ref):
    f = pl.program_id(1)

    @pl.when(f == 0)
    def _():
        x = x_ref[...]
        h_ref[...] = _rms(x, gain_ref[...]).astype(BF16)
        o_ref[...] = x

    h = h_ref[...]
    act = (_silu(_dot(h, wg_ref[...].astype(BF16))) * _dot(h, wu_ref[...].astype(BF16))).astype(BF16)
    o_ref[...] += _dot(act, wd_ref[...].astype(BF16))


def _ffn(x, gain, w_gate_up, w_down, *, tm, tf):
    m = x.shape[0]
    nf = D_FF // tf
    return pl.pallas_call(
        _ffn_body,
        grid=(m // tm, nf),
        in_specs=[
            pl.BlockSpec((tm, D_MODEL), lambda i, f: (i, 0)),
            pl.BlockSpec((1, D_MODEL), lambda i, f: (0, 0)),
            pl.BlockSpec((D_MODEL, tf), lambda i, f: (0, f)),
            pl.BlockSpec((D_MODEL, tf), lambda i, f: (0, nf + f)),
            pl.BlockSpec((tf, D_MODEL), lambda i, f: (f, 0)),
        ],
        out_specs=pl.BlockSpec((tm, D_MODEL), lambda i, f: (i, 0)),
        out_shape=jax.ShapeDtypeStruct((m, D_MODEL), F32),
        scratch_shapes=[pltpu.VMEM((tm, D_MODEL), BF16)],
        compiler_params=_cparams("parallel", "arbitrary"),
        name="ffn",
    )(x, gain, w_gate_up, w_gate_up, w_down)


def _final_body(x_ref, p_ref, wple_ref, wpg_ref, gain_ref, o_ref):
    x = x_ref[...]
    gate = jax.nn.sigmoid(_dot(x.astype(BF16), wpg_ref[...]))
    emb = _dot(p_ref[...].astype(BF16), wple_ref[...])
    o_ref[...] = _rms(x + emb * gate, gain_ref[...])


def _final(x, p, w_ple, w_ple_gate, gain, *, tm):
    m = x.shape[0]
    return pl.pallas_call(
        _final_body,
        grid=(m // tm,),
        in_specs=[
            pl.BlockSpec((tm, D_MODEL), lambda i: (i, 0)),
            pl.BlockSpec((tm, PLE_DIM), lambda i: (i, 0)),
            pl.BlockSpec((PLE_DIM, D_MODEL), lambda i: (0, 0), pipeline_mode=pl.Buffered(1)),
            pl.BlockSpec((D_MODEL, D_MODEL), lambda i: (0, 0), pipeline_mode=pl.Buffered(1)),
            pl.BlockSpec((1, D_MODEL), lambda i: (0, 0)),
        ],
        out_specs=pl.BlockSpec((tm, D_MODEL), lambda i: (i, 0)),
        out_shape=jax.ShapeDtypeStruct((m, D_MODEL), F32),
        compiler_params=_cparams("parallel"),
        name="final",
    )(x, p, w_ple, w_ple_gate, gain)


def _tile_rows(m, want):
    return want if m % want == 0 else m


def _dense_tail(x, p, proj, pool_out, gdn_out, wts):
    m = x.shape[0]
    tm = _tile_rows(m, 1024)
    merged = _merge(proj, pool_out, gdn_out, wts["w_pool_up"], wts["w_gdn_up"], tm=tm)
    x = _outproj(merged, x, wts["w_o"], tm=tm)
    x = _ffn(x, wts["norm_ffn"], wts["w_gate_up"], wts["w_down"], tm=tm, tf=256 if tm >= 1024 else 512)
    return _final(x, p, wts["w_ple"], wts["w_ple_gate"], wts["norm_final"], tm=tm)


def kernel(x_prompt, x_sample, p_prompt, p_sample, state_pool, state_conv, state_ssm, norm_mix, w_in, pool_w,
           pool_scale, conv_w, a_log, dt_bias, gdn_norm, w_pool_up, w_gdn_up, w_o, norm_ffn, w_gate_up, w_down,
           w_ple, w_ple_gate, norm_final):
    batch, seq, _ = x_prompt.shape
    nb = x_sample.shape[0]
    w_in_t = jnp.swapaxes(w_in[0], 0, 1)
    wts = dict(
        w_pool_up=w_pool_up[0].astype(BF16), w_gdn_up=w_gdn_up[0].astype(BF16), w_o=w_o[0].astype(BF16),
        norm_ffn=norm_ffn, w_gate_up=w_gate_up[0], w_down=w_down[0],
        w_ple=w_ple[0].astype(BF16), w_ple_gate=w_ple_gate[0].astype(BF16), norm_final=norm_final[None, :],
    )
    pool_w_b = pool_w[0].astype(BF16)
    lane_pad = ((0, 0), (0, LANES - HEADS))
    alog_pad = jnp.pad(a_log, lane_pad)
    dtb_pad = jnp.pad(dt_bias, lane_pad)
    gn = gdn_norm

    mp = batch * seq
    xp = x_prompt.reshape(mp, D_MODEL)
    xs = x_sample.reshape(nb, D_MODEL)
    h_p, ab_p = _norm_in(xp, norm_mix, w_in_t, tm=_tile_rows(mp, 512))
    h_s, ab_s = _norm_in(xs, norm_mix, w_in_t, tm=nb)
    tm_in = _tile_rows(mp, 1024)
    proj_p, proj_s = _inproj(h_p, h_s, w_in_t, tm=tm_in, tn=1024)

    pool_out_p = _pool_prompt(proj_p, pool_w_b, pool_scale, batch=batch, seq=seq, tt=_tile_rows(seq, 512))
    gdn_out_p, ssm_p = _gdn_prompt(proj_p, ab_p, conv_w[0], alog_pad, dtb_pad, gn, batch=batch, seq=seq,
                                   hg=8, tt=_tile_rows(seq, 256))
    y_p = _dense_tail(xp, p_prompt[0].reshape(mp, PLE_DIM), proj_p, pool_out_p, gdn_out_p, wts)
    proj_p3 = proj_p.reshape(batch, seq, C_ALL)
    pool_p = proj_p3[:, seq - POOL_BUF:, C_POOL:C_END][None]
    conv_p = proj_p3[:, seq - CONV_BUF:, C_Q:C_Z][None]

    pool_out_s = _pool_sample(proj_s, jnp.swapaxes(state_pool[0], 0, 1), pool_w_b, pool_scale)
    prep = _gdn_sample_prep(proj_s, ab_s, jnp.swapaxes(state_conv[0], 0, 1), conv_w[0], alog_pad, dtb_pad)
    to_heads = lambda a: a.reshape(nb, HEADS, HEAD_DIM)
    gdn_out_s, ssm_s = _gdn_sample_step(*[to_heads(a) for a in prep], to_heads(proj_s[:, C_Z:C_POOL]), gn,
                                        state_ssm, bb=4)
    y_s = _dense_tail(xs, p_sample[0].reshape(nb, PLE_DIM), proj_s, pool_out_s, gdn_out_s.reshape(nb, GDN_DIM), wts)
    pool_s = jnp.concatenate([state_pool[0][:, 1:], proj_s[:, None, C_POOL:C_END]], axis=1)[None]
    conv_s = jnp.concatenate([state_conv[0][:, 1:], proj_s[:, None, C_Q:C_Z]], axis=1)[None]

    return (y_p.reshape(batch, seq, D_MODEL), y_s.reshape(nb, 1, D_MODEL), pool_p, conv_p, ssm_p,
            pool_s, conv_s, ssm_s)
```

```python
import functools

import jax
import jax.numpy as jnp
from jax import lax
from jax.experimental import pallas as pl
from jax.experimental.pallas import tpu as pltpu

F32 = jnp.float32
BF16 = jnp.bfloat16

D_MODEL = 2048
PAST_LEN = 16384
POOL_WINDOWS = (2, 4, 8, 16)
POOL_GROUP_DIM = D_MODEL // 8
POOL_DIM = len(POOL_WINDOWS) * POOL_GROUP_DIM
POOL_BUF = max(POOL_WINDOWS) - 1
HEAD_DIM = 128
HEADS = D_MODEL // HEAD_DIM
GDN_DIM = HEADS * HEAD_DIM
QKV_DIM = 3 * GDN_DIM
CONV_WIDTH = 4
CONV_BUF = CONV_WIDTH - 1
D_FF = -(-8 * D_MODEL // (3 * 256)) * 256
PLE_DIM = 256
EPS = 1e-6

IN_QKV = POOL_DIM
IN_Z = IN_QKV + QKV_DIM
IN_AB = IN_Z + GDN_DIM
IN_GATES = IN_AB + 2 * HEADS
C_Q = 0
C_K = C_Q + GDN_DIM
C_V = C_K + GDN_DIM
C_Z = C_V + GDN_DIM
C_POOL = C_Z + GDN_DIM
C_END = C_POOL + POOL_DIM
C_GP = C_END
C_GG = C_GP + D_MODEL
C_ALL = C_GG + D_MODEL

LANES = 128
SUBLANES = 8
VMEM_LIMIT_BYTES = 56 * 1024 * 1024
CHUNK = 128

TM_DENSE = 1024
TN_INPROJ = 1024
TN_DENSE = 1024
TF_FFN = 256
TF_FFN_SMALL = 512
TM_NORM = 512
TT_POOL = 512
GDN_HEADS_PER_STEP = 8
TT_GDN = 256
BB_SAMPLE = 8


def _cparams(*sem):
    return pltpu.CompilerParams(dimension_semantics=sem, vmem_limit_bytes=VMEM_LIMIT_BYTES)


def _rms(x, gain):
    return x * lax.rsqrt(jnp.mean(x * x, axis=-1, keepdims=True) + EPS) * gain


def _dot(a, b):
    return jnp.dot(a, b, preferred_element_type=F32)


def _dot_nt(a, b):
    return lax.dot_general(a, b, (((1,), (1,)), ((), ())), preferred_element_type=F32)


def _silu(x):
    return x * jax.nn.sigmoid(x)


def _norm_in_body(x_ref, gain_ref, wab_ref, h_ref, ab_ref):
    h = _rms(x_ref[...], gain_ref[...]).astype(BF16)
    h_ref[...] = h
    ab_ref[...] = _dot_nt(h, wab_ref[...].astype(BF16))


def _norm_in(x, gain, w_in_t, *, tm):
    m = x.shape[0]
    return pl.pallas_call(
        _norm_in_body,
        grid=(m // tm,),
        in_specs=[
            pl.BlockSpec((tm, D_MODEL), lambda i: (i, 0)),
            pl.BlockSpec((1, D_MODEL), lambda i: (0, 0)),
            pl.BlockSpec((LANES, D_MODEL), lambda i: (IN_AB // LANES, 0)),
        ],
        out_specs=[
            pl.BlockSpec((tm, D_MODEL), lambda i: (i, 0)),
            pl.BlockSpec((tm, LANES), lambda i: (i, 0)),
        ],
        out_shape=[jax.ShapeDtypeStruct((m, D_MODEL), BF16), jax.ShapeDtypeStruct((m, LANES), F32)],
        compiler_params=_cparams("parallel"),
        name="norm_in",
    )(x, gain, w_in_t)


def _inproj_body(hp_ref, hs_ref, w_ref, op_ref, os_ref, wb_ref):
    i = pl.program_id(1)

    @pl.when(i == 0)
    def _():
        wb_ref[...] = w_ref[...].astype(BF16)
        os_ref[...] = _dot_nt(hs_ref[...], wb_ref[...])

    @pl.when(i > 0)
    def _():
        op_ref[...] = _dot_nt(hp_ref[...], wb_ref[...])


def _inproj(h_prompt, h_sample, w_in_t, *, tm, tn):
    mp, ms = h_prompt.shape[0], h_sample.shape[0]
    n_p = mp // tm
    n_out, out_dtype = C_ALL, F32
    n_mid = (QKV_DIM + GDN_DIM) // tn
    n_pool = POOL_DIM // tn

    def row_start(j):
        return jnp.where(j < n_mid, IN_QKV + j * tn,
                         jnp.where(j < n_mid + n_pool, (j - n_mid) * tn, IN_GATES + (j - n_mid - n_pool) * tn))

    return pl.pallas_call(
        _inproj_body,
        grid=(n_out // tn, n_p + 1),
        in_specs=[
            pl.BlockSpec((tm, D_MODEL), lambda j, i: (jnp.maximum(i - 1, 0), 0)),
            pl.BlockSpec((ms, D_MODEL), lambda j, i: (0, 0)),
            pl.BlockSpec((pl.Element(tn), pl.Element(D_MODEL)),
                         lambda j, i: (pl.multiple_of(row_start(j), SUBLANES), 0)),
        ],
        out_specs=[pl.BlockSpec((tm, tn), lambda j, i: (jnp.maximum(i - 1, 0), j)),
                   pl.BlockSpec((ms, tn), lambda j, i: (0, j))],
        out_shape=[jax.ShapeDtypeStruct((mp, n_out), out_dtype), jax.ShapeDtypeStruct((ms, n_out), out_dtype)],
        scratch_shapes=[pltpu.VMEM((tn, D_MODEL), BF16)],
        compiler_params=_cparams("arbitrary", "arbitrary"),
        name="inproj",
    )(h_prompt, h_sample, w_in_t)


def _pool_group_out(mean, tok, pw_ref, scale_ref, gi):
    cols = slice(gi * POOL_GROUP_DIM, (gi + 1) * POOL_GROUP_DIM)
    y = _dot((mean - tok).astype(BF16), pw_ref[gi])
    return (y * scale_ref[:, cols]).astype(BF16)


def _pool_prompt_body(u_ref, pw_ref, scale_ref, o_ref, ext_ref, *, tt, start_pos):
    t = pl.program_id(1)
    halo = POOL_BUF + 1

    @pl.when(t == 0)
    def _():
        ext_ref[0:halo, :] = jnp.zeros((halo, POOL_DIM), F32)

    ext_ref[halo:halo + tt, :] = u_ref[...]
    pos = start_pos + t * tt + lax.broadcasted_iota(jnp.int32, (tt, 1), 0)
    for gi, w in enumerate(POOL_WINDOWS):
        cols = slice(gi * POOL_GROUP_DIM, (gi + 1) * POOL_GROUP_DIM)
        tok = ext_ref[halo:halo + tt, cols]
        tot = tok
        for i in range(1, w):
            tot = tot + ext_ref[pl.ds(halo - i, tt), cols]
        cnt = jnp.minimum(pos + 1, w).astype(F32)
        o_ref[:, cols] = _pool_group_out(tot / cnt, tok, pw_ref, scale_ref, gi)
    ext_ref[0:halo, :] = ext_ref[tt:tt + halo, :]


def _pool_prompt(proj, pool_w, pool_scale, *, batch, seq, tt):
    nt = seq // tt
    body = functools.partial(_pool_prompt_body, tt=tt, start_pos=0)
    return pl.pallas_call(
        body,
        grid=(batch, nt),
        in_specs=[
            pl.BlockSpec((tt, POOL_DIM), lambda b, t: (b * nt + t, C_POOL // POOL_DIM)),
            pl.BlockSpec((len(POOL_WINDOWS), POOL_GROUP_DIM, POOL_GROUP_DIM), lambda b, t: (0, 0, 0)),
            pl.BlockSpec((1, POOL_DIM), lambda b, t: (0, 0)),
        ],
        out_specs=pl.BlockSpec((tt, POOL_DIM), lambda b, t: (b * nt + t, 0)),
        out_shape=jax.ShapeDtypeStruct((batch * seq, POOL_DIM), BF16),
        scratch_shapes=[pltpu.VMEM((POOL_BUF + 1 + tt, POOL_DIM), F32)],
        compiler_params=_cparams("parallel", "arbitrary"),
        name="pool_prompt",
    )(proj, pool_w, pool_scale)


def _pool_sample_body(u_ref, st_ref, pw_ref, scale_ref, o_ref, *, start_pos):
    for gi, w in enumerate(POOL_WINDOWS):
        cols = slice(gi * POOL_GROUP_DIM, (gi + 1) * POOL_GROUP_DIM)
        tok = u_ref[:, cols]
        tot = tok
        for i in range(1, w):
            tot = tot + st_ref[POOL_BUF - i, :, cols]
        cnt = float(min(start_pos + 1, w))
        o_ref[:, cols] = _pool_group_out(tot / cnt, tok, pw_ref, scale_ref, gi)


def _pool_sample(proj, state_t, pool_w, pool_scale):
    nb = proj.shape[0]
    body = functools.partial(_pool_sample_body, start_pos=PAST_LEN)
    return pl.pallas_call(
        body,
        grid=(1,),
        in_specs=[
            pl.BlockSpec((nb, POOL_DIM), lambda i: (0, C_POOL // POOL_DIM)),
            pl.BlockSpec((POOL_BUF, nb, POOL_DIM), lambda i: (0, 0, 0)),
            pl.BlockSpec((len(POOL_WINDOWS), POOL_GROUP_DIM, POOL_GROUP_DIM), lambda i: (0, 0, 0)),
            pl.BlockSpec((1, POOL_DIM), lambda i: (0, 0)),
        ],
        out_specs=pl.BlockSpec((nb, POOL_DIM), lambda i: (0, 0)),
        out_shape=jax.ShapeDtypeStruct((nb, POOL_DIM), BF16),
        compiler_params=_cparams("arbitrary"),
        name="pool_sample",
    )(proj, state_t, pool_w, pool_scale)


def _split3(x):
    hi = x.astype(BF16)
    r1 = x - hi.astype(F32)
    mid = r1.astype(BF16)
    lo = (r1 - mid.astype(F32)).astype(BF16)
    return hi, mid, lo


def _softplus(x):
    return jnp.maximum(x, 0.0) + jnp.log1p(jnp.exp(-jnp.abs(x)))


def _unit_lower_inverse_minus_identity(mats):
    eye = jnp.where(lax.broadcasted_iota(jnp.int32, (CHUNK, CHUNK), 0)
                    == lax.broadcasted_iota(jnp.int32, (CHUNK, CHUNK), 1), 1.0, 0.0)
    levels = CHUNK.bit_length() - 1
    qbs = [(-a).astype(BF16) for a in mats]
    rs = [eye - a for a in mats]
    qs = [_dot(qb, qb) for qb in qbs]
    for j in range(1, levels):
        qbs = [q.astype(BF16) for q in qs]
        rbs = [r.astype(BF16) for r in rs]
        if j < levels - 1:
            ps = [_dot(qb, jnp.concatenate([qb, rb], axis=1)) for qb, rb in zip(qbs, rbs)]
            rs = [r + p[:, CHUNK:] for r, p in zip(rs, ps)]
            qs = [p[:, :CHUNK] for p in ps]
        else:
            rs = [r + _dot(qb, rb) for r, qb, rb in zip(rs, qbs, rbs)]
    return [r - eye for r in rs]


def _gdn_prompt_body(q_ref, k_ref, v_ref, z_ref, ab_ref, cwq_ref, cwk_ref, cwv_ref, alog_ref, dtb_ref, gn_ref,
                     o_ref, ssm_ref, ext_ref, s_ref, gct_ref, *, hg, tt):
    hgi = pl.program_id(1)
    t = pl.program_id(2)
    w = hg * HEAD_DIM
    halo = SUBLANES

    @pl.when(t == 0)
    def _():
        ext_ref[0:halo, :] = jnp.zeros((halo, 3 * w), F32)
        s_ref[...] = jnp.zeros_like(s_ref)

    ext_ref[halo:halo + tt, 0:w] = q_ref[...]
    ext_ref[halo:halo + tt, w:2 * w] = k_ref[...]
    ext_ref[halo:halo + tt, 2 * w:3 * w] = v_ref[...]

    def conv(part, cw_ref):
        cols = slice(part * w, (part + 1) * w)
        acc = ext_ref[halo:halo + tt, cols] * cw_ref[CONV_BUF:CONV_WIDTH, :]
        for i in range(CONV_BUF):
            acc = acc + ext_ref[pl.ds(halo - CONV_BUF + i, tt), cols] * cw_ref[i:i + 1, :]
        return _silu(acc)

    qc = conv(0, cwq_ref)
    kc = conv(1, cwk_ref)
    vc = conv(2, cwv_ref)
    ext_ref[0:halo, :] = ext_ref[tt:tt + halo, :]

    ab = ab_ref[...]
    g_all = -jnp.exp(alog_ref[...]) * _softplus(ab + dtb_ref[...])
    beta_all = jax.nn.sigmoid(ab)
    ri = lax.broadcasted_iota(jnp.int32, (tt, tt), 0)
    ci = lax.broadcasted_iota(jnp.int32, (tt, tt), 1)
    shift = CHUNK.bit_length() - 1
    tri = jnp.where(((ri >> shift) == (ci >> shift)) & (ci <= ri), 1.0, 0.0).astype(BF16)
    g_hi, g_mid, g_lo = _split3(g_all)
    gc_all = _dot(tri, g_hi) + _dot(tri, g_mid) + _dot(tri, g_lo)
    gct_ref[...] = gc_all.T

    lane = lax.broadcasted_iota(jnp.int32, (tt, LANES), 1)
    r128 = lax.broadcasted_iota(jnp.int32, (CHUNK, CHUNK), 0)
    c128 = lax.broadcasted_iota(jnp.int32, (CHUNK, CHUNK), 1)
    sub8 = lax.broadcasted_iota(jnp.int32, (SUBLANES, CHUNK), 0)
    n_chunks = tt // CHUNK
    chains = [(hh, c) for hh in range(hg) for c in range(n_chunks)]
    gcol_heads, bcol_heads = [], []
    for hh in range(hg):
        head = hgi * hg + hh
        gcol_heads.append(jnp.sum(jnp.where(lane == head, gc_all, 0.0), axis=-1, keepdims=True))
        bcol_heads.append(jnp.sum(jnp.where(lane == head + HEADS, beta_all, 0.0), axis=-1, keepdims=True))

    qn_l, kn_l, kb_l, rhs_l, eg_l, gcol_l, dec_l = [], [], [], [], [], [], []
    for hh, c in chains:
        head = hgi * hg + hh
        rows = slice(c * CHUNK, (c + 1) * CHUNK)
        lanes = slice(hh * HEAD_DIM, (hh + 1) * HEAD_DIM)
        qh, kh, vh = qc[rows, lanes], kc[rows, lanes], vc[rows, lanes]
        qn = qh * lax.rsqrt(jnp.sum(qh * qh, axis=-1, keepdims=True) + EPS) * (HEAD_DIM ** -0.5)
        kn = kh * lax.rsqrt(jnp.sum(kh * kh, axis=-1, keepdims=True) + EPS)
        gcol = gcol_heads[hh][rows]
        bcol = bcol_heads[hh][rows]
        gblk = gct_ref[pl.ds(pl.multiple_of((head // SUBLANES) * SUBLANES, SUBLANES), SUBLANES), rows]
        grow = jnp.sum(jnp.where(sub8 == (head & (SUBLANES - 1)), gblk, 0.0), axis=0, keepdims=True)
        eg = jnp.exp(gcol)
        kb = kn * bcol
        qn_l.append(qn)
        kn_l.append(kn)
        kb_l.append(kb)
        eg_l.append(eg)
        gcol_l.append(gcol)
        dec_l.append(jnp.exp(jnp.minimum(gcol - grow, 0.0)))
        rhs_l.append(jnp.concatenate([vh * bcol, kb * eg], axis=1))

    a_l, qk_l = [], []
    for qn, kn, kb, decay in zip(qn_l, kn_l, kb_l, dec_l):
        knb = kn.astype(BF16)
        a_l.append(_dot_nt(kb.astype(BF16), knb) * jnp.where(c128 < r128, decay, 0.0))
        qk_l.append(_dot_nt(qn.astype(BF16), knb) * jnp.where(c128 <= r128, decay, 0.0))
    tinv_l = [x.astype(BF16) for x in _unit_lower_inverse_minus_identity(a_l)]
    sol_l = [rhs + _dot(tinv, rhs.astype(BF16)) for tinv, rhs in zip(tinv_l, rhs_l)]
    res_l = []
    for a, sol, rhs in zip(a_l, sol_l, rhs_l):
        a_sol = _dot(a.astype(BF16), sol.astype(BF16))
        res_l.append((rhs - sol) - a_sol)
    sol_l = [(sol + res + _dot(tinv, res.astype(BF16))).astype(BF16) for sol, res, tinv in zip(sol_l, res_l, tinv_l)]
    lhs_l, o_add_l, s_add_l, glast_l = [], [], [], []
    for qn, kn, eg, gcol, qk, sol_b in zip(qn_l, kn_l, eg_l, gcol_l, qk_l, sol_l):
        glast = gcol[CHUNK - 1:CHUNK, :]
        ktail = kn * jnp.exp(glast - gcol)
        kt_sol = _dot(ktail.T.astype(BF16), sol_b)
        qk_sol = _dot(qk.astype(BF16), sol_b)
        lhs_l.append(jnp.concatenate([qn * eg - qk_sol[:, HEAD_DIM:], kt_sol[:, HEAD_DIM:]], axis=0).astype(BF16))
        o_add_l.append(qk_sol[:, :HEAD_DIM])
        s_add_l.append(kt_sol[:, :HEAD_DIM])
        glast_l.append(jnp.exp(glast))
    states = [s_ref[hh] for hh in range(hg)]
    for c in range(n_chunks):
        for hh in range(hg):
            i = hh * n_chunks + c
            rows = slice(c * CHUNK, (c + 1) * CHUNK)
            lanes = slice(hh * HEAD_DIM, (hh + 1) * HEAD_DIM)
            on_s = _dot(lhs_l[i], states[hh].astype(BF16))
            o = on_s[:CHUNK] + o_add_l[i]
            states[hh] = states[hh] * glast_l[i] - on_s[CHUNK:] + s_add_l[i]
            o_ref[rows, lanes] = (_rms(o, gn_ref[...]) * _silu(z_ref[rows, lanes])).astype(BF16)
    for hh in range(hg):
        s_ref[hh] = states[hh]

    @pl.when(t == pl.num_programs(2) - 1)
    def _():
        ssm_ref[0, 0] = s_ref[...]


def _gdn_prompt(proj, ab, conv_w, alog_pad, dtb_pad, gdn_norm, *, batch, seq, hg, tt):
    nt = seq // tt
    w = hg * HEAD_DIM
    body = functools.partial(_gdn_prompt_body, hg=hg, tt=tt)

    def col_spec(col0):
        return pl.BlockSpec((tt, w), lambda b, h, t: (b * nt + t, col0 // w + h))

    def cw_spec(part):
        return pl.BlockSpec((CONV_WIDTH, w), lambda b, h, t: (0, part * GDN_DIM // w + h))

    small = pl.BlockSpec((1, LANES), lambda b, h, t: (0, 0))
    return pl.pallas_call(
        body,
        grid=(batch, HEADS // hg, nt),
        in_specs=[col_spec(C_Q), col_spec(C_K), col_spec(C_V), col_spec(C_Z),
                  pl.BlockSpec((tt, LANES), lambda b, h, t: (b * nt + t, 0)),
                  cw_spec(0), cw_spec(1), cw_spec(2), small, small, small],
        out_specs=[
            pl.BlockSpec((tt, w), lambda b, h, t: (b * nt + t, h)),
            pl.BlockSpec((1, 1, hg, HEAD_DIM, HEAD_DIM), lambda b, h, t: (0, b, h, 0, 0)),
        ],
        out_shape=[
            jax.ShapeDtypeStruct((batch * seq, GDN_DIM), BF16),
            jax.ShapeDtypeStruct((1, batch, HEADS, HEAD_DIM, HEAD_DIM), F32),
        ],
        scratch_shapes=[
            pltpu.VMEM((SUBLANES + tt, 3 * w), F32),
            pltpu.VMEM((hg, HEAD_DIM, HEAD_DIM), F32),
            pltpu.VMEM((LANES, tt), F32),
        ],
        compiler_params=_cparams("parallel", "parallel", "arbitrary"),
        name="gdn_prompt",
    )(proj, proj, proj, proj, ab, conv_w, conv_w, conv_w, alog_pad, dtb_pad, gdn_norm)


def _gdn_sample_prep_body(q_ref, k_ref, v_ref, ab_ref, st_ref, cw_ref, alog_ref, dtb_ref,
                          qn_ref, kn_ref, vc_ref, eg_ref, beta_ref, qk_ref):
    def conv(part, u_ref):
        cols = slice(part * GDN_DIM, (part + 1) * GDN_DIM)
        acc = u_ref[...] * cw_ref[CONV_BUF:CONV_WIDTH, cols]
        for i in range(CONV_BUF):
            acc = acc + st_ref[i, :, cols] * cw_ref[i:i + 1, cols]
        return _silu(acc)

    qc = conv(0, q_ref)
    kc = conv(1, k_ref)
    vc_ref[...] = conv(2, v_ref)
    ab = ab_ref[...]
    g_all = -jnp.exp(alog_ref[...]) * _softplus(ab + dtb_ref[...])
    eg_all = jnp.exp(g_all)
    beta_all = jax.nn.sigmoid(ab)
    for h in range(HEADS):
        lanes = slice(h * HEAD_DIM, (h + 1) * HEAD_DIM)
        qh, kh = qc[:, lanes], kc[:, lanes]
        qn = qh * lax.rsqrt(jnp.sum(qh * qh, axis=-1, keepdims=True) + EPS) * (HEAD_DIM ** -0.5)
        kn = kh * lax.rsqrt(jnp.sum(kh * kh, axis=-1, keepdims=True) + EPS)
        qn_ref[:, lanes] = qn
        kn_ref[:, lanes] = kn
        shape = qn.shape
        qk_ref[:, lanes] = jnp.broadcast_to(jnp.sum(qn * kn, axis=-1, keepdims=True), shape)
        eg_ref[:, lanes] = jnp.broadcast_to(eg_all[:, h:h + 1], shape)
        beta_ref[:, lanes] = jnp.broadcast_to(beta_all[:, HEADS + h:HEADS + h + 1], shape)


def _gdn_sample_prep(proj, ab, conv_state_t, conv_w, alog_pad, dtb_pad):
    nb = proj.shape[0]

    def col_spec(col0):
        return pl.BlockSpec((nb, GDN_DIM), lambda i: (0, col0 // GDN_DIM))

    small = pl.BlockSpec((1, LANES), lambda i: (0, 0))
    full = pl.BlockSpec((nb, GDN_DIM), lambda i: (0, 0))
    return pl.pallas_call(
        _gdn_sample_prep_body,
        grid=(1,),
        in_specs=[col_spec(C_Q), col_spec(C_K), col_spec(C_V),
                  pl.BlockSpec((nb, LANES), lambda i: (0, 0)),
                  pl.BlockSpec((CONV_BUF, nb, QKV_DIM), lambda i: (0, 0, 0)),
                  pl.BlockSpec((CONV_WIDTH, QKV_DIM), lambda i: (0, 0)),
                  small, small],
        out_specs=[full] * 6,
        out_shape=[jax.ShapeDtypeStruct((nb, GDN_DIM), F32)] * 6,
        compiler_params=_cparams("arbitrary"),
        name="gdn_sample_prep",
    )(proj, proj, proj, ab, conv_state_t, conv_w, alog_pad, dtb_pad)


def _gdn_sample_step_body(qn_ref, kn_ref, vc_ref, eg_ref, beta_ref, qk_ref, z_ref, gn_ref, s_ref,
                          o_ref, snew_ref, *, bb):
    pad = jnp.zeros((LANES - HEADS, HEAD_DIM), F32)
    for b in range(bb):
        k16, q16 = kn_ref[b], qn_ref[b]
        cols = jnp.concatenate([k16, pad], axis=0).T
        kq = jnp.concatenate([k16, q16], axis=0).astype(BF16)
        outs = []
        for h in range(HEADS):
            kcol = cols[:, h:h + 1]
            s = s_ref[0, b, h]
            kq_s = _dot(kq, s.astype(BF16))
            k_s = kq_s[h:h + 1]
            q_s = kq_s[HEADS + h:HEADS + h + 1]
            eg = eg_ref[b, h:h + 1, :]
            v_new = beta_ref[b, h:h + 1, :] * (vc_ref[b, h:h + 1, :] - eg * k_s)
            snew_ref[0, b, h] = eg * s + kcol * v_new
            outs.append(eg * q_s + qk_ref[b, h:h + 1, :] * v_new)
        o = jnp.concatenate(outs, axis=0)
        o_ref[b] = (_rms(o, gn_ref[...]) * _silu(z_ref[b])).astype(BF16)


def _gdn_sample_step(qn, kn, vc, eg, beta, qk, z, gdn_norm, state_ssm, *, bb):
    nb = qn.shape[0]
    body = functools.partial(_gdn_sample_step_body, bb=bb)
    vec = pl.BlockSpec((bb, HEADS, HEAD_DIM), lambda i: (i, 0, 0))
    st = pl.BlockSpec((1, bb, HEADS, HEAD_DIM, HEAD_DIM), lambda i: (0, i, 0, 0, 0))
    return pl.pallas_call(
        body,
        grid=(nb // bb,),
        in_specs=[vec] * 7 + [pl.BlockSpec((1, LANES), lambda i: (0, 0)), st],
        out_specs=[vec, st],
        out_shape=[jax.ShapeDtypeStruct((nb, HEADS, HEAD_DIM), BF16),
                   jax.ShapeDtypeStruct(state_ssm.shape, F32)],
        compiler_params=_cparams("parallel"),
        name="gdn_sample_step",
    )(qn, kn, vc, eg, beta, qk, z, gdn_norm, state_ssm)


def _merge_body(pool_ref, gdn_ref, wpu_ref, wgu_ref, gp_ref, gg_ref, o_ref):
    o_ref[...] = (jax.nn.sigmoid(gp_ref[...]) * _dot(pool_ref[...], wpu_ref[...])
                  + jax.nn.sigmoid(gg_ref[...]) * _dot(gdn_ref[...], wgu_ref[...])).astype(BF16)


def _merge(proj, pool_out, gdn_out, w_pool_up, w_gdn_up, *, tm, tn):
    m = pool_out.shape[0]
    nj = D_MODEL // tn
    return pl.pallas_call(
        _merge_body,
        grid=(m // tm, nj),
        in_specs=[
            pl.BlockSpec((tm, POOL_DIM), lambda i, j: (i, 0)),
            pl.BlockSpec((tm, GDN_DIM), lambda i, j: (i, 0)),
            pl.BlockSpec((POOL_DIM, tn), lambda i, j: (0, j)),
            pl.BlockSpec((GDN_DIM, tn), lambda i, j: (0, j)),
            pl.BlockSpec((tm, tn), lambda i, j: (i, C_GP // tn + j)),
            pl.BlockSpec((tm, tn), lambda i, j: (i, C_GG // tn + j)),
        ],
        out_specs=pl.BlockSpec((tm, tn), lambda i, j: (i, j)),
        out_shape=jax.ShapeDtypeStruct((m, D_MODEL), BF16),
        compiler_params=_cparams("parallel", "parallel"),
        name="merge",
    )(pool_out, gdn_out, w_pool_up, w_gdn_up, proj, proj)


def _outproj_body(a_ref, w_ref, x_ref, o_ref):
    o_ref[...] = x_ref[...] + _dot(a_ref[...], w_ref[...])


def _outproj(merged, x, w_o, *, tm, tn):
    m = x.shape[0]
    return pl.pallas_call(
        _outproj_body,
        grid=(m // tm, D_MODEL // tn),
        in_specs=[
            pl.BlockSpec((tm, D_MODEL), lambda i, j: (i, 0)),
            pl.BlockSpec((D_MODEL, tn), lambda i, j: (0, j)),
            pl.BlockSpec((tm, tn), lambda i, j: (i, j)),
        ],
        out_specs=pl.BlockSpec((tm, tn), lambda i, j: (i, j)),
        out_shape=jax.ShapeDtypeStruct((m, D_MODEL), F32),
        compiler_params=_cparams("parallel", "parallel"),
        name="outproj",
    )(merged, w_o, x)


def _ffn_body(x_ref, gain_ref, wg_ref, wu_ref, wd_ref, o_ref, h_ref):
    f = pl.program_id(1)

    @pl.when(f == 0)
    def _():
        x = x_ref[...]
        h_ref[...] = _rms(x, gain_ref[...]).astype(BF16)
        o_ref[...] = x

    h = h_ref[...]
    act = (_silu(_dot(h, wg_ref[...].astype(BF16))) * _dot(h, wu_ref[...].astype(BF16))).astype(BF16)
    o_ref[...] += _dot(act, wd_ref[...].astype(BF16))


def _ffn(x, gain, w_gate_up, w_down, *, tm, tf):
    m = x.shape[0]
    nf = D_FF // tf
    return pl.pallas_call(
        _ffn_body,
        grid=(m // tm, nf),
        in_specs=[
            pl.BlockSpec((tm, D_MODEL), lambda i, f: (i, 0)),
            pl.BlockSpec((1, D_MODEL), lambda i, f: (0, 0)),
            pl.BlockSpec((D_MODEL, tf), lambda i, f: (0, f)),
            pl.BlockSpec((D_MODEL, tf), lambda i, f: (0, nf + f)),
            pl.BlockSpec((tf, D_MODEL), lambda i, f: (f, 0)),
        ],
        out_specs=pl.BlockSpec((tm, D_MODEL), lambda i, f: (i, 0)),
        out_shape=jax.ShapeDtypeStruct((m, D_MODEL), F32),
        scratch_shapes=[pltpu.VMEM((tm, D_MODEL), BF16)],
        compiler_params=_cparams("parallel", "arbitrary"),
        name="ffn",
    )(x, gain, w_gate_up, w_gate_up, w_down)


def _final_body(x_ref, p_ref, wple_ref, wpg_ref, gain_ref, o_ref):
    x = x_ref[...]
    gate = jax.nn.sigmoid(_dot(x.astype(BF16), wpg_ref[...]))
    emb = _dot(p_ref[...].astype(BF16), wple_ref[...])
    o_ref[...] = _rms(x + emb * gate, gain_ref[...])


def _final(x, p, w_ple, w_ple_gate, gain, *, tm):
    m = x.shape[0]
    return pl.pallas_call(
        _final_body,
        grid=(m // tm,),
        in_specs=[
            pl.BlockSpec((tm, D_MODEL), lambda i: (i, 0)),
            pl.BlockSpec((tm, PLE_DIM), lambda i: (i, 0)),
            pl.BlockSpec((PLE_DIM, D_MODEL), lambda i: (0, 0), pipeline_mode=pl.Buffered(1)),
            pl.BlockSpec((D_MODEL, D_MODEL), lambda i: (0, 0), pipeline_mode=pl.Buffered(1)),
            pl.BlockSpec((1, D_MODEL), lambda i: (0, 0)),
        ],
        out_specs=pl.BlockSpec((tm, D_MODEL), lambda i: (i, 0)),
        out_shape=jax.ShapeDtypeStruct((m, D_MODEL), F32),
        compiler_params=_cparams("parallel"),
        name="final",
    )(x, p, w_ple, w_ple_gate, gain)


def _tile_rows(m, want):
    return want if m % want == 0 else m


def _dense_tail(x, p, proj, pool_out, gdn_out, wts):
    m = x.shape[0]
    tm = _tile_rows(m, TM_DENSE)
    merged = _merge(proj, pool_out, gdn_out, wts["w_pool_up"], wts["w_gdn_up"], tm=tm, tn=TN_DENSE)
    x = _outproj(merged, x, wts["w_o"], tm=tm, tn=TN_DENSE)
    x = _ffn(x, wts["norm_ffn"], wts["w_gate_up"], wts["w_down"], tm=tm,
             tf=TF_FFN if tm == TM_DENSE else TF_FFN_SMALL)
    return _final(x, p, wts["w_ple"], wts["w_ple_gate"], wts["norm_final"], tm=tm)


def kernel(x_prompt, x_sample, p_prompt, p_sample, state_pool, state_conv, state_ssm, norm_mix, w_in, pool_w,
           pool_scale, conv_w, a_log, dt_bias, gdn_norm, w_pool_up, w_gdn_up, w_o, norm_ffn, w_gate_up, w_down,
           w_ple, w_ple_gate, norm_final):
    batch, seq, _ = x_prompt.shape
    nb = x_sample.shape[0]
    w_in_t = jnp.swapaxes(w_in[0], 0, 1)
    wts = dict(
        w_pool_up=w_pool_up[0].astype(BF16), w_gdn_up=w_gdn_up[0].astype(BF16), w_o=w_o[0].astype(BF16),
        norm_ffn=norm_ffn, w_gate_up=w_gate_up[0], w_down=w_down[0],
        w_ple=w_ple[0].astype(BF16), w_ple_gate=w_ple_gate[0].astype(BF16), norm_final=norm_final[None, :],
    )
    pool_w_b = pool_w[0].astype(BF16)
    lane_pad = ((0, 0), (0, LANES - HEADS))
    alog_pad = jnp.pad(a_log, lane_pad)
    dtb_pad = jnp.pad(dt_bias, lane_pad)
    gn = gdn_norm

    mp = batch * seq
    xp = x_prompt.reshape(mp, D_MODEL)
    xs = x_sample.reshape(nb, D_MODEL)
    h_p, ab_p = _norm_in(xp, norm_mix, w_in_t, tm=_tile_rows(mp, TM_NORM))
    h_s, ab_s = _norm_in(xs, norm_mix, w_in_t, tm=nb)
    proj_p, proj_s = _inproj(h_p, h_s, w_in_t, tm=_tile_rows(mp, TM_DENSE), tn=TN_INPROJ)

    pool_out_p = _pool_prompt(proj_p, pool_w_b, pool_scale, batch=batch, seq=seq, tt=_tile_rows(seq, TT_POOL))
    gdn_out_p, ssm_p = _gdn_prompt(proj_p, ab_p, conv_w[0], alog_pad, dtb_pad, gn, batch=batch, seq=seq,
                                   hg=GDN_HEADS_PER_STEP, tt=_tile_rows(seq, TT_GDN))
    y_p = _dense_tail(xp, p_prompt[0].reshape(mp, PLE_DIM), proj_p, pool_out_p, gdn_out_p, wts)
    proj_p3 = proj_p.reshape(batch, seq, C_ALL)
    pool_p = proj_p3[:, seq - POOL_BUF:, C_POOL:C_END][None]
    conv_p = proj_p3[:, seq - CONV_BUF:, C_Q:C_Z][None]

    pool_out_s = _pool_sample(proj_s, jnp.swapaxes(state_pool[0], 0, 1), pool_w_b, pool_scale)
    prep = _gdn_sample_prep(proj_s, ab_s, jnp.swapaxes(state_conv[0], 0, 1), conv_w[0], alog_pad, dtb_pad)
    to_heads = lambda a: a.reshape(nb, HEADS, HEAD_DIM)
    gdn_out_s, ssm_s = _gdn_sample_step(*[to_heads(a) for a in prep], to_heads(proj_s[:, C_Z:C_POOL]), gn,
                                        state_ssm, bb=_tile_rows(nb, BB_SAMPLE))
    y_s = _dense_tail(xs, p_sample[0].reshape(nb, PLE_DIM), proj_s, pool_out_s, gdn_out_s.reshape(nb, GDN_DIM), wts)
    pool_s = jnp.concatenate([state_pool[0][:, 1:], proj_s[:, None, C_POOL:C_END]], axis=1)[None]
    conv_s = jnp.concatenate([state_conv[0][:, 1:], proj_s[:, None, C_Q:C_Z]], axis=1)[None]

    return (y_p.reshape(batch, seq, D_MODEL), y_s.reshape(nb, 1, D_MODEL), pool_p, conv_p, ssm_p,
            pool_s, conv_s, ssm_s)
```

```python
import functools

import jax
import jax.numpy as jnp
from jax import lax
from jax.experimental import pallas as pl
from jax.experimental.pallas import tpu as pltpu

F32 = jnp.float32
BF16 = jnp.bfloat16

D_MODEL = 2048
PAST_LEN = 16384
POOL_WINDOWS = (2, 4, 8, 16)
POOL_GROUP_DIM = D_MODEL // 8
POOL_DIM = len(POOL_WINDOWS) * POOL_GROUP_DIM
POOL_BUF = max(POOL_WINDOWS) - 1
HEAD_DIM = 128
HEADS = D_MODEL // HEAD_DIM
GDN_DIM = HEADS * HEAD_DIM
QKV_DIM = 3 * GDN_DIM
CONV_WIDTH = 4
CONV_BUF = CONV_WIDTH - 1
D_FF = -(-8 * D_MODEL // (3 * 256)) * 256
PLE_DIM = 256
EPS = 1e-6

IN_QKV = POOL_DIM
IN_Z = IN_QKV + QKV_DIM
IN_AB = IN_Z + GDN_DIM
IN_GATES = IN_AB + 2 * HEADS
C_Q = 0
C_K = C_Q + GDN_DIM
C_V = C_K + GDN_DIM
C_Z = C_V + GDN_DIM
C_POOL = C_Z + GDN_DIM
C_END = C_POOL + POOL_DIM
C_GP = C_END
C_GG = C_GP + D_MODEL
C_ALL = C_GG + D_MODEL

LANES = 128
SUBLANES = 8
VMEM_LIMIT_BYTES = 56 * 1024 * 1024
CHUNK = 128

TM_DENSE = 1024
TN_INPROJ = 1024
TN_DENSE = 1024
TF_FFN = 256
TF_FFN_SMALL = 512
TM_NORM = 1024
TT_POOL = 1024
GDN_HEADS_PER_STEP = 8
TT_GDN = 256
BB_SAMPLE = 8


def _cparams(*sem):
    return pltpu.CompilerParams(dimension_semantics=sem, vmem_limit_bytes=VMEM_LIMIT_BYTES)


def _rms(x, gain):
    return x * lax.rsqrt(jnp.mean(x * x, axis=-1, keepdims=True) + EPS) * gain


def _dot(a, b):
    return jnp.dot(a, b, preferred_element_type=F32)


def _dot_nt(a, b):
    return lax.dot_general(a, b, (((1,), (1,)), ((), ())), preferred_element_type=F32)


def _silu(x):
    return x * jax.nn.sigmoid(x)


def _norm_in_body(x_ref, gain_ref, wab_ref, h_ref, ab_ref):
    h = _rms(x_ref[...], gain_ref[...]).astype(BF16)
    h_ref[...] = h
    ab_ref[...] = _dot_nt(h, wab_ref[...].astype(BF16))


def _norm_in(x, gain, w_in_t, *, tm):
    m = x.shape[0]
    return pl.pallas_call(
        _norm_in_body,
        grid=(m // tm,),
        in_specs=[
            pl.BlockSpec((tm, D_MODEL), lambda i: (i, 0)),
            pl.BlockSpec((1, D_MODEL), lambda i: (0, 0)),
            pl.BlockSpec((LANES, D_MODEL), lambda i: (IN_AB // LANES, 0)),
        ],
        out_specs=[
            pl.BlockSpec((tm, D_MODEL), lambda i: (i, 0)),
            pl.BlockSpec((tm, LANES), lambda i: (i, 0)),
        ],
        out_shape=[jax.ShapeDtypeStruct((m, D_MODEL), BF16), jax.ShapeDtypeStruct((m, LANES), F32)],
        compiler_params=_cparams("parallel"),
        name="norm_in",
    )(x, gain, w_in_t)


def _inproj_body(hp_ref, hs_ref, w_ref, op_ref, os_ref, wb_ref):
    i = pl.program_id(1)

    @pl.when(i == 0)
    def _():
        wb_ref[...] = w_ref[...].astype(BF16)
        os_ref[...] = _dot_nt(hs_ref[...], wb_ref[...])

    @pl.when(i > 0)
    def _():
        op_ref[...] = _dot_nt(hp_ref[...], wb_ref[...])


def _inproj(h_prompt, h_sample, w_in_t, *, tm, tn):
    mp, ms = h_prompt.shape[0], h_sample.shape[0]
    n_p = mp // tm
    n_out, out_dtype = C_ALL, F32
    n_mid = (QKV_DIM + GDN_DIM) // tn
    n_pool = POOL_DIM // tn

    def row_start(j):
        return jnp.where(j < n_mid, IN_QKV + j * tn,
                         jnp.where(j < n_mid + n_pool, (j - n_mid) * tn, IN_GATES + (j - n_mid - n_pool) * tn))

    return pl.pallas_call(
        _inproj_body,
        grid=(n_out // tn, n_p + 1),
        in_specs=[
            pl.BlockSpec((tm, D_MODEL), lambda j, i: (jnp.maximum(i - 1, 0), 0)),
            pl.BlockSpec((ms, D_MODEL), lambda j, i: (0, 0)),
            pl.BlockSpec((pl.Element(tn), pl.Element(D_MODEL)),
                         lambda j, i: (pl.multiple_of(row_start(j), SUBLANES), 0)),
        ],
        out_specs=[pl.BlockSpec((tm, tn), lambda j, i: (jnp.maximum(i - 1, 0), j)),
                   pl.BlockSpec((ms, tn), lambda j, i: (0, j))],
        out_shape=[jax.ShapeDtypeStruct((mp, n_out), out_dtype), jax.ShapeDtypeStruct((ms, n_out), out_dtype)],
        scratch_shapes=[pltpu.VMEM((tn, D_MODEL), BF16)],
        compiler_params=_cparams("arbitrary", "arbitrary"),
        name="inproj",
    )(h_prompt, h_sample, w_in_t)


def _pool_group_out(mean, tok, pw_ref, scale_ref, gi):
    cols = slice(gi * POOL_GROUP_DIM, (gi + 1) * POOL_GROUP_DIM)
    y = _dot((mean - tok).astype(BF16), pw_ref[gi])
    return (y * scale_ref[:, cols]).astype(BF16)


def _pool_prompt_body(u_ref, pw_ref, scale_ref, o_ref, ext_ref, *, tt, start_pos):
    t = pl.program_id(1)
    halo = POOL_BUF + 1

    @pl.when(t == 0)
    def _():
        ext_ref[0:halo, :] = jnp.zeros((halo, POOL_DIM), F32)

    ext_ref[halo:halo + tt, :] = u_ref[...]
    pos = start_pos + t * tt + lax.broadcasted_iota(jnp.int32, (tt, 1), 0)
    for gi, w in enumerate(POOL_WINDOWS):
        cols = slice(gi * POOL_GROUP_DIM, (gi + 1) * POOL_GROUP_DIM)
        tok = ext_ref[halo:halo + tt, cols]
        tot = tok
        for i in range(1, w):
            tot = tot + ext_ref[pl.ds(halo - i, tt), cols]
        cnt = jnp.minimum(pos + 1, w).astype(F32)
        o_ref[:, cols] = _pool_group_out(tot / cnt, tok, pw_ref, scale_ref, gi)
    ext_ref[0:halo, :] = ext_ref[tt:tt + halo, :]


def _pool_prompt(proj, pool_w, pool_scale, *, batch, seq, tt):
    nt = seq // tt
    body = functools.partial(_pool_prompt_body, tt=tt, start_pos=0)
    return pl.pallas_call(
        body,
        grid=(batch, nt),
        in_specs=[
            pl.BlockSpec((tt, POOL_DIM), lambda b, t: (b * nt + t, C_POOL // POOL_DIM)),
            pl.BlockSpec((len(POOL_WINDOWS), POOL_GROUP_DIM, POOL_GROUP_DIM), lambda b, t: (0, 0, 0)),
            pl.BlockSpec((1, POOL_DIM), lambda b, t: (0, 0)),
        ],
        out_specs=pl.BlockSpec((tt, POOL_DIM), lambda b, t: (b * nt + t, 0)),
        out_shape=jax.ShapeDtypeStruct((batch * seq, POOL_DIM), BF16),
        scratch_shapes=[pltpu.VMEM((POOL_BUF + 1 + tt, POOL_DIM), F32)],
        compiler_params=_cparams("parallel", "arbitrary"),
        name="pool_prompt",
    )(proj, pool_w, pool_scale)


def _pool_sample_body(u_ref, st_ref, pw_ref, scale_ref, o_ref, *, start_pos):
    for gi, w in enumerate(POOL_WINDOWS):
        cols = slice(gi * POOL_GROUP_DIM, (gi + 1) * POOL_GROUP_DIM)
        tok = u_ref[:, cols]
        tot = tok
        for i in range(1, w):
            tot = tot + st_ref[POOL_BUF - i, :, cols]
        cnt = float(min(start_pos + 1, w))
        o_ref[:, cols] = _pool_group_out(tot / cnt, tok, pw_ref, scale_ref, gi)


def _pool_sample(proj, state_t, pool_w, pool_scale):
    nb = proj.shape[0]
    body = functools.partial(_pool_sample_body, start_pos=PAST_LEN)
    return pl.pallas_call(
        body,
        grid=(1,),
        in_specs=[
            pl.BlockSpec((nb, POOL_DIM), lambda i: (0, C_POOL // POOL_DIM)),
            pl.BlockSpec((POOL_BUF, nb, POOL_DIM), lambda i: (0, 0, 0)),
            pl.BlockSpec((len(POOL_WINDOWS), POOL_GROUP_DIM, POOL_GROUP_DIM), lambda i: (0, 0, 0)),
            pl.BlockSpec((1, POOL_DIM), lambda i: (0, 0)),
        ],
        out_specs=pl.BlockSpec((nb, POOL_DIM), lambda i: (0, 0)),
        out_shape=jax.ShapeDtypeStruct((nb, POOL_DIM), BF16),
        compiler_params=_cparams("arbitrary"),
        name="pool_sample",
    )(proj, state_t, pool_w, pool_scale)


def _split3(x):
    hi = x.astype(BF16)
    r1 = x - hi.astype(F32)
    mid = r1.astype(BF16)
    lo = (r1 - mid.astype(F32)).astype(BF16)
    return hi, mid, lo


def _softplus(x):
    return jnp.maximum(x, 0.0) + jnp.log1p(jnp.exp(-jnp.abs(x)))


def _unit_lower_inverse_minus_identity(mats):
    eye = jnp.where(lax.broadcasted_iota(jnp.int32, (CHUNK, CHUNK), 0)
                    == lax.broadcasted_iota(jnp.int32, (CHUNK, CHUNK), 1), 1.0, 0.0)
    levels = CHUNK.bit_length() - 1
    qbs = [(-a).astype(BF16) for a in mats]
    rs = [eye - a for a in mats]
    qs = [_dot(qb, qb) for qb in qbs]
    for j in range(1, levels):
        qbs = [q.astype(BF16) for q in qs]
        rbs = [r.astype(BF16) for r in rs]
        if j < levels - 1:
            ps = [_dot(qb, jnp.concatenate([qb, rb], axis=1)) for qb, rb in zip(qbs, rbs)]
            rs = [r + p[:, CHUNK:] for r, p in zip(rs, ps)]
            qs = [p[:, :CHUNK] for p in ps]
        else:
            rs = [r + _dot(qb, rb) for r, qb, rb in zip(rs, qbs, rbs)]
    return [r - eye for r in rs]


def _gdn_prompt_body(q_ref, k_ref, v_ref, z_ref, ab_ref, cwq_ref, cwk_ref, cwv_ref, alog_ref, dtb_ref, gn_ref,
                     o_ref, ssm_ref, ext_ref, s_ref, gct_ref, *, hg, tt):
    hgi = pl.program_id(1)
    t = pl.program_id(2)
    w = hg * HEAD_DIM
    halo = SUBLANES

    @pl.when(t == 0)
    def _():
        ext_ref[0:halo, :] = jnp.zeros((halo, 3 * w), F32)
        s_ref[...] = jnp.zeros_like(s_ref)

    ext_ref[halo:halo + tt, 0:w] = q_ref[...]
    ext_ref[halo:halo + tt, w:2 * w] = k_ref[...]
    ext_ref[halo:halo + tt, 2 * w:3 * w] = v_ref[...]

    def conv(part, cw_ref):
        cols = slice(part * w, (part + 1) * w)
        acc = ext_ref[halo:halo + tt, cols] * cw_ref[CONV_BUF:CONV_WIDTH, :]
        for i in range(CONV_BUF):
            acc = acc + ext_ref[pl.ds(halo - CONV_BUF + i, tt), cols] * cw_ref[i:i + 1, :]
        return _silu(acc)

    qc = conv(0, cwq_ref)
    kc = conv(1, cwk_ref)
    vc = conv(2, cwv_ref)
    ext_ref[0:halo, :] = ext_ref[tt:tt + halo, :]

    ab = ab_ref[...]
    g_all = -jnp.exp(alog_ref[...]) * _softplus(ab + dtb_ref[...])
    beta_all = jax.nn.sigmoid(ab)
    ri = lax.broadcasted_iota(jnp.int32, (tt, tt), 0)
    ci = lax.broadcasted_iota(jnp.int32, (tt, tt), 1)
    shift = CHUNK.bit_length() - 1
    tri = jnp.where(((ri >> shift) == (ci >> shift)) & (ci <= ri), 1.0, 0.0).astype(BF16)
    g_hi, g_mid, g_lo = _split3(g_all)
    gc_all = _dot(tri, g_hi) + _dot(tri, g_mid) + _dot(tri, g_lo)
    gct_ref[...] = gc_all.T

    lane = lax.broadcasted_iota(jnp.int32, (tt, LANES), 1)
    r128 = lax.broadcasted_iota(jnp.int32, (CHUNK, CHUNK), 0)
    c128 = lax.broadcasted_iota(jnp.int32, (CHUNK, CHUNK), 1)
    sub8 = lax.broadcasted_iota(jnp.int32, (SUBLANES, CHUNK), 0)
    n_chunks = tt // CHUNK
    chains = [(hh, c) for hh in range(hg) for c in range(n_chunks)]
    gcol_heads, bcol_heads = [], []
    for hh in range(hg):
        head = hgi * hg + hh
        gcol_heads.append(jnp.sum(jnp.where(lane == head, gc_all, 0.0), axis=-1, keepdims=True))
        bcol_heads.append(jnp.sum(jnp.where(lane == head + HEADS, beta_all, 0.0), axis=-1, keepdims=True))

    qn_l, kn_l, kb_l, rhs_l, eg_l, gcol_l, dec_l = [], [], [], [], [], [], []
    for hh, c in chains:
        head = hgi * hg + hh
        rows = slice(c * CHUNK, (c + 1) * CHUNK)
        lanes = slice(hh * HEAD_DIM, (hh + 1) * HEAD_DIM)
        qh, kh, vh = qc[rows, lanes], kc[rows, lanes], vc[rows, lanes]
        qn = qh * lax.rsqrt(jnp.sum(qh * qh, axis=-1, keepdims=True) + EPS) * (HEAD_DIM ** -0.5)
        kn = kh * lax.rsqrt(jnp.sum(kh * kh, axis=-1, keepdims=True) + EPS)
        gcol = gcol_heads[hh][rows]
        bcol = bcol_heads[hh][rows]
        gblk = gct_ref[pl.ds(pl.multiple_of((head // SUBLANES) * SUBLANES, SUBLANES), SUBLANES), rows]
        grow = jnp.sum(jnp.where(sub8 == (head & (SUBLANES - 1)), gblk, 0.0), axis=0, keepdims=True)
        eg = jnp.exp(gcol)
        kb = kn * bcol
        qn_l.append(qn)
        kn_l.append(kn)
        kb_l.append(kb)
        eg_l.append(eg)
        gcol_l.append(gcol)
        dec_l.append(jnp.exp(jnp.minimum(gcol - grow, 0.0)))
        rhs_l.append(jnp.concatenate([vh * bcol, kb * eg], axis=1))

    a_l, qk_l = [], []
    for qn, kn, kb, decay in zip(qn_l, kn_l, kb_l, dec_l):
        knb = kn.astype(BF16)
        a_l.append(_dot_nt(kb.astype(BF16), knb) * jnp.where(c128 < r128, decay, 0.0))
        qk_l.append(_dot_nt(qn.astype(BF16), knb) * jnp.where(c128 <= r128, decay, 0.0))
    tinv_l = [x.astype(BF16) for x in _unit_lower_inverse_minus_identity(a_l)]
    sol_l = [rhs + _dot(tinv, rhs.astype(BF16)) for tinv, rhs in zip(tinv_l, rhs_l)]
    res_l = []
    for a, sol, rhs in zip(a_l, sol_l, rhs_l):
        a_sol = _dot(a.astype(BF16), sol.astype(BF16))
        res_l.append((rhs - sol) - a_sol)
    sol_l = [(sol + res + _dot(tinv, res.astype(BF16))).astype(BF16) for sol, res, tinv in zip(sol_l, res_l, tinv_l)]
    lhs_l, o_add_l, s_add_l, glast_l = [], [], [], []
    for qn, kn, eg, gcol, qk, sol_b in zip(qn_l, kn_l, eg_l, gcol_l, qk_l, sol_l):
        glast = gcol[CHUNK - 1:CHUNK, :]
        ktail = kn * jnp.exp(glast - gcol)
        kt_sol = _dot(ktail.T.astype(BF16), sol_b)
        qk_sol = _dot(qk.astype(BF16), sol_b)
        lhs_l.append(jnp.concatenate([qn * eg - qk_sol[:, HEAD_DIM:], kt_sol[:, HEAD_DIM:]], axis=0).astype(BF16))
        o_add_l.append(qk_sol[:, :HEAD_DIM])
        s_add_l.append(kt_sol[:, :HEAD_DIM])
        glast_l.append(jnp.exp(glast))
    states = [s_ref[hh] for hh in range(hg)]
    for c in range(n_chunks):
        for hh in range(hg):
            i = hh * n_chunks + c
            rows = slice(c * CHUNK, (c + 1) * CHUNK)
            lanes = slice(hh * HEAD_DIM, (hh + 1) * HEAD_DIM)
            on_s = _dot(lhs_l[i], states[hh].astype(BF16))
            o = on_s[:CHUNK] + o_add_l[i]
            states[hh] = states[hh] * glast_l[i] - on_s[CHUNK:] + s_add_l[i]
            o_ref[rows, lanes] = (_rms(o, gn_ref[...]) * _silu(z_ref[rows, lanes])).astype(BF16)
    for hh in range(hg):
        s_ref[hh] = states[hh]

    @pl.when(t == pl.num_programs(2) - 1)
    def _():
        ssm_ref[0, 0] = s_ref[...]


def _gdn_prompt(proj, ab, conv_w, alog_pad, dtb_pad, gdn_norm, *, batch, seq, hg, tt):
    nt = seq // tt
    w = hg * HEAD_DIM
    body = functools.partial(_gdn_prompt_body, hg=hg, tt=tt)

    def col_spec(col0):
        return pl.BlockSpec((tt, w), lambda b, h, t: (b * nt + t, col0 // w + h))

    def cw_spec(part):
        return pl.BlockSpec((CONV_WIDTH, w), lambda b, h, t: (0, part * GDN_DIM // w + h))

    small = pl.BlockSpec((1, LANES), lambda b, h, t: (0, 0))
    return pl.pallas_call(
        body,
        grid=(batch, HEADS // hg, nt),
        in_specs=[col_spec(C_Q), col_spec(C_K), col_spec(C_V), col_spec(C_Z),
                  pl.BlockSpec((tt, LANES), lambda b, h, t: (b * nt + t, 0)),
                  cw_spec(0), cw_spec(1), cw_spec(2), small, small, small],
        out_specs=[
            pl.BlockSpec((tt, w), lambda b, h, t: (b * nt + t, h)),
            pl.BlockSpec((1, 1, hg, HEAD_DIM, HEAD_DIM), lambda b, h, t: (0, b, h, 0, 0)),
        ],
        out_shape=[
            jax.ShapeDtypeStruct((batch * seq, GDN_DIM), BF16),
            jax.ShapeDtypeStruct((1, batch, HEADS, HEAD_DIM, HEAD_DIM), F32),
        ],
        scratch_shapes=[
            pltpu.VMEM((SUBLANES + tt, 3 * w), F32),
            pltpu.VMEM((hg, HEAD_DIM, HEAD_DIM), F32),
            pltpu.VMEM((LANES, tt), F32),
        ],
        compiler_params=_cparams("parallel", "parallel", "arbitrary"),
        name="gdn_prompt",
    )(proj, proj, proj, proj, ab, conv_w, conv_w, conv_w, alog_pad, dtb_pad, gdn_norm)


def _gdn_sample_prep_body(q_ref, k_ref, v_ref, ab_ref, st_ref, cw_ref, alog_ref, dtb_ref,
                          qn_ref, kn_ref, vc_ref, eg_ref, beta_ref, qk_ref):
    def conv(part, u_ref):
        cols = slice(part * GDN_DIM, (part + 1) * GDN_DIM)
        acc = u_ref[...] * cw_ref[CONV_BUF:CONV_WIDTH, cols]
        for i in range(CONV_BUF):
            acc = acc + st_ref[i, :, cols] * cw_ref[i:i + 1, cols]
        return _silu(acc)

    qc = conv(0, q_ref)
    kc = conv(1, k_ref)
    vc_ref[...] = conv(2, v_ref)
    ab = ab_ref[...]
    g_all = -jnp.exp(alog_ref[...]) * _softplus(ab + dtb_ref[...])
    eg_all = jnp.exp(g_all)
    beta_all = jax.nn.sigmoid(ab)
    for h in range(HEADS):
        lanes = slice(h * HEAD_DIM, (h + 1) * HEAD_DIM)
        qh, kh = qc[:, lanes], kc[:, lanes]
        qn = qh * lax.rsqrt(jnp.sum(qh * qh, axis=-1, keepdims=True) + EPS) * (HEAD_DIM ** -0.5)
        kn = kh * lax.rsqrt(jnp.sum(kh * kh, axis=-1, keepdims=True) + EPS)
        qn_ref[:, lanes] = qn
        kn_ref[:, lanes] = kn
        shape = qn.shape
        qk_ref[:, lanes] = jnp.broadcast_to(jnp.sum(qn * kn, axis=-1, keepdims=True), shape)
        eg_ref[:, lanes] = jnp.broadcast_to(eg_all[:, h:h + 1], shape)
        beta_ref[:, lanes] = jnp.broadcast_to(beta_all[:, HEADS + h:HEADS + h + 1], shape)


def _gdn_sample_prep(proj, ab, conv_state_t, conv_w, alog_pad, dtb_pad):
    nb = proj.shape[0]

    def col_spec(col0):
        return pl.BlockSpec((nb, GDN_DIM), lambda i: (0, col0 // GDN_DIM))

    small = pl.BlockSpec((1, LANES), lambda i: (0, 0))
    full = pl.BlockSpec((nb, GDN_DIM), lambda i: (0, 0))
    return pl.pallas_call(
        _gdn_sample_prep_body,
        grid=(1,),
        in_specs=[col_spec(C_Q), col_spec(C_K), col_spec(C_V),
                  pl.BlockSpec((nb, LANES), lambda i: (0, 0)),
                  pl.BlockSpec((CONV_BUF, nb, QKV_DIM), lambda i: (0, 0, 0)),
                  pl.BlockSpec((CONV_WIDTH, QKV_DIM), lambda i: (0, 0)),
                  small, small],
        out_specs=[full] * 6,
        out_shape=[jax.ShapeDtypeStruct((nb, GDN_DIM), F32)] * 6,
        compiler_params=_cparams("arbitrary"),
        name="gdn_sample_prep",
    )(proj, proj, proj, ab, conv_state_t, conv_w, alog_pad, dtb_pad)


def _gdn_sample_step_body(qn_ref, kn_ref, vc_ref, eg_ref, beta_ref, qk_ref, z_ref, gn_ref, s_ref,
                          o_ref, snew_ref, *, bb):
    pad = jnp.zeros((LANES - HEADS, HEAD_DIM), F32)
    for b in range(bb):
        k16, q16 = kn_ref[b], qn_ref[b]
        cols = jnp.concatenate([k16, pad], axis=0).T
        kq = jnp.concatenate([k16, q16], axis=0).astype(BF16)
        outs = []
        for h in range(HEADS):
            kcol = cols[:, h:h + 1]
            s = s_ref[0, b, h]
            kq_s = _dot(kq, s.astype(BF16))
            k_s = kq_s[h:h + 1]
            q_s = kq_s[HEADS + h:HEADS + h + 1]
            eg = eg_ref[b, h:h + 1, :]
            v_new = beta_ref[b, h:h + 1, :] * (vc_ref[b, h:h + 1, :] - eg * k_s)
            snew_ref[0, b, h] = eg * s + kcol * v_new
            outs.append(eg * q_s + qk_ref[b, h:h + 1, :] * v_new)
        o = jnp.concatenate(outs, axis=0)
        o_ref[b] = (_rms(o, gn_ref[...]) * _silu(z_ref[b])).astype(BF16)


def _gdn_sample_step(qn, kn, vc, eg, beta, qk, z, gdn_norm, state_ssm, *, bb):
    nb = qn.shape[0]
    body = functools.partial(_gdn_sample_step_body, bb=bb)
    vec = pl.BlockSpec((bb, HEADS, HEAD_DIM), lambda i: (i, 0, 0))
    st = pl.BlockSpec((1, bb, HEADS, HEAD_DIM, HEAD_DIM), lambda i: (0, i, 0, 0, 0))
    return pl.pallas_call(
        body,
        grid=(nb // bb,),
        in_specs=[vec] * 7 + [pl.BlockSpec((1, LANES), lambda i: (0, 0)), st],
        out_specs=[vec, st],
        out_shape=[jax.ShapeDtypeStruct((nb, HEADS, HEAD_DIM), BF16),
                   jax.ShapeDtypeStruct(state_ssm.shape, F32)],
        compiler_params=_cparams("parallel"),
        name="gdn_sample_step",
    )(qn, kn, vc, eg, beta, qk, z, gdn_norm, state_ssm)


def _merge_body(pool_ref, gdn_ref, wpu_ref, wgu_ref, gp_ref, gg_ref, o_ref):
    o_ref[...] = (jax.nn.sigmoid(gp_ref[...]) * _dot(pool_ref[...], wpu_ref[...])
                  + jax.nn.sigmoid(gg_ref[...]) * _dot(gdn_ref[...], wgu_ref[...])).astype(BF16)


def _merge(proj, pool_out, gdn_out, w_pool_up, w_gdn_up, *, tm, tn):
    m = pool_out.shape[0]
    nj = D_MODEL // tn
    return pl.pallas_call(
        _merge_body,
        grid=(m // tm, nj),
        in_specs=[
            pl.BlockSpec((tm, POOL_DIM), lambda i, j: (i, 0)),
            pl.BlockSpec((tm, GDN_DIM), lambda i, j: (i, 0)),
            pl.BlockSpec((POOL_DIM, tn), lambda i, j: (0, j)),
            pl.BlockSpec((GDN_DIM, tn), lambda i, j: (0, j)),
            pl.BlockSpec((tm, tn), lambda i, j: (i, C_GP // tn + j)),
            pl.BlockSpec((tm, tn), lambda i, j: (i, C_GG // tn + j)),
        ],
        out_specs=pl.BlockSpec((tm, tn), lambda i, j: (i, j)),
        out_shape=jax.ShapeDtypeStruct((m, D_MODEL), BF16),
        compiler_params=_cparams("parallel", "parallel"),
        name="merge",
    )(pool_out, gdn_out, w_pool_up, w_gdn_up, proj, proj)


def _outproj_body(a_ref, w_ref, x_ref, o_ref):
    o_ref[...] = x_ref[...] + _dot(a_ref[...], w_ref[...])


def _outproj(merged, x, w_o, *, tm, tn):
    m = x.shape[0]
    return pl.pallas_call(
        _outproj_body,
        grid=(m // tm, D_MODEL // tn),
        in_specs=[
            pl.BlockSpec((tm, D_MODEL), lambda i, j: (i, 0)),
            pl.BlockSpec((D_MODEL, tn), lambda i, j: (0, j)),
            pl.BlockSpec((tm, tn), lambda i, j: (i, j)),
        ],
        out_specs=pl.BlockSpec((tm, tn), lambda i, j: (i, j)),
        out_shape=jax.ShapeDtypeStruct((m, D_MODEL), F32),
        compiler_params=_cparams("parallel", "parallel"),
        name="outproj",
    )(merged, w_o, x)


def _ffn_body(x_ref, gain_ref, wg_ref, wu_ref, wd_ref, o_ref, h_ref):
    f = pl.program_id(1)

    @pl.when(f == 0)
    def _():
        x = x_ref[...]
        h_ref[...] = _rms(x, gain_ref[...]).astype(BF16)
        o_ref[...] = x

    h = h_ref[...]
    act = (_silu(_dot(h, wg_ref[...].astype(BF16))) * _dot(h, wu_ref[...].astype(BF16))).astype(BF16)
    o_ref[...] += _dot(act, wd_ref[...].astype(BF16))


def _ffn(x, gain, w_gate_up, w_down, *, tm, tf):
    m = x.shape[0]
    nf = D_FF // tf
    return pl.pallas_call(
        _ffn_body,
        grid=(m // tm, nf),
        in_specs=[
            pl.BlockSpec((tm, D_MODEL), lambda i, f: (i, 0)),
            pl.BlockSpec((1, D_MODEL), lambda i, f: (0, 0)),
            pl.BlockSpec((D_MODEL, tf), lambda i, f: (0, f)),
            pl.BlockSpec((D_MODEL, tf), lambda i, f: (0, nf + f)),
            pl.BlockSpec((tf, D_MODEL), lambda i, f: (f, 0)),
        ],
        out_specs=pl.BlockSpec((tm, D_MODEL), lambda i, f: (i, 0)),
        out_shape=jax.ShapeDtypeStruct((m, D_MODEL), F32),
        scratch_shapes=[pltpu.VMEM((tm, D_MODEL), BF16)],
        compiler_params=_cparams("parallel", "arbitrary"),
        name="ffn",
    )(x, gain, w_gate_up, w_gate_up, w_down)


def _final_body(x_ref, p_ref, wple_ref, wpg_ref, gain_ref, o_ref):
    x = x_ref[...]
    gate = jax.nn.sigmoid(_dot(x.astype(BF16), wpg_ref[...]))
    emb = _dot(p_ref[...].astype(BF16), wple_ref[...])
    o_ref[...] = _rms(x + emb * gate, gain_ref[...])


def _final(x, p, w_ple, w_ple_gate, gain, *, tm):
    m = x.shape[0]
    return pl.pallas_call(
        _final_body,
        grid=(m // tm,),
        in_specs=[
            pl.BlockSpec((tm, D_MODEL), lambda i: (i, 0)),
            pl.BlockSpec((tm, PLE_DIM), lambda i: (i, 0)),
            pl.BlockSpec((PLE_DIM, D_MODEL), lambda i: (0, 0), pipeline_mode=pl.Buffered(1)),
            pl.BlockSpec((D_MODEL, D_MODEL), lambda i: (0, 0), pipeline_mode=pl.Buffered(1)),
            pl.BlockSpec((1, D_MODEL), lambda i: (0, 0)),
        ],
        out_specs=pl.BlockSpec((tm, D_MODEL), lambda i: (i, 0)),
        out_shape=jax.ShapeDtypeStruct((m, D_MODEL), F32),
        compiler_params=_cparams("parallel"),
        name="final",
    )(x, p, w_ple, w_ple_gate, gain)


def _tile_rows(m, want):
    return want if m % want == 0 else m


def _dense_tail(x, p, proj, pool_out, gdn_out, wts):
    m = x.shape[0]
    tm = _tile_rows(m, TM_DENSE)
    merged = _merge(proj, pool_out, gdn_out, wts["w_pool_up"], wts["w_gdn_up"], tm=tm, tn=TN_DENSE)
    x = _outproj(merged, x, wts["w_o"], tm=tm, tn=TN_DENSE)
    x = _ffn(x, wts["norm_ffn"], wts["w_gate_up"], wts["w_down"], tm=tm,
             tf=TF_FFN if tm == TM_DENSE else TF_FFN_SMALL)
    return _final(x, p, wts["w_ple"], wts["w_ple_gate"], wts["norm_final"], tm=tm)


def kernel(x_prompt, x_sample, p_prompt, p_sample, state_pool, state_conv, state_ssm, norm_mix, w_in, pool_w,
           pool_scale, conv_w, a_log, dt_bias, gdn_norm, w_pool_up, w_gdn_up, w_o, norm_ffn, w_gate_up, w_down,
           w_ple, w_ple_gate, norm_final):
    batch, seq, _ = x_prompt.shape
    nb = x_sample.shape[0]
    w_in_t = jnp.swapaxes(w_in[0], 0, 1)
    wts = dict(
        w_pool_up=w_pool_up[0].astype(BF16), w_gdn_up=w_gdn_up[0].astype(BF16), w_o=w_o[0].astype(BF16),
        norm_ffn=norm_ffn, w_gate_up=w_gate_up[0], w_down=w_down[0],
        w_ple=w_ple[0].astype(BF16), w_ple_gate=w_ple_gate[0].astype(BF16), norm_final=norm_final[None, :],
    )
    pool_w_b = pool_w[0].astype(BF16)
    lane_pad = ((0, 0), (0, LANES - HEADS))
    alog_pad = jnp.pad(a_log, lane_pad)
    dtb_pad = jnp.pad(dt_bias, lane_pad)
    gn = gdn_norm

    mp = batch * seq
    xp = x_prompt.reshape(mp, D_MODEL)
    xs = x_sample.reshape(nb, D_MODEL)
    h_p, ab_p = _norm_in(xp, norm_mix, w_in_t, tm=_tile_rows(mp, TM_NORM))
    h_s, ab_s = _norm_in(xs, norm_mix, w_in_t, tm=nb)
    proj_p, proj_s = _inproj(h_p, h_s, w_in_t, tm=_tile_rows(mp, TM_DENSE), tn=TN_INPROJ)

    pool_out_p = _pool_prompt(proj_p, pool_w_b, pool_scale, batch=batch, seq=seq, tt=_tile_rows(seq, TT_POOL))
    gdn_out_p, ssm_p = _gdn_prompt(proj_p, ab_p, conv_w[0], alog_pad, dtb_pad, gn, batch=batch, seq=seq,
                                   hg=GDN_HEADS_PER_STEP, tt=_tile_rows(seq, TT_GDN))
    y_p = _dense_tail(xp, p_prompt[0].reshape(mp, PLE_DIM), proj_p, pool_out_p, gdn_out_p, wts)
    proj_p3 = proj_p.reshape(batch, seq, C_ALL)
    pool_p = proj_p3[:, seq - POOL_BUF:, C_POOL:C_END][None]
    conv_p = proj_p3[:, seq - CONV_BUF:, C_Q:C_Z][None]

    pool_out_s = _pool_sample(proj_s, jnp.swapaxes(state_pool[0], 0, 1), pool_w_b, pool_scale)
    prep = _gdn_sample_prep(proj_s, ab_s, jnp.swapaxes(state_conv[0], 0, 1), conv_w[0], alog_pad, dtb_pad)
    to_heads = lambda a: a.reshape(nb, HEADS, HEAD_DIM)
    gdn_out_s, ssm_s = _gdn_sample_step(*[to_heads(a) for a in prep], to_heads(proj_s[:, C_Z:C_POOL]), gn,
                                        state_ssm, bb=_tile_rows(nb, BB_SAMPLE))
    y_s = _dense_tail(xs, p_sample[0].reshape(nb, PLE_DIM), proj_s, pool_out_s, gdn_out_s.reshape(nb, GDN_DIM), wts)
    pool_s = jnp.concatenate([state_pool[0][:, 1:], proj_s[:, None, C_POOL:C_END]], axis=1)[None]
    conv_s = jnp.concatenate([state_conv[0][:, 1:], proj_s[:, None, C_Q:C_Z]], axis=1)[None]

    return (y_p.reshape(batch, seq, D_MODEL), y_s.reshape(nb, 1, D_MODEL), pool_p, conv_p, ssm_p,
            pool_s, conv_s, ssm_s)
```

```python
import functools

import jax
import jax.numpy as jnp
from jax import lax
from jax.experimental import pallas as pl
from jax.experimental.pallas import tpu as pltpu

F32 = jnp.float32
BF16 = jnp.bfloat16

D_MODEL = 2048
PAST_LEN = 16384
POOL_WINDOWS = (2, 4, 8, 16)
POOL_GROUP_DIM = D_MODEL // 8
POOL_DIM = len(POOL_WINDOWS) * POOL_GROUP_DIM
POOL_BUF = max(POOL_WINDOWS) - 1
HEAD_DIM = 128
HEADS = D_MODEL // HEAD_DIM
GDN_DIM = HEADS * HEAD_DIM
QKV_DIM = 3 * GDN_DIM
CONV_WIDTH = 4
CONV_BUF = CONV_WIDTH - 1
D_FF = -(-8 * D_MODEL // (3 * 256)) * 256
PLE_DIM = 256
EPS = 1e-6

IN_QKV = POOL_DIM
IN_Z = IN_QKV + QKV_DIM
IN_AB = IN_Z + GDN_DIM
IN_GATES = IN_AB + 2 * HEADS
C_Q = 0
C_K = C_Q + GDN_DIM
C_V = C_K + GDN_DIM
C_Z = C_V + GDN_DIM
C_POOL = C_Z + GDN_DIM
C_END = C_POOL + POOL_DIM
C_GP = C_END
C_GG = C_GP + D_MODEL
C_ALL = C_GG + D_MODEL

LANES = 128
SUBLANES = 8
VMEM_LIMIT_BYTES = 56 * 1024 * 1024
CHUNK = 128

TM_DENSE = 1024
TN_INPROJ = 1024
TN_DENSE = 1024
TF_FFN = 256
TF_FFN_SMALL = 512
TM_NORM = 1024
TT_POOL = 1024
GDN_HEADS_PER_STEP = 8
TT_GDN = 256
BB_SAMPLE = 8


def _cparams(*sem):
    return pltpu.CompilerParams(dimension_semantics=sem, vmem_limit_bytes=VMEM_LIMIT_BYTES)


def _rms(x, gain):
    return x * lax.rsqrt(jnp.mean(x * x, axis=-1, keepdims=True) + EPS) * gain


def _dot(a, b):
    return jnp.dot(a, b, preferred_element_type=F32)


def _dot_nt(a, b):
    return lax.dot_general(a, b, (((1,), (1,)), ((), ())), preferred_element_type=F32)


def _silu(x):
    return x * jax.nn.sigmoid(x)


def _norm_in_body(x_ref, gain_ref, wab_ref, h_ref, ab_ref):
    h = _rms(x_ref[...], gain_ref[...]).astype(BF16)
    h_ref[...] = h
    ab_ref[...] = _dot_nt(h, wab_ref[...].astype(BF16))


def _norm_in(x, gain, w_in_t, *, tm):
    m = x.shape[0]
    return pl.pallas_call(
        _norm_in_body,
        grid=(m // tm,),
        in_specs=[
            pl.BlockSpec((tm, D_MODEL), lambda i: (i, 0)),
            pl.BlockSpec((1, D_MODEL), lambda i: (0, 0)),
            pl.BlockSpec((LANES, D_MODEL), lambda i: (IN_AB // LANES, 0)),
        ],
        out_specs=[
            pl.BlockSpec((tm, D_MODEL), lambda i: (i, 0)),
            pl.BlockSpec((tm, LANES), lambda i: (i, 0)),
        ],
        out_shape=[jax.ShapeDtypeStruct((m, D_MODEL), BF16), jax.ShapeDtypeStruct((m, LANES), F32)],
        compiler_params=_cparams("parallel"),
        name="norm_in",
    )(x, gain, w_in_t)


def _inproj_body(hp_ref, hs_ref, w_ref, op_ref, os_ref, wb_ref):
    i = pl.program_id(1)

    @pl.when(i == 0)
    def _():
        wb_ref[...] = w_ref[...].astype(BF16)
        os_ref[...] = _dot_nt(hs_ref[...], wb_ref[...])

    @pl.when(i > 0)
    def _():
        op_ref[...] = _dot_nt(hp_ref[...], wb_ref[...])


def _inproj(h_prompt, h_sample, w_in_t, *, tm, tn):
    mp, ms = h_prompt.shape[0], h_sample.shape[0]
    n_p = mp // tm
    n_out, out_dtype = C_ALL, F32
    n_mid = (QKV_DIM + GDN_DIM) // tn
    n_pool = POOL_DIM // tn

    def row_start(j):
        return jnp.where(j < n_mid, IN_QKV + j * tn,
                         jnp.where(j < n_mid + n_pool, (j - n_mid) * tn, IN_GATES + (j - n_mid - n_pool) * tn))

    return pl.pallas_call(
        _inproj_body,
        grid=(n_out // tn, n_p + 1),
        in_specs=[
            pl.BlockSpec((tm, D_MODEL), lambda j, i: (jnp.maximum(i - 1, 0), 0)),
            pl.BlockSpec((ms, D_MODEL), lambda j, i: (0, 0)),
            pl.BlockSpec((pl.Element(tn), pl.Element(D_MODEL)),
                         lambda j, i: (pl.multiple_of(row_start(j), SUBLANES), 0)),
        ],
        out_specs=[pl.BlockSpec((tm, tn), lambda j, i: (jnp.maximum(i - 1, 0), j)),
                   pl.BlockSpec((ms, tn), lambda j, i: (0, j))],
        out_shape=[jax.ShapeDtypeStruct((mp, n_out), out_dtype), jax.ShapeDtypeStruct((ms, n_out), out_dtype)],
        scratch_shapes=[pltpu.VMEM((tn, D_MODEL), BF16)],
        compiler_params=_cparams("arbitrary", "arbitrary"),
        name="inproj",
    )(h_prompt, h_sample, w_in_t)


def _pool_group_out(mean, tok, pw_ref, scale_ref, gi):
    cols = slice(gi * POOL_GROUP_DIM, (gi + 1) * POOL_GROUP_DIM)
    y = _dot((mean - tok).astype(BF16), pw_ref[gi])
    return (y * scale_ref[:, cols]).astype(BF16)


def _pool_prompt_body(u_ref, pw_ref, scale_ref, o_ref, ext_ref, *, tt, start_pos):
    t = pl.program_id(1)
    halo = POOL_BUF + 1

    @pl.when(t == 0)
    def _():
        ext_ref[0:halo, :] = jnp.zeros((halo, POOL_DIM), F32)

    ext_ref[halo:halo + tt, :] = u_ref[...]
    pos = start_pos + t * tt + lax.broadcasted_iota(jnp.int32, (tt, 1), 0)
    for gi, w in enumerate(POOL_WINDOWS):
        cols = slice(gi * POOL_GROUP_DIM, (gi + 1) * POOL_GROUP_DIM)
        tok = ext_ref[halo:halo + tt, cols]
        tot = tok
        for i in range(1, w):
            tot = tot + ext_ref[pl.ds(halo - i, tt), cols]
        cnt = jnp.minimum(pos + 1, w).astype(F32)
        o_ref[:, cols] = _pool_group_out(tot / cnt, tok, pw_ref, scale_ref, gi)
    ext_ref[0:halo, :] = ext_ref[tt:tt + halo, :]


def _pool_prompt(proj, pool_w, pool_scale, *, batch, seq, tt):
    nt = seq // tt
    body = functools.partial(_pool_prompt_body, tt=tt, start_pos=0)
    return pl.pallas_call(
        body,
        grid=(batch, nt),
        in_specs=[
            pl.BlockSpec((tt, POOL_DIM), lambda b, t: (b * nt + t, C_POOL // POOL_DIM)),
            pl.BlockSpec((len(POOL_WINDOWS), POOL_GROUP_DIM, POOL_GROUP_DIM), lambda b, t: (0, 0, 0)),
            pl.BlockSpec((1, POOL_DIM), lambda b, t: (0, 0)),
        ],
        out_specs=pl.BlockSpec((tt, POOL_DIM), lambda b, t: (b * nt + t, 0)),
        out_shape=jax.ShapeDtypeStruct((batch * seq, POOL_DIM), BF16),
        scratch_shapes=[pltpu.VMEM((POOL_BUF + 1 + tt, POOL_DIM), F32)],
        compiler_params=_cparams("parallel", "arbitrary"),
        name="pool_prompt",
    )(proj, pool_w, pool_scale)


def _pool_sample_body(u_ref, st_ref, pw_ref, scale_ref, o_ref, *, start_pos):
    for gi, w in enumerate(POOL_WINDOWS):
        cols = slice(gi * POOL_GROUP_DIM, (gi + 1) * POOL_GROUP_DIM)
        tok = u_ref[:, cols]
        tot = tok
        for i in range(1, w):
            tot = tot + st_ref[POOL_BUF - i, :, cols]
        cnt = float(min(start_pos + 1, w))
        o_ref[:, cols] = _pool_group_out(tot / cnt, tok, pw_ref, scale_ref, gi)


def _pool_sample(proj, state_t, pool_w, pool_scale):
    nb = proj.shape[0]
    body = functools.partial(_pool_sample_body, start_pos=PAST_LEN)
    return pl.pallas_call(
        body,
        grid=(1,),
        in_specs=[
            pl.BlockSpec((nb, POOL_DIM), lambda i: (0, C_POOL // POOL_DIM)),
            pl.BlockSpec((POOL_BUF, nb, POOL_DIM), lambda i: (0, 0, 0)),
            pl.BlockSpec((len(POOL_WINDOWS), POOL_GROUP_DIM, POOL_GROUP_DIM), lambda i: (0, 0, 0)),
            pl.BlockSpec((1, POOL_DIM), lambda i: (0, 0)),
        ],
        out_specs=pl.BlockSpec((nb, POOL_DIM), lambda i: (0, 0)),
        out_shape=jax.ShapeDtypeStruct((nb, POOL_DIM), BF16),
        compiler_params=_cparams("arbitrary"),
        name="pool_sample",
    )(proj, state_t, pool_w, pool_scale)


def _split3(x):
    hi = x.astype(BF16)
    r1 = x - hi.astype(F32)
    mid = r1.astype(BF16)
    lo = (r1 - mid.astype(F32)).astype(BF16)
    return hi, mid, lo


def _softplus(x):
    return jnp.maximum(x, 0.0) + jnp.log1p(jnp.exp(-jnp.abs(x)))


def _unit_lower_inverse_minus_identity(mats):
    eye = jnp.where(lax.broadcasted_iota(jnp.int32, (CHUNK, CHUNK), 0)
                    == lax.broadcasted_iota(jnp.int32, (CHUNK, CHUNK), 1), 1.0, 0.0)
    levels = CHUNK.bit_length() - 1
    qbs = [(-a).astype(BF16) for a in mats]
    rs = [eye - a for a in mats]
    qs = [_dot(qb, qb) for qb in qbs]
    for j in range(1, levels):
        qbs = [q.astype(BF16) for q in qs]
        rbs = [r.astype(BF16) for r in rs]
        if j < levels - 1:
            ps = [_dot(qb, jnp.concatenate([qb, rb], axis=1)) for qb, rb in zip(qbs, rbs)]
            rs = [r + p[:, CHUNK:] for r, p in zip(rs, ps)]
            qs = [p[:, :CHUNK] for p in ps]
        else:
            rs = [r + _dot(qb, rb) for r, qb, rb in zip(rs, qbs, rbs)]
    return [r - eye for r in rs]


def _gdn_prompt_body(q_ref, k_ref, v_ref, z_ref, ab_ref, cwq_ref, cwk_ref, cwv_ref, alog_ref, dtb_ref, gn_ref,
                     o_ref, ssm_ref, ext_ref, zext_ref, s_ref, gct_ref, *, hg, tt):
    hgi = pl.program_id(1)
    t = pl.program_id(2)
    w = hg * HEAD_DIM
    halo = SUBLANES

    @pl.when(t == 0)
    def _():
        ext_ref[0:halo, :] = jnp.zeros((halo, 3 * w), F32)
        zext_ref[0:halo, :] = jnp.zeros((halo, 3 * w), F32)
        s_ref[...] = jnp.zeros_like(s_ref)

    ext_ref[halo:halo + tt, 0:w] = q_ref[...]
    ext_ref[halo:halo + tt, w:2 * w] = k_ref[...]
    ext_ref[halo:halo + tt, 2 * w:3 * w] = v_ref[...]

    def conv(part, cw_ref):
        cols = slice(part * w, (part + 1) * w)
        x0 = ext_ref[halo:halo + tt, cols]
        x1 = ext_ref[pl.ds(halo - 1, tt), cols]
        zext_ref[halo:halo + tt, cols] = x0 * cw_ref[1:2, :] + x1 * cw_ref[0:1, :]
        acc = x0 * cw_ref[3:4, :] + x1 * cw_ref[2:3, :] + zext_ref[pl.ds(halo - 2, tt), cols]
        return _silu(acc)

    assert CONV_WIDTH == 4
    qc = conv(0, cwq_ref)
    kc = conv(1, cwk_ref)
    vc = conv(2, cwv_ref)
    ext_ref[0:halo, :] = ext_ref[tt:tt + halo, :]
    zext_ref[0:halo, :] = zext_ref[tt:tt + halo, :]

    ab = ab_ref[...]
    g_all = -jnp.exp(alog_ref[...]) * _softplus(ab + dtb_ref[...])
    beta_all = jax.nn.sigmoid(ab)
    ri = lax.broadcasted_iota(jnp.int32, (tt, tt), 0)
    ci = lax.broadcasted_iota(jnp.int32, (tt, tt), 1)
    shift = CHUNK.bit_length() - 1
    tri = jnp.where(((ri >> shift) == (ci >> shift)) & (ci <= ri), 1.0, 0.0).astype(BF16)
    g_hi, g_mid, g_lo = _split3(g_all)
    gc_all = _dot(tri, g_hi) + _dot(tri, g_mid) + _dot(tri, g_lo)
    gct_ref[...] = gc_all.T

    lane = lax.broadcasted_iota(jnp.int32, (tt, LANES), 1)
    r128 = lax.broadcasted_iota(jnp.int32, (CHUNK, CHUNK), 0)
    c128 = lax.broadcasted_iota(jnp.int32, (CHUNK, CHUNK), 1)
    sub8 = lax.broadcasted_iota(jnp.int32, (SUBLANES, CHUNK), 0)
    n_chunks = tt // CHUNK
    chains = [(hh, c) for hh in range(hg) for c in range(n_chunks)]
    gcol_heads, bcol_heads = [], []
    for hh in range(hg):
        head = hgi * hg + hh
        gcol_heads.append(jnp.sum(jnp.where(lane == head, gc_all, 0.0), axis=-1, keepdims=True))
        bcol_heads.append(jnp.sum(jnp.where(lane == head + HEADS, beta_all, 0.0), axis=-1, keepdims=True))

    qn_l, kn_l, kb_l, rhs_l, eg_l, gcol_l, dec_l = [], [], [], [], [], [], []
    for hh, c in chains:
        head = hgi * hg + hh
        rows = slice(c * CHUNK, (c + 1) * CHUNK)
        lanes = slice(hh * HEAD_DIM, (hh + 1) * HEAD_DIM)
        qh, kh, vh = qc[rows, lanes], kc[rows, lanes], vc[rows, lanes]
        qn = qh * lax.rsqrt(jnp.sum(qh * qh, axis=-1, keepdims=True) + EPS) * (HEAD_DIM ** -0.5)
        kn = kh * lax.rsqrt(jnp.sum(kh * kh, axis=-1, keepdims=True) + EPS)
        gcol = gcol_heads[hh][rows]
        bcol = bcol_heads[hh][rows]
        gblk = gct_ref[pl.ds(pl.multiple_of((head // SUBLANES) * SUBLANES, SUBLANES), SUBLANES), rows]
        grow = jnp.sum(jnp.where(sub8 == (head & (SUBLANES - 1)), gblk, 0.0), axis=0, keepdims=True)
        eg = jnp.exp(gcol)
        kb = kn * bcol
        qn_l.append(qn)
        kn_l.append(kn)
        kb_l.append(kb)
        eg_l.append(eg)
        gcol_l.append(gcol)
        dec_l.append(jnp.exp(jnp.minimum(gcol - grow, 0.0)))
        rhs_l.append(jnp.concatenate([vh * bcol, kb * eg], axis=1))

    a_l, qk_l = [], []
    for qn, kn, kb, decay in zip(qn_l, kn_l, kb_l, dec_l):
        knb = kn.astype(BF16)
        a_l.append(_dot_nt(kb.astype(BF16), knb) * jnp.where(c128 < r128, decay, 0.0))
        qk_l.append(_dot_nt(qn.astype(BF16), knb) * jnp.where(c128 <= r128, decay, 0.0))
    tinv_l = [x.astype(BF16) for x in _unit_lower_inverse_minus_identity(a_l)]
    sol_l = [rhs + _dot(tinv, rhs.astype(BF16)) for tinv, rhs in zip(tinv_l, rhs_l)]
    res_l = []
    for a, sol, rhs in zip(a_l, sol_l, rhs_l):
        a_sol = _dot(a.astype(BF16), sol.astype(BF16))
        res_l.append((rhs - sol) - a_sol)
    sol_l = [(sol + res + _dot(tinv, res.astype(BF16))).astype(BF16) for sol, res, tinv in zip(sol_l, res_l, tinv_l)]
    lhs_l, o_add_l, s_add_l, glast_l = [], [], [], []
    for qn, kn, eg, gcol, qk, sol_b in zip(qn_l, kn_l, eg_l, gcol_l, qk_l, sol_l):
        glast = gcol[CHUNK - 1:CHUNK, :]
        ktail = kn * jnp.exp(glast - gcol)
        kt_sol = _dot(ktail.T.astype(BF16), sol_b)
        qk_sol = _dot(qk.astype(BF16), sol_b)
        lhs_l.append(jnp.concatenate([qn * eg - qk_sol[:, HEAD_DIM:], kt_sol[:, HEAD_DIM:]], axis=0).astype(BF16))
        o_add_l.append(qk_sol[:, :HEAD_DIM])
        s_add_l.append(kt_sol[:, :HEAD_DIM])
        glast_l.append(jnp.exp(glast))
    states = [s_ref[hh] for hh in range(hg)]
    for c in range(n_chunks):
        for hh in range(hg):
            i = hh * n_chunks + c
            rows = slice(c * CHUNK, (c + 1) * CHUNK)
            lanes = slice(hh * HEAD_DIM, (hh + 1) * HEAD_DIM)
            on_s = _dot(lhs_l[i], states[hh].astype(BF16))
            o = on_s[:CHUNK] + o_add_l[i]
            states[hh] = states[hh] * glast_l[i] - on_s[CHUNK:] + s_add_l[i]
            o_ref[rows, lanes] = (_rms(o, gn_ref[...]) * _silu(z_ref[rows, lanes])).astype(BF16)
    for hh in range(hg):
        s_ref[hh] = states[hh]

    @pl.when(t == pl.num_programs(2) - 1)
    def _():
        ssm_ref[0, 0] = s_ref[...]


def _gdn_prompt(proj, ab, conv_w, alog_pad, dtb_pad, gdn_norm, *, batch, seq, hg, tt):
    nt = seq // tt
    w = hg * HEAD_DIM
    body = functools.partial(_gdn_prompt_body, hg=hg, tt=tt)

    def col_spec(col0):
        return pl.BlockSpec((tt, w), lambda b, h, t: (b * nt + t, col0 // w + h))

    def cw_spec(part):
        return pl.BlockSpec((CONV_WIDTH, w), lambda b, h, t: (0, part * GDN_DIM // w + h))

    small = pl.BlockSpec((1, LANES), lambda b, h, t: (0, 0))
    return pl.pallas_call(
        body,
        grid=(batch, HEADS // hg, nt),
        in_specs=[col_spec(C_Q), col_spec(C_K), col_spec(C_V), col_spec(C_Z),
                  pl.BlockSpec((tt, LANES), lambda b, h, t: (b * nt + t, 0)),
                  cw_spec(0), cw_spec(1), cw_spec(2), small, small, small],
        out_specs=[
            pl.BlockSpec((tt, w), lambda b, h, t: (b * nt + t, h)),
            pl.BlockSpec((1, 1, hg, HEAD_DIM, HEAD_DIM), lambda b, h, t: (0, b, h, 0, 0)),
        ],
        out_shape=[
            jax.ShapeDtypeStruct((batch * seq, GDN_DIM), BF16),
            jax.ShapeDtypeStruct((1, batch, HEADS, HEAD_DIM, HEAD_DIM), F32),
        ],
        scratch_shapes=[
            pltpu.VMEM((SUBLANES + tt, 3 * w), F32),
            pltpu.VMEM((SUBLANES + tt, 3 * w), F32),
            pltpu.VMEM((hg, HEAD_DIM, HEAD_DIM), F32),
            pltpu.VMEM((LANES, tt), F32),
        ],
        compiler_params=_cparams("parallel", "parallel", "arbitrary"),
        name="gdn_prompt",
    )(proj, proj, proj, proj, ab, conv_w, conv_w, conv_w, alog_pad, dtb_pad, gdn_norm)


def _gdn_sample_prep_body(q_ref, k_ref, v_ref, ab_ref, st_ref, cw_ref, alog_ref, dtb_ref,
                          qn_ref, kn_ref, vc_ref, eg_ref, beta_ref, qk_ref):
    def conv(part, u_ref):
        cols = slice(part * GDN_DIM, (part + 1) * GDN_DIM)
        acc = u_ref[...] * cw_ref[CONV_BUF:CONV_WIDTH, cols]
        for i in range(CONV_BUF):
            acc = acc + st_ref[i, :, cols] * cw_ref[i:i + 1, cols]
        return _silu(acc)

    qc = conv(0, q_ref)
    kc = conv(1, k_ref)
    vc_ref[...] = conv(2, v_ref)
    ab = ab_ref[...]
    g_all = -jnp.exp(alog_ref[...]) * _softplus(ab + dtb_ref[...])
    eg_all = jnp.exp(g_all)
    beta_all = jax.nn.sigmoid(ab)
    for h in range(HEADS):
        lanes = slice(h * HEAD_DIM, (h + 1) * HEAD_DIM)
        qh, kh = qc[:, lanes], kc[:, lanes]
        qn = qh * lax.rsqrt(jnp.sum(qh * qh, axis=-1, keepdims=True) + EPS) * (HEAD_DIM ** -0.5)
        kn = kh * lax.rsqrt(jnp.sum(kh * kh, axis=-1, keepdims=True) + EPS)
        qn_ref[:, lanes] = qn
        kn_ref[:, lanes] = kn
        shape = qn.shape
        qk_ref[:, lanes] = jnp.broadcast_to(jnp.sum(qn * kn, axis=-1, keepdims=True), shape)
        eg_ref[:, lanes] = jnp.broadcast_to(eg_all[:, h:h + 1], shape)
        beta_ref[:, lanes] = jnp.broadcast_to(beta_all[:, HEADS + h:HEADS + h + 1], shape)


def _gdn_sample_prep(proj, ab, conv_state_t, conv_w, alog_pad, dtb_pad):
    nb = proj.shape[0]

    def col_spec(col0):
        return pl.BlockSpec((nb, GDN_DIM), lambda i: (0, col0 // GDN_DIM))

    small = pl.BlockSpec((1, LANES), lambda i: (0, 0))
    full = pl.BlockSpec((nb, GDN_DIM), lambda i: (0, 0))
    return pl.pallas_call(
        _gdn_sample_prep_body,
        grid=(1,),
        in_specs=[col_spec(C_Q), col_spec(C_K), col_spec(C_V),
                  pl.BlockSpec((nb, LANES), lambda i: (0, 0)),
                  pl.BlockSpec((CONV_BUF, nb, QKV_DIM), lambda i: (0, 0, 0)),
                  pl.BlockSpec((CONV_WIDTH, QKV_DIM), lambda i: (0, 0)),
                  small, small],
        out_specs=[full] * 6,
        out_shape=[jax.ShapeDtypeStruct((nb, GDN_DIM), F32)] * 6,
        compiler_params=_cparams("arbitrary"),
        name="gdn_sample_prep",
    )(proj, proj, proj, ab, conv_state_t, conv_w, alog_pad, dtb_pad)


def _gdn_sample_step_body(qn_ref, kn_ref, vc_ref, eg_ref, beta_ref, qk_ref, z_ref, gn_ref, s_ref,
                          o_ref, snew_ref, *, bb):
    pad = jnp.zeros((LANES - HEADS, HEAD_DIM), F32)
    for b in range(bb):
        k16, q16 = kn_ref[b], qn_ref[b]
        cols = jnp.concatenate([k16, pad], axis=0).T
        kq = jnp.concatenate([k16, q16], axis=0).astype(BF16)
        outs = []
        for h in range(HEADS):
            kcol = cols[:, h:h + 1]
            s = s_ref[0, b, h]
            kq_s = _dot(kq, s.astype(BF16))
            k_s = kq_s[h:h + 1]
            q_s = kq_s[HEADS + h:HEADS + h + 1]
            eg = eg_ref[b, h:h + 1, :]
            v_new = beta_ref[b, h:h + 1, :] * (vc_ref[b, h:h + 1, :] - eg * k_s)
            snew_ref[0, b, h] = eg * s + kcol * v_new
            outs.append(eg * q_s + qk_ref[b, h:h + 1, :] * v_new)
        o = jnp.concatenate(outs, axis=0)
        o_ref[b] = (_rms(o, gn_ref[...]) * _silu(z_ref[b])).astype(BF16)


def _gdn_sample_step(qn, kn, vc, eg, beta, qk, z, gdn_norm, state_ssm, *, bb):
    nb = qn.shape[0]
    body = functools.partial(_gdn_sample_step_body, bb=bb)
    vec = pl.BlockSpec((bb, HEADS, HEAD_DIM), lambda i: (i, 0, 0))
    st = pl.BlockSpec((1, bb, HEADS, HEAD_DIM, HEAD_DIM), lambda i: (0, i, 0, 0, 0))
    return pl.pallas_call(
        body,
        grid=(nb // bb,),
        in_specs=[vec] * 7 + [pl.BlockSpec((1, LANES), lambda i: (0, 0)), st],
        out_specs=[vec, st],
        out_shape=[jax.ShapeDtypeStruct((nb, HEADS, HEAD_DIM), BF16),
                   jax.ShapeDtypeStruct(state_ssm.shape, F32)],
        compiler_params=_cparams("parallel"),
        name="gdn_sample_step",
    )(qn, kn, vc, eg, beta, qk, z, gdn_norm, state_ssm)


def _merge_body(pool_ref, gdn_ref, wpu_ref, wgu_ref, gp_ref, gg_ref, o_ref):
    o_ref[...] = (jax.nn.sigmoid(gp_ref[...]) * _dot(pool_ref[...], wpu_ref[...])
                  + jax.nn.sigmoid(gg_ref[...]) * _dot(gdn_ref[...], wgu_ref[...])).astype(BF16)


def _merge(proj, pool_out, gdn_out, w_pool_up, w_gdn_up, *, tm, tn):
    m = pool_out.shape[0]
    nj = D_MODEL // tn
    return pl.pallas_call(
        _merge_body,
        grid=(m // tm, nj),
        in_specs=[
            pl.BlockSpec((tm, POOL_DIM), lambda i, j: (i, 0)),
            pl.BlockSpec((tm, GDN_DIM), lambda i, j: (i, 0)),
            pl.BlockSpec((POOL_DIM, tn), lambda i, j: (0, j)),
            pl.BlockSpec((GDN_DIM, tn), lambda i, j: (0, j)),
            pl.BlockSpec((tm, tn), lambda i, j: (i, C_GP // tn + j)),
            pl.BlockSpec((tm, tn), lambda i, j: (i, C_GG // tn + j)),
        ],
        out_specs=pl.BlockSpec((tm, tn), lambda i, j: (i, j)),
        out_shape=jax.ShapeDtypeStruct((m, D_MODEL), BF16),
        compiler_params=_cparams("parallel", "parallel"),
        name="merge",
    )(pool_out, gdn_out, w_pool_up, w_gdn_up, proj, proj)


def _outproj_body(a_ref, w_ref, x_ref, o_ref):
    o_ref[...] = x_ref[...] + _dot(a_ref[...], w_ref[...])


def _outproj(merged, x, w_o, *, tm, tn):
    m = x.shape[0]
    return pl.pallas_call(
        _outproj_body,
        grid=(m // tm, D_MODEL // tn),
        in_specs=[
            pl.BlockSpec((tm, D_MODEL), lambda i, j: (i, 0)),
            pl.BlockSpec((D_MODEL, tn), lambda i, j: (0, j)),
            pl.BlockSpec((tm, tn), lambda i, j: (i, j)),
        ],
        out_specs=pl.BlockSpec((tm, tn), lambda i, j: (i, j)),
        out_shape=jax.ShapeDtypeStruct((m, D_MODEL), F32),
        compiler_params=_cparams("parallel", "parallel"),
        name="outproj",
    )(merged, w_o, x)


def _ffn_body(x_ref, gain_ref, wg_ref, wu_ref, wd_ref, o_ref, h_ref):
    f = pl.program_id(1)

    @pl.when(f == 0)
    def _():
        x = x_ref[...]
        h_ref[...] = _rms(x, gain_ref[...]).astype(BF16)
        o_ref[...] = x

    h = h_ref[...]
    act = (_silu(_dot(h, wg_ref[...].astype(BF16))) * _dot(h, wu_ref[...].astype(BF16))).astype(BF16)
    o_ref[...] += _dot(act, wd_ref[...].astype(BF16))


def _ffn(x, gain, w_gate_up, w_down, *, tm, tf):
    m = x.shape[0]
    nf = D_FF // tf
    return pl.pallas_call(
        _ffn_body,
        grid=(m // tm, nf),
        in_specs=[
            pl.BlockSpec((tm, D_MODEL), lambda i, f: (i, 0)),
            pl.BlockSpec((1, D_MODEL), lambda i, f: (0, 0)),
            pl.BlockSpec((D_MODEL, tf), lambda i, f: (0, f)),
            pl.BlockSpec((D_MODEL, tf), lambda i, f: (0, nf + f)),
            pl.BlockSpec((tf, D_MODEL), lambda i, f: (f, 0)),
        ],
        out_specs=pl.BlockSpec((tm, D_MODEL), lambda i, f: (i, 0)),
        out_shape=jax.ShapeDtypeStruct((m, D_MODEL), F32),
        scratch_shapes=[pltpu.VMEM((tm, D_MODEL), BF16)],
        compiler_params=_cparams("parallel", "arbitrary"),
        name="ffn",
    )(x, gain, w_gate_up, w_gate_up, w_down)


def _final_body(x_ref, p_ref, wple_ref, wpg_ref, gain_ref, o_ref):
    x = x_ref[...]
    gate = jax.nn.sigmoid(_dot(x.astype(BF16), wpg_ref[...]))
    emb = _dot(p_ref[...].astype(BF16), wple_ref[...])
    o_ref[...] = _rms(x + emb * gate, gain_ref[...])


def _final(x, p, w_ple, w_ple_gate, gain, *, tm):
    m = x.shape[0]
    return pl.pallas_call(
        _final_body,
        grid=(m // tm,),
        in_specs=[
            pl.BlockSpec((tm, D_MODEL), lambda i: (i, 0)),
            pl.BlockSpec((tm, PLE_DIM), lambda i: (i, 0)),
            pl.BlockSpec((PLE_DIM, D_MODEL), lambda i: (0, 0), pipeline_mode=pl.Buffered(1)),
            pl.BlockSpec((D_MODEL, D_MODEL), lambda i: (0, 0), pipeline_mode=pl.Buffered(1)),
            pl.BlockSpec((1, D_MODEL), lambda i: (0, 0)),
        ],
        out_specs=pl.BlockSpec((tm, D_MODEL), lambda i: (i, 0)),
        out_shape=jax.ShapeDtypeStruct((m, D_MODEL), F32),
        compiler_params=_cparams("parallel"),
        name="final",
    )(x, p, w_ple, w_ple_gate, gain)


def _tile_rows(m, want):
    return want if m % want == 0 else m


def _dense_tail(x, p, proj, pool_out, gdn_out, wts):
    m = x.shape[0]
    tm = _tile_rows(m, TM_DENSE)
    merged = _merge(proj, pool_out, gdn_out, wts["w_pool_up"], wts["w_gdn_up"], tm=tm, tn=TN_DENSE)
    x = _outproj(merged, x, wts["w_o"], tm=tm, tn=TN_DENSE)
    x = _ffn(x, wts["norm_ffn"], wts["w_gate_up"], wts["w_down"], tm=tm,
             tf=TF_FFN if tm == TM_DENSE else TF_FFN_SMALL)
    return _final(x, p, wts["w_ple"], wts["w_ple_gate"], wts["norm_final"], tm=tm)


def kernel(x_prompt, x_sample, p_prompt, p_sample, state_pool, state_conv, state_ssm, norm_mix, w_in, pool_w,
           pool_scale, conv_w, a_log, dt_bias, gdn_norm, w_pool_up, w_gdn_up, w_o, norm_ffn, w_gate_up, w_down,
           w_ple, w_ple_gate, norm_final):
    batch, seq, _ = x_prompt.shape
    nb = x_sample.shape[0]
    w_in_t = jnp.swapaxes(w_in[0], 0, 1)
    wts = dict(
        w_pool_up=w_pool_up[0].astype(BF16), w_gdn_up=w_gdn_up[0].astype(BF16), w_o=w_o[0].astype(BF16),
        norm_ffn=norm_ffn, w_gate_up=w_gate_up[0], w_down=w_down[0],
        w_ple=w_ple[0].astype(BF16), w_ple_gate=w_ple_gate[0].astype(BF16), norm_final=norm_final[None, :],
    )
    pool_w_b = pool_w[0].astype(BF16)
    lane_pad = ((0, 0), (0, LANES - HEADS))
    alog_pad = jnp.pad(a_log, lane_pad)
    dtb_pad = jnp.pad(dt_bias, lane_pad)
    gn = gdn_norm

    mp = batch * seq
    xp = x_prompt.reshape(mp, D_MODEL)
    xs = x_sample.reshape(nb, D_MODEL)
    h_p, ab_p = _norm_in(xp, norm_mix, w_in_t, tm=_tile_rows(mp, TM_NORM))
    h_s, ab_s = _norm_in(xs, norm_mix, w_in_t, tm=nb)
    proj_p, proj_s = _inproj(h_p, h_s, w_in_t, tm=_tile_rows(mp, TM_DENSE), tn=TN_INPROJ)

    pool_out_p = _pool_prompt(proj_p, pool_w_b, pool_scale, batch=batch, seq=seq, tt=_tile_rows(seq, TT_POOL))
    gdn_out_p, ssm_p = _gdn_prompt(proj_p, ab_p, conv_w[0], alog_pad, dtb_pad, gn, batch=batch, seq=seq,
                                   hg=GDN_HEADS_PER_STEP, tt=_tile_rows(seq, TT_GDN))
    y_p = _dense_tail(xp, p_prompt[0].reshape(mp, PLE_DIM), proj_p, pool_out_p, gdn_out_p, wts)
    proj_p3 = proj_p.reshape(batch, seq, C_ALL)
    pool_p = proj_p3[:, seq - POOL_BUF:, C_POOL:C_END][None]
    conv_p = proj_p3[:, seq - CONV_BUF:, C_Q:C_Z][None]

    pool_out_s = _pool_sample(proj_s, jnp.swapaxes(state_pool[0], 0, 1), pool_w_b, pool_scale)
    prep = _gdn_sample_prep(proj_s, ab_s, jnp.swapaxes(state_conv[0], 0, 1), conv_w[0], alog_pad, dtb_pad)
    to_heads = lambda a: a.reshape(nb, HEADS, HEAD_DIM)
    gdn_out_s, ssm_s = _gdn_sample_step(*[to_heads(a) for a in prep], to_heads(proj_s[:, C_Z:C_POOL]), gn,
                                        state_ssm, bb=_tile_rows(nb, BB_SAMPLE))
    y_s = _dense_tail(xs, p_sample[0].reshape(nb, PLE_DIM), proj_s, pool_out_s, gdn_out_s.reshape(nb, GDN_DIM), wts)
    pool_s = jnp.concatenate([state_pool[0][:, 1:], proj_s[:, None, C_POOL:C_END]], axis=1)[None]
    conv_s = jnp.concatenate([state_conv[0][:, 1:], proj_s[:, None, C_Q:C_Z]], axis=1)[None]

    return (y_p.reshape(batch, seq, D_MODEL), y_s.reshape(nb, 1, D_MODEL), pool_p, conv_p, ssm_p,
            pool_s, conv_s, ssm_s)
```

```python
import functools

import jax
import jax.numpy as jnp
from jax import lax
from jax.experimental import pallas as pl
from jax.experimental.pallas import tpu as pltpu

F32 = jnp.float32
BF16 = jnp.bfloat16

D_MODEL = 2048
PAST_LEN = 16384
POOL_WINDOWS = (2, 4, 8, 16)
POOL_GROUP_DIM = D_MODEL // 8
POOL_DIM = len(POOL_WINDOWS) * POOL_GROUP_DIM
POOL_BUF = max(POOL_WINDOWS) - 1
HEAD_DIM = 128
HEADS = D_MODEL // HEAD_DIM
GDN_DIM = HEADS * HEAD_DIM
QKV_DIM = 3 * GDN_DIM
CONV_WIDTH = 4
CONV_BUF = CONV_WIDTH - 1
D_FF = -(-8 * D_MODEL // (3 * 256)) * 256
PLE_DIM = 256
EPS = 1e-6

IN_QKV = POOL_DIM
IN_Z = IN_QKV + QKV_DIM
IN_AB = IN_Z + GDN_DIM
IN_GATES = IN_AB + 2 * HEADS
C_Q = 0
C_K = C_Q + GDN_DIM
C_V = C_K + GDN_DIM
C_Z = C_V + GDN_DIM
C_POOL = C_Z + GDN_DIM
C_END = C_POOL + POOL_DIM
C_GP = C_END
C_GG = C_GP + D_MODEL
C_ALL = C_GG + D_MODEL

LANES = 128
SUBLANES = 8
VMEM_LIMIT_BYTES = 56 * 1024 * 1024
CHUNK = 128

TM_DENSE = 1024
TN_INPROJ = 1024
TN_DENSE = 1024
TF_FFN = 256
TF_FFN_SMALL = 512
TM_NORM = 1024
TT_POOL = 1024
GDN_HEADS_PER_STEP = 8
TT_GDN = 256
BB_SAMPLE = 8


def _cparams(*sem):
    return pltpu.CompilerParams(dimension_semantics=sem, vmem_limit_bytes=VMEM_LIMIT_BYTES)


def _rms(x, gain):
    return x * lax.rsqrt(jnp.mean(x * x, axis=-1, keepdims=True) + EPS) * gain


def _dot(a, b):
    return jnp.dot(a, b, preferred_element_type=F32)


def _dot_nt(a, b):
    return lax.dot_general(a, b, (((1,), (1,)), ((), ())), preferred_element_type=F32)


def _silu(x):
    return x * jax.nn.sigmoid(x)


def _norm_in_body(x_ref, gain_ref, wab_ref, h_ref, ab_ref):
    h = _rms(x_ref[...], gain_ref[...]).astype(BF16)
    h_ref[...] = h
    ab_ref[...] = _dot_nt(h, wab_ref[...].astype(BF16))


def _norm_in(x, gain, w_in_t, *, tm):
    m = x.shape[0]
    return pl.pallas_call(
        _norm_in_body,
        grid=(m // tm,),
        in_specs=[
            pl.BlockSpec((tm, D_MODEL), lambda i: (i, 0)),
            pl.BlockSpec((1, D_MODEL), lambda i: (0, 0)),
            pl.BlockSpec((LANES, D_MODEL), lambda i: (IN_AB // LANES, 0)),
        ],
        out_specs=[
            pl.BlockSpec((tm, D_MODEL), lambda i: (i, 0)),
            pl.BlockSpec((tm, LANES), lambda i: (i, 0)),
        ],
        out_shape=[jax.ShapeDtypeStruct((m, D_MODEL), BF16), jax.ShapeDtypeStruct((m, LANES), F32)],
        compiler_params=_cparams("parallel"),
        name="norm_in",
    )(x, gain, w_in_t)


def _inproj_body(hp_ref, hs_ref, w_ref, op_ref, os_ref, wb_ref):
    i = pl.program_id(1)

    @pl.when(i == 0)
    def _():
        wb_ref[...] = w_ref[...].astype(BF16)
        os_ref[...] = _dot_nt(hs_ref[...], wb_ref[...])

    @pl.when(i > 0)
    def _():
        op_ref[...] = _dot_nt(hp_ref[...], wb_ref[...])


def _inproj(h_prompt, h_sample, w_in_t, *, tm, tn):
    mp, ms = h_prompt.shape[0], h_sample.shape[0]
    n_p = mp // tm
    n_out, out_dtype = C_ALL, F32
    n_mid = (QKV_DIM + GDN_DIM) // tn
    n_pool = POOL_DIM // tn

    def row_start(j):
        return jnp.where(j < n_mid, IN_QKV + j * tn,
                         jnp.where(j < n_mid + n_pool, (j - n_mid) * tn, IN_GATES + (j - n_mid - n_pool) * tn))

    return pl.pallas_call(
        _inproj_body,
        grid=(n_out // tn, n_p + 1),
        in_specs=[
            pl.BlockSpec((tm, D_MODEL), lambda j, i: (jnp.maximum(i - 1, 0), 0)),
            pl.BlockSpec((ms, D_MODEL), lambda j, i: (0, 0)),
            pl.BlockSpec((pl.Element(tn), pl.Element(D_MODEL)),
                         lambda j, i: (pl.multiple_of(row_start(j), SUBLANES), 0)),
        ],
        out_specs=[pl.BlockSpec((tm, tn), lambda j, i: (jnp.maximum(i - 1, 0), j)),
                   pl.BlockSpec((ms, tn), lambda j, i: (0, j))],
        out_shape=[jax.ShapeDtypeStruct((mp, n_out), out_dtype), jax.ShapeDtypeStruct((ms, n_out), out_dtype)],
        scratch_shapes=[pltpu.VMEM((tn, D_MODEL), BF16)],
        compiler_params=_cparams("arbitrary", "arbitrary"),
        name="inproj",
    )(h_prompt, h_sample, w_in_t)


def _pool_group_out(mean, tok, pw_ref, scale_ref, gi):
    cols = slice(gi * POOL_GROUP_DIM, (gi + 1) * POOL_GROUP_DIM)
    y = _dot((mean - tok).astype(BF16), pw_ref[gi])
    return (y * scale_ref[:, cols]).astype(BF16)


def _pool_prompt_body(u_ref, pw_ref, scale_ref, o_ref, ext_ref, *, tt, start_pos):
    t = pl.program_id(1)
    halo = POOL_BUF + 1

    @pl.when(t == 0)
    def _():
        ext_ref[0:halo, :] = jnp.zeros((halo, POOL_DIM), F32)

    ext_ref[halo:halo + tt, :] = u_ref[...]
    pos = start_pos + t * tt + lax.broadcasted_iota(jnp.int32, (tt, 1), 0)
    for gi, w in enumerate(POOL_WINDOWS):
        cols = slice(gi * POOL_GROUP_DIM, (gi + 1) * POOL_GROUP_DIM)
        tok = ext_ref[halo:halo + tt, cols]
        tot = tok
        for i in range(1, w):
            tot = tot + ext_ref[pl.ds(halo - i, tt), cols]
        cnt = jnp.minimum(pos + 1, w).astype(F32)
        o_ref[:, cols] = _pool_group_out(tot / cnt, tok, pw_ref, scale_ref, gi)
    ext_ref[0:halo, :] = ext_ref[tt:tt + halo, :]


def _pool_prompt(proj, pool_w, pool_scale, *, batch, seq, tt):
    nt = seq // tt
    body = functools.partial(_pool_prompt_body, tt=tt, start_pos=0)
    return pl.pallas_call(
        body,
        grid=(batch, nt),
        in_specs=[
            pl.BlockSpec((tt, POOL_DIM), lambda b, t: (b * nt + t, C_POOL // POOL_DIM)),
            pl.BlockSpec((len(POOL_WINDOWS), POOL_GROUP_DIM, POOL_GROUP_DIM), lambda b, t: (0, 0, 0)),
            pl.BlockSpec((1, POOL_DIM), lambda b, t: (0, 0)),
        ],
        out_specs=pl.BlockSpec((tt, POOL_DIM), lambda b, t: (b * nt + t, 0)),
        out_shape=jax.ShapeDtypeStruct((batch * seq, POOL_DIM), BF16),
        scratch_shapes=[pltpu.VMEM((POOL_BUF + 1 + tt, POOL_DIM), F32)],
        compiler_params=_cparams("parallel", "arbitrary"),
        name="pool_prompt",
    )(proj, pool_w, pool_scale)


def _pool_sample_body(u_ref, st_ref, pw_ref, scale_ref, o_ref, *, start_pos):
    for gi, w in enumerate(POOL_WINDOWS):
        cols = slice(gi * POOL_GROUP_DIM, (gi + 1) * POOL_GROUP_DIM)
        tok = u_ref[:, cols]
        tot = tok
        for i in range(1, w):
            tot = tot + st_ref[POOL_BUF - i, :, cols]
        cnt = float(min(start_pos + 1, w))
        o_ref[:, cols] = _pool_group_out(tot / cnt, tok, pw_ref, scale_ref, gi)


def _pool_sample(proj, state_t, pool_w, pool_scale):
    nb = proj.shape[0]
    body = functools.partial(_pool_sample_body, start_pos=PAST_LEN)
    return pl.pallas_call(
        body,
        grid=(1,),
        in_specs=[
            pl.BlockSpec((nb, POOL_DIM), lambda i: (0, C_POOL // POOL_DIM)),
            pl.BlockSpec((POOL_BUF, nb, POOL_DIM), lambda i: (0, 0, 0)),
            pl.BlockSpec((len(POOL_WINDOWS), POOL_GROUP_DIM, POOL_GROUP_DIM), lambda i: (0, 0, 0)),
            pl.BlockSpec((1, POOL_DIM), lambda i: (0, 0)),
        ],
        out_specs=pl.BlockSpec((nb, POOL_DIM), lambda i: (0, 0)),
        out_shape=jax.ShapeDtypeStruct((nb, POOL_DIM), BF16),
        compiler_params=_cparams("arbitrary"),
        name="pool_sample",
    )(proj, state_t, pool_w, pool_scale)


def _split3(x):
    hi = x.astype(BF16)
    r1 = x - hi.astype(F32)
    mid = r1.astype(BF16)
    lo = (r1 - mid.astype(F32)).astype(BF16)
    return hi, mid, lo


def _softplus(x):
    return jnp.maximum(x, 0.0) + jnp.log1p(jnp.exp(-jnp.abs(x)))


def _unit_lower_inverse_minus_identity(mats):
    eye = jnp.where(lax.broadcasted_iota(jnp.int32, (CHUNK, CHUNK), 0)
                    == lax.broadcasted_iota(jnp.int32, (CHUNK, CHUNK), 1), 1.0, 0.0)
    levels = CHUNK.bit_length() - 1
    qbs = [(-a).astype(BF16) for a in mats]
    rs = [eye - a for a in mats]
    qs = [_dot(qb, qb) for qb in qbs]
    for j in range(1, levels):
        qbs = [q.astype(BF16) for q in qs]
        rbs = [r.astype(BF16) for r in rs]
        if j < levels - 1:
            ps = [_dot(qb, jnp.concatenate([qb, rb], axis=1)) for qb, rb in zip(qbs, rbs)]
            rs = [r + p[:, CHUNK:] for r, p in zip(rs, ps)]
            qs = [p[:, :CHUNK] for p in ps]
        else:
            rs = [r + _dot(qb, rb) for r, qb, rb in zip(rs, qbs, rbs)]
    return [r - eye for r in rs]


def _gdn_prompt_body(q_ref, k_ref, v_ref, z_ref, ab_ref, cwq_ref, cwk_ref, cwv_ref, alog_ref, dtb_ref, gn_ref,
                     o_ref, ssm_ref, ext_ref, s_ref, gct_ref, *, hg, tt):
    hgi = pl.program_id(1)
    t = pl.program_id(2)
    w = hg * HEAD_DIM
    halo = SUBLANES

    @pl.when(t == 0)
    def _():
        ext_ref[0:halo, :] = jnp.zeros((halo, 3 * w), F32)
        s_ref[...] = jnp.zeros_like(s_ref)

    ext_ref[halo:halo + tt, 0:w] = q_ref[...]
    ext_ref[halo:halo + tt, w:2 * w] = k_ref[...]
    ext_ref[halo:halo + tt, 2 * w:3 * w] = v_ref[...]

    def conv(part, cw_ref):
        cols = slice(part * w, (part + 1) * w)
        acc = ext_ref[halo:halo + tt, cols] * cw_ref[CONV_BUF:CONV_WIDTH, :]
        for i in range(CONV_BUF):
            acc = acc + ext_ref[pl.ds(halo - CONV_BUF + i, tt), cols] * cw_ref[i:i + 1, :]
        return _silu(acc)

    qc = conv(0, cwq_ref)
    kc = conv(1, cwk_ref)
    vc = conv(2, cwv_ref)
    ext_ref[0:halo, :] = ext_ref[tt:tt + halo, :]

    ab = ab_ref[...]
    g_all = -jnp.exp(alog_ref[...]) * _softplus(ab + dtb_ref[...])
    beta_all = jax.nn.sigmoid(ab)
    ri = lax.broadcasted_iota(jnp.int32, (tt, tt), 0)
    ci = lax.broadcasted_iota(jnp.int32, (tt, tt), 1)
    shift = CHUNK.bit_length() - 1
    tri = jnp.where(((ri >> shift) == (ci >> shift)) & (ci <= ri), 1.0, 0.0).astype(BF16)
    g_hi, g_mid, g_lo = _split3(g_all)
    gc_all = _dot(tri, g_hi) + _dot(tri, g_mid) + _dot(tri, g_lo)
    gct_ref[...] = gc_all.T

    lane = lax.broadcasted_iota(jnp.int32, (tt, LANES), 1)
    r128 = lax.broadcasted_iota(jnp.int32, (CHUNK, CHUNK), 0)
    c128 = lax.broadcasted_iota(jnp.int32, (CHUNK, CHUNK), 1)
    sub8 = lax.broadcasted_iota(jnp.int32, (SUBLANES, CHUNK), 0)
    n_chunks = tt // CHUNK
    chains = [(hh, c) for hh in range(hg) for c in range(n_chunks)]
    gcol_heads, bcol_heads = [], []
    for hh in range(hg):
        head = hgi * hg + hh
        gcol_heads.append(jnp.sum(jnp.where(lane == head, gc_all, 0.0), axis=-1, keepdims=True))
        bcol_heads.append(jnp.sum(jnp.where(lane == head + HEADS, beta_all, 0.0), axis=-1, keepdims=True))

    qn_l, kn_l, kb_l, rhs_l, eg_l, gcol_l, dec_l = [], [], [], [], [], [], []
    for hh, c in chains:
        head = hgi * hg + hh
        rows = slice(c * CHUNK, (c + 1) * CHUNK)
        lanes = slice(hh * HEAD_DIM, (hh + 1) * HEAD_DIM)
        qh, kh, vh = qc[rows, lanes], kc[rows, lanes], vc[rows, lanes]
        qn = qh * lax.rsqrt(jnp.sum(qh * qh, axis=-1, keepdims=True) + EPS) * (HEAD_DIM ** -0.5)
        kn = kh * lax.rsqrt(jnp.sum(kh * kh, axis=-1, keepdims=True) + EPS)
        gcol = gcol_heads[hh][rows]
        bcol = bcol_heads[hh][rows]
        gblk = gct_ref[pl.ds(pl.multiple_of((head // SUBLANES) * SUBLANES, SUBLANES), SUBLANES), rows]
        grow = jnp.sum(jnp.where(sub8 == (head & (SUBLANES - 1)), gblk, 0.0), axis=0, keepdims=True)
        eg = jnp.exp(gcol)
        kb = kn * bcol
        qn_l.append(qn)
        kn_l.append(kn)
        kb_l.append(kb)
        eg_l.append(eg)
        gcol_l.append(gcol)
        dec_l.append(jnp.exp(jnp.minimum(gcol - grow, 0.0)))
        rhs_l.append(jnp.concatenate([vh * bcol, kb * eg], axis=1))

    a_l, qk_l = [], []
    for qn, kn, kb, decay in zip(qn_l, kn_l, kb_l, dec_l):
        knb = kn.astype(BF16)
        a_l.append(_dot_nt(kb.astype(BF16), knb) * jnp.where(c128 < r128, decay, 0.0))
        qk_l.append(_dot_nt(qn.astype(BF16), knb) * jnp.where(c128 <= r128, decay, 0.0))
    tinv_l = [x.astype(BF16) for x in _unit_lower_inverse_minus_identity(a_l)]
    sol_l = [rhs + _dot(tinv, rhs.astype(BF16)) for tinv, rhs in zip(tinv_l, rhs_l)]
    res_l = []
    for a, sol, rhs in zip(a_l, sol_l, rhs_l):
        a_sol = _dot(a.astype(BF16), sol.astype(BF16))
        res_l.append((rhs - sol) - a_sol)
    sol_l = [(sol + res + _dot(tinv, res.astype(BF16))).astype(BF16) for sol, res, tinv in zip(sol_l, res_l, tinv_l)]
    lhs_l, o_add_l, s_add_l, glast_l = [], [], [], []
    for qn, kn, eg, gcol, qk, sol_b in zip(qn_l, kn_l, eg_l, gcol_l, qk_l, sol_l):
        glast = gcol[CHUNK - 1:CHUNK, :]
        ktail = kn * jnp.exp(glast - gcol)
        kt_sol = _dot(ktail.T.astype(BF16), sol_b)
        qk_sol = _dot(qk.astype(BF16), sol_b)
        lhs_l.append(jnp.concatenate([qn * eg - qk_sol[:, HEAD_DIM:], kt_sol[:, HEAD_DIM:]], axis=0).astype(BF16))
        o_add_l.append(qk_sol[:, :HEAD_DIM])
        s_add_l.append(kt_sol[:, :HEAD_DIM])
        glast_l.append(jnp.exp(glast))
    states = [s_ref[hh] for hh in range(hg)]
    for c in range(n_chunks):
        for hh in range(hg):
            i = hh * n_chunks + c
            rows = slice(c * CHUNK, (c + 1) * CHUNK)
            lanes = slice(hh * HEAD_DIM, (hh + 1) * HEAD_DIM)
            on_s = _dot(lhs_l[i], states[hh].astype(BF16))
            o = on_s[:CHUNK] + o_add_l[i]
            states[hh] = states[hh] * glast_l[i] - on_s[CHUNK:] + s_add_l[i]
            o_ref[rows, lanes] = (_rms(o, gn_ref[...]) * _silu(z_ref[rows, lanes])).astype(BF16)
    for hh in range(hg):
        s_ref[hh] = states[hh]

    @pl.when(t == pl.num_programs(2) - 1)
    def _():
        ssm_ref[0, 0] = s_ref[...]


def _gdn_prompt(proj, ab, conv_w, alog_pad, dtb_pad, gdn_norm, *, batch, seq, hg, tt):
    nt = seq // tt
    w = hg * HEAD_DIM
    body = functools.partial(_gdn_prompt_body, hg=hg, tt=tt)

    def col_spec(col0):
        return pl.BlockSpec((tt, w), lambda b, h, t: (b * nt + t, col0 // w + h))

    def cw_spec(part):
        return pl.BlockSpec((CONV_WIDTH, w), lambda b, h, t: (0, part * GDN_DIM // w + h))

    small = pl.BlockSpec((1, LANES), lambda b, h, t: (0, 0))
    return pl.pallas_call(
        body,
        grid=(batch, HEADS // hg, nt),
        in_specs=[col_spec(C_Q), col_spec(C_K), col_spec(C_V), col_spec(C_Z),
                  pl.BlockSpec((tt, LANES), lambda b, h, t: (b * nt + t, 0)),
                  cw_spec(0), cw_spec(1), cw_spec(2), small, small, small],
        out_specs=[
            pl.BlockSpec((tt, w), lambda b, h, t: (b * nt + t, h)),
            pl.BlockSpec((1, 1, hg, HEAD_DIM, HEAD_DIM), lambda b, h, t: (0, b, h, 0, 0)),
        ],
        out_shape=[
            jax.ShapeDtypeStruct((batch * seq, GDN_DIM), BF16),
            jax.ShapeDtypeStruct((1, batch, HEADS, HEAD_DIM, HEAD_DIM), F32),
        ],
        scratch_shapes=[
            pltpu.VMEM((SUBLANES + tt, 3 * w), F32),
            pltpu.VMEM((hg, HEAD_DIM, HEAD_DIM), F32),
            pltpu.VMEM((LANES, tt), F32),
        ],
        compiler_params=_cparams("parallel", "parallel", "arbitrary"),
        name="gdn_prompt",
    )(proj, proj, proj, proj, ab, conv_w, conv_w, conv_w, alog_pad, dtb_pad, gdn_norm)


def _gdn_sample_prep_body(q_ref, k_ref, v_ref, ab_ref, st_ref, cw_ref, alog_ref, dtb_ref,
                          qn_ref, kn_ref, vc_ref, eg_ref, beta_ref, qk_ref):
    def conv(part, u_ref):
        cols = slice(part * GDN_DIM, (part + 1) * GDN_DIM)
        acc = u_ref[...] * cw_ref[CONV_BUF:CONV_WIDTH, cols]
        for i in range(CONV_BUF):
            acc = acc + st_ref[i, :, cols] * cw_ref[i:i + 1, cols]
        return _silu(acc)

    qc = conv(0, q_ref)
    kc = conv(1, k_ref)
    vc_ref[...] = conv(2, v_ref)
    ab = ab_ref[...]
    g_all = -jnp.exp(alog_ref[...]) * _softplus(ab + dtb_ref[...])
    eg_all = jnp.exp(g_all)
    beta_all = jax.nn.sigmoid(ab)
    for h in range(HEADS):
        lanes = slice(h * HEAD_DIM, (h + 1) * HEAD_DIM)
        qh, kh = qc[:, lanes], kc[:, lanes]
        qn = qh * lax.rsqrt(jnp.sum(qh * qh, axis=-1, keepdims=True) + EPS) * (HEAD_DIM ** -0.5)
        kn = kh * lax.rsqrt(jnp.sum(kh * kh, axis=-1, keepdims=True) + EPS)
        qn_ref[:, lanes] = qn
        kn_ref[:, lanes] = kn
        shape = qn.shape
        qk_ref[:, lanes] = jnp.broadcast_to(jnp.sum(qn * kn, axis=-1, keepdims=True), shape)
        eg_ref[:, lanes] = jnp.broadcast_to(eg_all[:, h:h + 1], shape)
        beta_ref[:, lanes] = jnp.broadcast_to(beta_all[:, HEADS + h:HEADS + h + 1], shape)


def _gdn_sample_prep(proj, ab, conv_state_t, conv_w, alog_pad, dtb_pad):
    nb = proj.shape[0]

    def col_spec(col0):
        return pl.BlockSpec((nb, GDN_DIM), lambda i: (0, col0 // GDN_DIM))

    small = pl.BlockSpec((1, LANES), lambda i: (0, 0))
    full = pl.BlockSpec((nb, GDN_DIM), lambda i: (0, 0))
    return pl.pallas_call(
        _gdn_sample_prep_body,
        grid=(1,),
        in_specs=[col_spec(C_Q), col_spec(C_K), col_spec(C_V),
                  pl.BlockSpec((nb, LANES), lambda i: (0, 0)),
                  pl.BlockSpec((CONV_BUF, nb, QKV_DIM), lambda i: (0, 0, 0)),
                  pl.BlockSpec((CONV_WIDTH, QKV_DIM), lambda i: (0, 0)),
                  small, small],
        out_specs=[full] * 6,
        out_shape=[jax.ShapeDtypeStruct((nb, GDN_DIM), F32)] * 6,
        compiler_params=_cparams("arbitrary"),
        name="gdn_sample_prep",
    )(proj, proj, proj, ab, conv_state_t, conv_w, alog_pad, dtb_pad)


def _gdn_sample_step_body(qn_ref, kn_ref, vc_ref, eg_ref, beta_ref, qk_ref, z_ref, gn_ref, s_ref,
                          o_ref, snew_ref, *, bb):
    pad = jnp.zeros((LANES - HEADS, HEAD_DIM), F32)
    for b in range(bb):
        k16, q16 = kn_ref[b], qn_ref[b]
        cols = jnp.concatenate([k16, pad], axis=0).T
        kq = jnp.concatenate([k16, q16], axis=0).astype(BF16)
        outs = []
        for h in range(HEADS):
            kcol = cols[:, h:h + 1]
            s = s_ref[0, b, h]
            kq_s = _dot(kq, s.astype(BF16))
            k_s = kq_s[h:h + 1]
            q_s = kq_s[HEADS + h:HEADS + h + 1]
            eg = eg_ref[b, h:h + 1, :]
            v_new = beta_ref[b, h:h + 1, :] * (vc_ref[b, h:h + 1, :] - eg * k_s)
            snew_ref[0, b, h] = eg * s + kcol * v_new
            outs.append(eg * q_s + qk_ref[b, h:h + 1, :] * v_new)
        o = jnp.concatenate(outs, axis=0)
        o_ref[b] = (_rms(o, gn_ref[...]) * _silu(z_ref[b])).astype(BF16)


def _gdn_sample_step(qn, kn, vc, eg, beta, qk, z, gdn_norm, state_ssm, *, bb):
    nb = qn.shape[0]
    body = functools.partial(_gdn_sample_step_body, bb=bb)
    vec = pl.BlockSpec((bb, HEADS, HEAD_DIM), lambda i: (i, 0, 0))
    st = pl.BlockSpec((1, bb, HEADS, HEAD_DIM, HEAD_DIM), lambda i: (0, i, 0, 0, 0))
    return pl.pallas_call(
        body,
        grid=(nb // bb,),
        in_specs=[vec] * 7 + [pl.BlockSpec((1, LANES), lambda i: (0, 0)), st],
        out_specs=[vec, st],
        out_shape=[jax.ShapeDtypeStruct((nb, HEADS, HEAD_DIM), BF16),
                   jax.ShapeDtypeStruct(state_ssm.shape, F32)],
        compiler_params=_cparams("parallel"),
        name="gdn_sample_step",
    )(qn, kn, vc, eg, beta, qk, z, gdn_norm, state_ssm)


def _merge_body(pool_ref, gdn_ref, wpu_ref, wgu_ref, gp_ref, gg_ref, o_ref):
    o_ref[...] = (jax.nn.sigmoid(gp_ref[...]) * _dot(pool_ref[...], wpu_ref[...])
                  + jax.nn.sigmoid(gg_ref[...]) * _dot(gdn_ref[...], wgu_ref[...])).astype(BF16)


def _merge(proj, pool_out, gdn_out, w_pool_up, w_gdn_up, *, tm, tn):
    m = pool_out.shape[0]
    nj = D_MODEL // tn
    return pl.pallas_call(
        _merge_body,
        grid=(m // tm, nj),
        in_specs=[
            pl.BlockSpec((tm, POOL_DIM), lambda i, j: (i, 0)),
            pl.BlockSpec((tm, GDN_DIM), lambda i, j: (i, 0)),
            pl.BlockSpec((POOL_DIM, tn), lambda i, j: (0, j)),
            pl.BlockSpec((GDN_DIM, tn), lambda i, j: (0, j)),
            pl.BlockSpec((tm, tn), lambda i, j: (i, C_GP // tn + j)),
            pl.BlockSpec((tm, tn), lambda i, j: (i, C_GG // tn + j)),
        ],
        out_specs=pl.BlockSpec((tm, tn), lambda i, j: (i, j)),
        out_shape=jax.ShapeDtypeStruct((m, D_MODEL), BF16),
        compiler_params=_cparams("parallel", "parallel"),
        name="merge",
    )(pool_out, gdn_out, w_pool_up, w_gdn_up, proj, proj)


def _outproj_body(a_ref, w_ref, x_ref, o_ref):
    o_ref[...] = x_ref[...] + _dot(a_ref[...], w_ref[...])


def _outproj(merged, x, w_o, *, tm, tn):
    m = x.shape[0]
    return pl.pallas_call(
        _outproj_body,
        grid=(m // tm, D_MODEL // tn),
        in_specs=[
            pl.BlockSpec((tm, D_MODEL), lambda i, j: (i, 0)),
            pl.BlockSpec((D_MODEL, tn), lambda i, j: (0, j)),
            pl.BlockSpec((tm, tn), lambda i, j: (i, j)),
        ],
        out_specs=pl.BlockSpec((tm, tn), lambda i, j: (i, j)),
        out_shape=jax.ShapeDtypeStruct((m, D_MODEL), F32),
        compiler_params=_cparams("parallel", "parallel"),
        name="outproj",
    )(merged, w_o, x)


def _ffn_body(x_ref, gain_ref, wg_ref, wu_ref, wd_ref, o_ref, h_ref):
    f = pl.program_id(1)

    def hidden_slice():
        h = h_ref[...]
        act = (_silu(_dot(h, wg_ref[...].astype(BF16))) * _dot(h, wu_ref[...].astype(BF16))).astype(BF16)
        return _dot(act, wd_ref[...].astype(BF16))

    @pl.when(f == 0)
    def _():
        x = x_ref[...]
        h_ref[...] = _rms(x, gain_ref[...]).astype(BF16)
        o_ref[...] = x + hidden_slice()

    @pl.when(f > 0)
    def _():
        o_ref[...] += hidden_slice()


def _ffn(x, gain, w_gate_up, w_down, *, tm, tf):
    m = x.shape[0]
    nf = D_FF // tf
    return pl.pallas_call(
        _ffn_body,
        grid=(m // tm, nf),
        in_specs=[
            pl.BlockSpec((tm, D_MODEL), lambda i, f: (i, 0)),
            pl.BlockSpec((1, D_MODEL), lambda i, f: (0, 0)),
            pl.BlockSpec((D_MODEL, tf), lambda i, f: (0, f)),
            pl.BlockSpec((D_MODEL, tf), lambda i, f: (0, nf + f)),
            pl.BlockSpec((tf, D_MODEL), lambda i, f: (f, 0)),
        ],
        out_specs=pl.BlockSpec((tm, D_MODEL), lambda i, f: (i, 0)),
        out_shape=jax.ShapeDtypeStruct((m, D_MODEL), F32),
        scratch_shapes=[pltpu.VMEM((tm, D_MODEL), BF16)],
        compiler_params=_cparams("parallel", "arbitrary"),
        name="ffn",
    )(x, gain, w_gate_up, w_gate_up, w_down)


def _final_body(x_ref, p_ref, wple_ref, wpg_ref, gain_ref, o_ref):
    x = x_ref[...]
    gate = jax.nn.sigmoid(_dot(x.astype(BF16), wpg_ref[...]))
    emb = _dot(p_ref[...].astype(BF16), wple_ref[...])
    o_ref[...] = _rms(x + emb * gate, gain_ref[...])


def _final(x, p, w_ple, w_ple_gate, gain, *, tm):
    m = x.shape[0]
    return pl.pallas_call(
        _final_body,
        grid=(m // tm,),
        in_specs=[
            pl.BlockSpec((tm, D_MODEL), lambda i: (i, 0)),
            pl.BlockSpec((tm, PLE_DIM), lambda i: (i, 0)),
            pl.BlockSpec((PLE_DIM, D_MODEL), lambda i: (0, 0), pipeline_mode=pl.Buffered(1)),
            pl.BlockSpec((D_MODEL, D_MODEL), lambda i: (0, 0), pipeline_mode=pl.Buffered(1)),
            pl.BlockSpec((1, D_MODEL), lambda i: (0, 0)),
        ],
        out_specs=pl.BlockSpec((tm, D_MODEL), lambda i: (i, 0)),
        out_shape=jax.ShapeDtypeStruct((m, D_MODEL), F32),
        compiler_params=_cparams("parallel"),
        name="final",
    )(x, p, w_ple, w_ple_gate, gain)


def _tile_rows(m, want):
    return want if m % want == 0 else m


def _dense_tail(x, p, proj, pool_out, gdn_out, wts):
    m = x.shape[0]
    tm = _tile_rows(m, TM_DENSE)
    merged = _merge(proj, pool_out, gdn_out, wts["w_pool_up"], wts["w_gdn_up"], tm=tm, tn=TN_DENSE)
    x = _outproj(merged, x, wts["w_o"], tm=tm, tn=TN_DENSE)
    x = _ffn(x, wts["norm_ffn"], wts["w_gate_up"], wts["w_down"], tm=tm,
             tf=TF_FFN if tm == TM_DENSE else TF_FFN_SMALL)
    return _final(x, p, wts["w_ple"], wts["w_ple_gate"], wts["norm_final"], tm=tm)


def kernel(x_prompt, x_sample, p_prompt, p_sample, state_pool, state_conv, state_ssm, norm_mix, w_in, pool_w,
           pool_scale, conv_w, a_log, dt_bias, gdn_norm, w_pool_up, w_gdn_up, w_o, norm_ffn, w_gate_up, w_down,
           w_ple, w_ple_gate, norm_final):
    batch, seq, _ = x_prompt.shape
    nb = x_sample.shape[0]
    w_in_t = jnp.swapaxes(w_in[0], 0, 1)
    wts = dict(
        w_pool_up=w_pool_up[0].astype(BF16), w_gdn_up=w_gdn_up[0].astype(BF16), w_o=w_o[0].astype(BF16),
        norm_ffn=norm_ffn, w_gate_up=w_gate_up[0], w_down=w_down[0],
        w_ple=w_ple[0].astype(BF16), w_ple_gate=w_ple_gate[0].astype(BF16), norm_final=norm_final[None, :],
    )
    pool_w_b = pool_w[0].astype(BF16)
    lane_pad = ((0, 0), (0, LANES - HEADS))
    alog_pad = jnp.pad(a_log, lane_pad)
    dtb_pad = jnp.pad(dt_bias, lane_pad)
    gn = gdn_norm

    mp = batch * seq
    xp = x_prompt.reshape(mp, D_MODEL)
    xs = x_sample.reshape(nb, D_MODEL)
    h_p, ab_p = _norm_in(xp, norm_mix, w_in_t, tm=_tile_rows(mp, TM_NORM))
    h_s, ab_s = _norm_in(xs, norm_mix, w_in_t, tm=nb)
    proj_p, proj_s = _inproj(h_p, h_s, w_in_t, tm=_tile_rows(mp, TM_DENSE), tn=TN_INPROJ)

    pool_out_p = _pool_prompt(proj_p, pool_w_b, pool_scale, batch=batch, seq=seq, tt=_tile_rows(seq, TT_POOL))
    gdn_out_p, ssm_p = _gdn_prompt(proj_p, ab_p, conv_w[0], alog_pad, dtb_pad, gn, batch=batch, seq=seq,
                                   hg=GDN_HEADS_PER_STEP, tt=_tile_rows(seq, TT_GDN))
    y_p = _dense_tail(xp, p_prompt[0].reshape(mp, PLE_DIM), proj_p, pool_out_p, gdn_out_p, wts)
    proj_p3 = proj_p.reshape(batch, seq, C_ALL)
    pool_p = proj_p3[:, seq - POOL_BUF:, C_POOL:C_END][None]
    conv_p = proj_p3[:, seq - CONV_BUF:, C_Q:C_Z][None]

    pool_out_s = _pool_sample(proj_s, jnp.swapaxes(state_pool[0], 0, 1), pool_w_b, pool_scale)
    prep = _gdn_sample_prep(proj_s, ab_s, jnp.swapaxes(state_conv[0], 0, 1), conv_w[0], alog_pad, dtb_pad)
    to_heads = lambda a: a.reshape(nb, HEADS, HEAD_DIM)
    gdn_out_s, ssm_s = _gdn_sample_step(*[to_heads(a) for a in prep], to_heads(proj_s[:, C_Z:C_POOL]), gn,
                                        state_ssm, bb=_tile_rows(nb, BB_SAMPLE))
    y_s = _dense_tail(xs, p_sample[0].reshape(nb, PLE_DIM), proj_s, pool_out_s, gdn_out_s.reshape(nb, GDN_DIM), wts)
    pool_s = jnp.concatenate([state_pool[0][:, 1:], proj_s[:, None, C_POOL:C_END]], axis=1)[None]
    conv_s = jnp.concatenate([state_conv[0][:, 1:], proj_s[:, None, C_Q:C_Z]], axis=1)[None]

    return (y_p.reshape(batch, seq, D_MODEL), y_s.reshape(nb, 1, D_MODEL), pool_p, conv_p, ssm_p,
            pool_s, conv_s, ssm_s)
```
